```python
import math
import jax
import jax.numpy as jnp
from jax import lax
import numpy as np

D_MODEL = 1024
BATCH = 2
SEQ = 8192
DEPTH = 4
DEC_BATCH = 32
DEC_SEQ = 1
PAST_LEN = 8192
PAGE_SIZE = 128

N_MIXERS = 3
N_GMLP_LAYERS = (DEPTH + 2) // 3
N_NSA_LAYERS = (DEPTH + 1) // 3
N_SSM_LAYERS = DEPTH // 3

D_FF = ((8 * D_MODEL + 3 * 256 - 1) // (3 * 256)) * 256

GMLP_HALF = D_MODEL
GMLP_GROUPS = 8
GMLP_GROUP_WIDTH = GMLP_HALF // GMLP_GROUPS
GMLP_CHUNK = 128

NSA_HEAD_DIM = 64
NSA_HEADS = D_MODEL // NSA_HEAD_DIM
NSA_KV_HEADS = 4
NSA_REP = NSA_HEADS // NSA_KV_HEADS
NSA_BLOCK = 64
NSA_TOPK = 16
NSA_WINDOW = 512
NSA_CMP_HIDDEN = 2 * NSA_HEAD_DIM
NSA_Q_BLOCK = 64
NSA_W_BLOCK = 128
NSA_Q_COLS = NSA_HEADS * NSA_HEAD_DIM
NSA_KV_COLS = 2 * NSA_KV_HEADS * NSA_HEAD_DIM
NSA_IN_COLS = NSA_Q_COLS + 3 * NSA_KV_COLS + 3 * NSA_HEADS
SEL_FORCE = 1.0e4
SEL_MASKED = -1.0

SSM_GROUP_WIDTH = 16
SSM_GROUPS = D_MODEL // SSM_GROUP_WIDTH
SSM_STATE = 64
SSM_CHUNK = 256
DT_MIN = 1.0e-3
DT_MAX = 1.0e-1

RMS_EPS = 1.0e-6
LN_EPS = 1.0e-5

kernel_name = "hybrid_gmlp_nsa_s5_decoder_step"


def rmsnorm(x, g):
    xf = x.astype(jnp.float32)
    y = xf * lax.rsqrt(jnp.mean(xf * xf, axis=-1, keepdims=True) + RMS_EPS)
    return (y * g.astype(jnp.float32)).astype(x.dtype)


def adaln(c, w, b):
    m = (jax.nn.silu(c) @ w + b)[:, None, :]
    return jnp.split(m, 6, axis=-1)


def modulate(x, g, shift, scale):
    return rmsnorm(x, g) * (1 + scale) + shift


def swiglu(h, w_gate, w_up, w_down):
    return (jax.nn.silu(h @ w_gate) * (h @ w_up)) @ w_down


def masked_softmax(s, mask):
    s = jnp.where(mask, s, -1e30)
    m = jnp.max(s, axis=-1, keepdims=True)
    e = jnp.where(mask, jnp.exp(s - m), 0.0)
    return e / jnp.maximum(jnp.sum(e, axis=-1, keepdims=True), 1e-30)


def gmlp_mixer(h, w_in, b_in, ln_g, ln_b, w_s, b_s, w_out):
    B, T, _ = h.shape
    z = jax.nn.gelu(h @ w_in + b_in)
    u, v = jnp.split(z, 2, axis=-1)
    vf = v.astype(jnp.float32)
    mu = jnp.mean(vf, axis=-1, keepdims=True)
    var = jnp.mean(jnp.square(vf - mu), axis=-1, keepdims=True)
    v = ((vf - mu) * lax.rsqrt(var + LN_EPS) * ln_g.astype(jnp.float32) + ln_b.astype(jnp.float32)).astype(h.dtype)
    pad = (-T) % GMLP_CHUNK
    n_chunks = (T + pad) // GMLP_CHUNK
    vc = jnp.pad(v, ((0, 0), (0, pad), (0, 0))).reshape(B, n_chunks, GMLP_CHUNK, GMLP_GROUPS, GMLP_GROUP_WIDTH)
    causal = jnp.tril(jnp.ones((GMLP_CHUNK, GMLP_CHUNK), dtype=bool))
    w = jnp.where(causal, w_s, 0.0)
    mixed = jnp.einsum("hts,bcshe->bcthe", w, vc) + b_s.T[:, :, None]
    mixed = mixed.reshape(B, n_chunks * GMLP_CHUNK, GMLP_HALF)[:, :T]
    return (u * mixed) @ w_out, v


def nsa_project(h, w_in):
    B, T, _ = h.shape
    z = h @ w_in
    cuts = np.cumsum([NSA_Q_COLS, NSA_KV_COLS, NSA_KV_COLS, NSA_KV_COLS]).tolist()
    q, kv_c, kv_s, kv_w, g = jnp.split(z, cuts, axis=-1)
    kv_shape = (B, T, 2, NSA_KV_HEADS, NSA_HEAD_DIM)
    gates = jax.nn.sigmoid(g.astype(jnp.float32)).reshape(B, T, 3, NSA_KV_HEADS, NSA_REP)
    return (q.reshape(B, T, NSA_KV_HEADS, NSA_REP, NSA_HEAD_DIM), kv_c.reshape(kv_shape),
            kv_s.reshape(kv_shape), kv_w.reshape(kv_shape), gates)


def nsa_compress(kv, w1, w2, pe):
    B, L = kv.shape[:2]
    nb = L // NSA_BLOCK
    blk = kv[:, :nb * NSA_BLOCK].reshape(B, nb, NSA_BLOCK, 2, NSA_KV_HEADS, NSA_HEAD_DIM)
    pe_bias = jnp.einsum("zld,zlde->ze", pe, w1)
    hid = jax.nn.silu(jnp.einsum("bnlzgd,zlde->bnzge", blk, w1) + pe_bias[:, None, :])
    return jnp.einsum("bnzge,zed->bnzgd", hid, w2)


def nsa_cmp_slc(q, kv_c, q_pos0, nb_sel, gather_sel, w1, w2, pe):
    B, Tq = q.shape[:2]
    scale = NSA_HEAD_DIM ** -0.5
    qf = q.astype(jnp.float32)
    cmp = nsa_compress(kv_c, w1, w2, pe).astype(jnp.float32)
    kc, vc = cmp[:, :, 0], cmp[:, :, 1]
    nb_cmp = kc.shape[1]
    t = q_pos0 + jnp.arange(Tq)
    s = jnp.einsum("bqgrd,bngd->bqgrn", qf, kc) * scale
    mask_c = ((jnp.arange(nb_cmp) + 1) * NSA_BLOCK <= t[:, None] + 1)[None, :, None, None, :]
    p = masked_softmax(s, mask_c)
    o_cmp = jnp.einsum("bqgrn,bngd->bqgrd", p, vc)
    imp = jnp.pad(p.sum(axis=3), ((0, 0), (0, 0), (0, 0), (0, nb_sel - nb_cmp)))
    blk = jnp.arange(nb_sel)[None, :]
    jt = (t // NSA_BLOCK)[:, None]
    forced = (blk == 0) | (blk == jt) | (blk == jt - 1)
    score = jnp.where((blk <= jt)[:, None, :], jnp.where(forced[:, None, :], SEL_FORCE, imp), SEL_MASKED)
    k_sel = min(NSA_TOPK, nb_sel)
    top_s, top_i = lax.top_k(score, k_sel)
    valid = top_s > 0.5 * SEL_MASKED
    qb = NSA_Q_BLOCK if Tq % NSA_Q_BLOCK == 0 else Tq
    nq = Tq // qb

    def to_blocks(a):
        return jnp.moveaxis(a.reshape((B, nq, qb) + a.shape[2:]), 1, 0)

    def sel_block(args):
        q_b, i_b, v_b, t_b = args
        kv = gather_sel(i_b).astype(jnp.float32)
        kv = kv.reshape(B, qb, NSA_KV_HEADS, k_sel * NSA_BLOCK, 2, NSA_HEAD_DIM)
        kpos = (i_b[..., None] * NSA_BLOCK + jnp.arange(NSA_BLOCK)).reshape(B, qb, NSA_KV_HEADS, k_sel * NSA_BLOCK)
        mask = (jnp.repeat(v_b, NSA_BLOCK, axis=-1) & (kpos <= t_b[None, :, None, None]))[:, :, :, None, :]
        sb = jnp.einsum("bqgrd,bqgsd->bqgrs", q_b, kv[..., 0, :]) * scale
        return jnp.einsum("bqgrs,bqgsd->bqgrd", masked_softmax(sb, mask), kv[..., 1, :])

    o_slc = lax.map(sel_block, (to_blocks(qf), to_blocks(top_i), to_blocks(valid), t.reshape(nq, qb)))
    o_slc = jnp.moveaxis(o_slc, 0, 1).reshape(B, Tq, NSA_KV_HEADS, NSA_REP, NSA_HEAD_DIM)
    return o_cmp, o_slc


def nsa_window_block(q_b, kv_b, t_q, t_k):
    kvf = kv_b.astype(jnp.float32)
    mask = (t_k[None, :] <= t_q[:, None]) & (t_k[None, :] >= t_q[:, None] - NSA_WINDOW) & (t_k[None, :] >= 0)
    s = jnp.einsum("bqgrd,bkgd->bqgrk", q_b, kvf[:, :, 0]) * NSA_HEAD_DIM ** -0.5
    p = masked_softmax(s, mask[None, :, None, None, :])
    return jnp.einsum("bqgrk,bkgd->bqgrd", p, kvf[:, :, 1])


def nsa_window_prompt(q, kv_w):
    B, T = q.shape[:2]
    qw = NSA_W_BLOCK if T % NSA_W_BLOCK == 0 else T
    n = T // qw
    kpad = jnp.pad(kv_w, ((0, 0), (NSA_WINDOW, 0), (0, 0), (0, 0), (0, 0)))

    def block_fn(args):
        q_b, start = args
        kv_b = lax.dynamic_slice_in_dim(kpad, start, NSA_WINDOW + qw, axis=1)
        t_q = start + jnp.arange(qw)
        t_k = start - NSA_WINDOW + jnp.arange(NSA_WINDOW + qw)
        return nsa_window_block(q_b, kv_b, t_q, t_k)

    q_blocks = jnp.moveaxis(q.astype(jnp.float32).reshape(B, n, qw, NSA_KV_HEADS, NSA_REP, NSA_HEAD_DIM), 1, 0)
    o = lax.map(block_fn, (q_blocks, jnp.arange(n) * qw))
    return jnp.moveaxis(o, 0, 1).reshape(B, T, NSA_KV_HEADS, NSA_REP, NSA_HEAD_DIM)


def nsa_merge(o_cmp, o_slc, o_win, gates, w_out, dtype):
    B, T = o_cmp.shape[:2]
    o = (gates[:, :, 0][..., None] * o_cmp + gates[:, :, 1][..., None] * o_slc
         + gates[:, :, 2][..., None] * o_win)
    return o.reshape(B, T, NSA_Q_COLS).astype(dtype) @ w_out


def nsa_prompt(h, w_in, w1, w2, pe, w_out):
    B, T, _ = h.shape
    q, kv_c, kv_s, kv_w, gates = nsa_project(h, w_in)
    nb = -(-T // NSA_BLOCK)
    blocks = jnp.pad(kv_s, ((0, 0), (0, nb * NSA_BLOCK - T), (0, 0), (0, 0), (0, 0)))
    blocks = blocks.reshape(B, nb, NSA_BLOCK, 2, NSA_KV_HEADS, NSA_HEAD_DIM)
    bi = jnp.arange(B)[:, None, None, None]
    gi = jnp.arange(NSA_KV_HEADS)[None, None, :, None]

    def gather_sel(idx):
        return blocks[bi, idx, :, :, gi]

    o_cmp, o_slc = nsa_cmp_slc(q, kv_c, 0, nb, gather_sel, w1, w2, pe)
    o_win = nsa_window_prompt(q, kv_w)
    y = nsa_merge(o_cmp, o_slc, o_win, gates, w_out, h.dtype)
    keep = min(NSA_WINDOW, T)
    return y, kv_c, kv_s, kv_w[:, T - keep:]


def nsa_sample(h, cache_cmp, cache_slc, win_buf, page_table, layer, w_in, w1, w2, pe, w_out):
    B, T, _ = h.shape
    n_pool = cache_cmp.shape[1]
    n_pages = page_table.shape[1]
    past = n_pages * PAGE_SIZE
    q, kv_c, kv_s, kv_w, gates = nsa_project(h, w_in)
    pool_c = cache_cmp.reshape((-1,) + cache_cmp.shape[2:])
    past_c = pool_c[layer * n_pool + page_table].reshape(B, past, 2, NSA_KV_HEADS, NSA_HEAD_DIM)
    full_c = jnp.concatenate([past_c.astype(kv_c.dtype), kv_c], axis=1)
    nb = -(-(past + T) // NSA_BLOCK)
    nb_past = past // NSA_BLOCK
    nb_new = nb - nb_past
    per_page = PAGE_SIZE // NSA_BLOCK
    pool_s = cache_slc.reshape(-1, NSA_BLOCK, 2, NSA_KV_HEADS, NSA_HEAD_DIM)
    new_blocks = jnp.pad(kv_s, ((0, 0), (0, nb_new * NSA_BLOCK - T), (0, 0), (0, 0), (0, 0)))
    new_blocks = new_blocks.reshape(B, nb_new, NSA_BLOCK, 2, NSA_KV_HEADS, NSA_HEAD_DIM)
    bi = jnp.arange(B)[:, None, None, None]
    gi = jnp.arange(NSA_KV_HEADS)[None, None, :, None]

    def gather_sel(idx):
        is_new = idx >= nb_past
        lp = jnp.minimum(idx // per_page, n_pages - 1)
        phys = (layer * n_pool + page_table[bi, lp]) * per_page + idx % per_page
        from_pool = pool_s[phys, :, :, gi]
        from_new = new_blocks[bi, jnp.clip(idx - nb_past, 0, nb_new - 1), :, :, gi]
        return jnp.where(is_new[..., None, None, None], from_new, from_pool.astype(from_new.dtype))

    o_cmp, o_slc = nsa_cmp_slc(q, full_c, past, nb, gather_sel, w1, w2, pe)
    wb = win_buf.shape[1]
    win = jnp.concatenate([win_buf.astype(kv_w.dtype), kv_w], axis=1)
    t_k = past - wb + jnp.arange(wb + T)
    t_q = past + jnp.arange(T)
    o_win = nsa_window_block(q.astype(jnp.float32), win, t_q, t_k)
    y = nsa_merge(o_cmp, o_slc, o_win, gates, w_out, h.dtype)
    keep = min(NSA_WINDOW, past + T)
    return y, kv_c, kv_s, win[:, wb + T - keep:]


def _complex_affine_combine(e1, e2):
    a1r, a1i, b1r, b1i = e1
    a2r, a2i, b2r, b2i = e2
    return (a2r * a1r - a2i * a1i, a2r * a1i + a2i * a1r,
            a2r * b1r - a2i * b1i + b2r, a2r * b1i + a2i * b1r + b2i)


def ssm_mixer(h, h0, lam_re, lam_im, b_re, b_im, c_re, c_im, d, log_step, w1, b1, w2, b2):
    B, T, _ = h.shape
    f32 = jnp.float32
    u = h.astype(f32).reshape(B, T, SSM_GROUPS, SSM_GROUP_WIDTH)
    lr, li = lam_re.astype(f32), lam_im.astype(f32)
    dt = jnp.exp(log_step.astype(f32))[:, None]
    mag = jnp.exp(lr * dt)
    ab_re, ab_im = mag * jnp.cos(li * dt), mag * jnp.sin(li * dt)
    den = lr * lr + li * li
    f_re = ((ab_re - 1.0) * lr + ab_im * li) / den
    f_im = (ab_im * lr - (ab_re - 1.0) * li) / den
    br, bim = b_re.astype(f32), b_im.astype(f32)
    bb_re = f_re[..., None] * br - f_im[..., None] * bim
    bb_im = f_re[..., None] * bim + f_im[..., None] * br
    cr, ci = c_re.astype(f32), c_im.astype(f32)
    tc = SSM_CHUNK if T % SSM_CHUNK == 0 else T
    uc = jnp.moveaxis(u.reshape(B, T // tc, tc, SSM_GROUPS, SSM_GROUP_WIDTH), 1, 0)

    def step(carry, u_blk):
        hr, hi = carry
        bu_re = jnp.einsum("btgi,gpi->btgp", u_blk, bb_re)
        bu_im = jnp.einsum("btgi,gpi->btgp", u_blk, bb_im)
        a_re = jnp.broadcast_to(ab_re, bu_re.shape)
        a_im = jnp.broadcast_to(ab_im, bu_re.shape)
        pr, pi, sr, si = lax.associative_scan(_complex_affine_combine, (a_re, a_im, bu_re, bu_im), axis=1)
        s_re = pr * hr[:, None] - pi * hi[:, None] + sr
        s_im = pr * hi[:, None] + pi * hr[:, None] + si
        y = jnp.einsum("btgp,gip->btgi", s_re, cr) - jnp.einsum("btgp,gip->btgi", s_im, ci)
        return (s_re[:, -1], s_im[:, -1]), y

    h0f = h0.astype(f32)
    (hr, hi), ys = lax.scan(step, (h0f[..., 0], h0f[..., 1]), uc)
    y = jnp.moveaxis(ys, 0, 1).reshape(B, T, D_MODEL) + d.astype(f32) * u.reshape(B, T, D_MODEL)
    g = jax.nn.gelu(y)
    out = (g @ w1.astype(f32) + b1.astype(f32)) * jax.nn.sigmoid(g @ w2.astype(f32) + b2.astype(f32))
    return out.astype(h.dtype), jnp.stack([hr, hi], axis=-1)


def setup_inputs(seed: int = 0) -> dict:
    keys = iter(jax.random.split(jax.random.key(seed), 48))

    def nrm(shape, scale):
        return jax.random.normal(next(keys), shape, jnp.float32) * scale

    n_pages = PAST_LEN // PAGE_SIZE
    n_used = DEC_BATCH * n_pages
    n_pool = n_used + max(1, n_used // 4)
    win_buf = min(NSA_WINDOW, PAST_LEN)
    kv_row = (2, NSA_KV_HEADS, NSA_HEAD_DIM)
    page_table = jax.random.permutation(next(keys), n_pool)[:n_used].reshape(DEC_BATCH, n_pages).astype(jnp.int32)
    nA, nB, nC = N_GMLP_LAYERS, N_NSA_LAYERS, N_SSM_LAYERS
    lam_im0 = jnp.pi * jnp.arange(SSM_STATE, dtype=jnp.float32)
    return {
        "x_prompt": nrm((BATCH, SEQ, D_MODEL), 1.0),
        "x_sample": nrm((DEC_BATCH, DEC_SEQ, D_MODEL), 1.0),
        "cache_nsa_cmp": nrm((nB, n_pool, PAGE_SIZE) + kv_row, 1.0),
        "cache_nsa_slc": nrm((nB, n_pool, PAGE_SIZE) + kv_row, 1.0),
        "cache_nsa_win": nrm((nB, DEC_BATCH, win_buf) + kv_row, 1.0),
        "state_ssm": nrm((nC, DEC_BATCH, SSM_GROUPS, SSM_STATE, 2), 0.5),
        "page_table": page_table,
        "c_prompt": nrm((BATCH, D_MODEL), 1.0),
        "c_sample": nrm((DEC_BATCH, D_MODEL), 1.0),
        "w_mod": nrm((DEPTH, D_MODEL, 6 * D_MODEL), 0.5 * D_MODEL ** -0.5),
        "b_mod": nrm((DEPTH, 6 * D_MODEL), 0.02),
        "norm_g": 1.0 + nrm((DEPTH, 4, D_MODEL), 0.05),
        "ffn_w_gate": nrm((DEPTH, D_MODEL, D_FF), D_MODEL ** -0.5),
        "ffn_w_up": nrm((DEPTH, D_MODEL, D_FF), D_MODEL ** -0.5),
        "ffn_w_down": nrm((DEPTH, D_FF, D_MODEL), D_FF ** -0.5),
        "gmlp_w_in": nrm((nA, D_MODEL, 2 * GMLP_HALF), D_MODEL ** -0.5),
        "gmlp_b_in": nrm((nA, 2 * GMLP_HALF), 0.02),
        "gmlp_ln_g": 1.0 + nrm((nA, GMLP_HALF), 0.05),
        "gmlp_ln_b": nrm((nA, GMLP_HALF), 0.02),
        "gmlp_w_s": nrm((nA, GMLP_GROUPS, GMLP_CHUNK, GMLP_CHUNK), GMLP_CHUNK ** -0.5),
        "gmlp_b_s": 1.0 + nrm((nA, GMLP_GROUPS, GMLP_CHUNK), 0.1),
        "gmlp_w_out": nrm((nA, GMLP_HALF, D_MODEL), GMLP_HALF ** -0.5),
        "nsa_w_in": nrm((nB, D_MODEL, NSA_IN_COLS), D_MODEL ** -0.5),
        "nsa_w_cmp1": nrm((nB, 2, NSA_BLOCK, NSA_HEAD_DIM, NSA_CMP_HIDDEN), (NSA_BLOCK * NSA_HEAD_DIM) ** -0.5),
        "nsa_w_cmp2": nrm((nB, 2, NSA_CMP_HIDDEN, NSA_HEAD_DIM), NSA_CMP_HIDDEN ** -0.5),
        "nsa_pe_cmp": nrm((nB, 2, NSA_BLOCK, NSA_HEAD_DIM), 0.1),
        "nsa_w_out": nrm((nB, NSA_Q_COLS, D_MODEL), NSA_Q_COLS ** -0.5),
        "ssm_lambda_re": -0.5 + nrm((nC, SSM_GROUPS, SSM_STATE), 0.01),
        "ssm_lambda_im": lam_im0 + nrm((nC, SSM_GROUPS, SSM_STATE), 0.01),
        "ssm_b_re": nrm((nC, SSM_GROUPS, SSM_STATE, SSM_GROUP_WIDTH), (2 * SSM_GROUP_WIDTH) ** -0.5),
        "ssm_b_im": nrm((nC, SSM_GROUPS, SSM_STATE, SSM_GROUP_WIDTH), (2 * SSM_GROUP_WIDTH) ** -0.5),
        "ssm_c_re": nrm((nC, SSM_GROUPS, SSM_GROUP_WIDTH, SSM_STATE), (2 * SSM_STATE) ** -0.5),
        "ssm_c_im": nrm((nC, SSM_GROUPS, SSM_GROUP_WIDTH, SSM_STATE), (2 * SSM_STATE) ** -0.5),
        "ssm_d": nrm((nC, D_MODEL), 0.5),
        "ssm_log_step": jax.random.uniform(next(keys), (nC, SSM_GROUPS), jnp.float32,
                                           minval=math.log(DT_MIN), maxval=math.log(DT_MAX)),
        "ssm_w_glu1": nrm((nC, D_MODEL, D_MODEL), D_MODEL ** -0.5),
        "ssm_b_glu1": nrm((nC, D_MODEL), 0.02),
        "ssm_w_glu2": nrm((nC, D_MODEL, D_MODEL), D_MODEL ** -0.5),
        "ssm_b_glu2": nrm((nC, D_MODEL), 0.02),
    }


def reference(x_prompt, x_sample, cache_nsa_cmp, cache_nsa_slc, cache_nsa_win, state_ssm, page_table,
              c_prompt, c_sample, w_mod, b_mod, norm_g, ffn_w_gate, ffn_w_up, ffn_w_down,
              gmlp_w_in, gmlp_b_in, gmlp_ln_g, gmlp_ln_b, gmlp_w_s, gmlp_b_s, gmlp_w_out,
              nsa_w_in, nsa_w_cmp1, nsa_w_cmp2, nsa_pe_cmp, nsa_w_out,
              ssm_lambda_re, ssm_lambda_im, ssm_b_re, ssm_b_im, ssm_c_re, ssm_c_im, ssm_d, ssm_log_step,
              ssm_w_glu1, ssm_b_glu1, ssm_w_glu2, ssm_b_glu2):
    xp, xs = x_prompt, x_sample
    cmp_p, cmp_s, slc_p, slc_s, win_p, win_s = [], [], [], [], [], []
    ssm_p, ssm_s, gv_s = [], [], []
    for i in range(DEPTH):
        j = i // N_MIXERS
        mp = adaln(c_prompt, w_mod[i], b_mod[i])
        ms = adaln(c_sample, w_mod[i], b_mod[i])
        hp = modulate(xp, norm_g[i, 0], mp[0], mp[1])
        hs = modulate(xs, norm_g[i, 0], ms[0], ms[1])
        if i % N_MIXERS == 0:
            gw = (gmlp_w_in[j], gmlp_b_in[j], gmlp_ln_g[j], gmlp_ln_b[j], gmlp_w_s[j], gmlp_b_s[j], gmlp_w_out[j])
            yp, _ = gmlp_mixer(hp, *gw)
            ys, v_new = gmlp_mixer(hs, *gw)
            gv_s.append(v_new)
        elif i % N_MIXERS == 1:
            nw = (nsa_w_in[j], nsa_w_cmp1[j], nsa_w_cmp2[j], nsa_pe_cmp[j], nsa_w_out[j])
            yp, kc_p, ks_p, kw_p = nsa_prompt(hp, *nw)
            ys, kc_s, ks_s, kw_s = nsa_sample(hs, cache_nsa_cmp, cache_nsa_slc, cache_nsa_win[j], page_table, j, *nw)
            cmp_p.append(kc_p)
            cmp_s.append(kc_s)
            slc_p.append(ks_p)
            slc_s.append(ks_s)
            win_p.append(kw_p)
            win_s.append(kw_s)
        else:
            sw = (ssm_lambda_re[j], ssm_lambda_im[j], ssm_b_re[j], ssm_b_im[j], ssm_c_re[j], ssm_c_im[j],
                  ssm_d[j], ssm_log_step[j], ssm_w_glu1[j], ssm_b_glu1[j], ssm_w_glu2[j], ssm_b_glu2[j])
            h0 = jnp.zeros((hp.shape[0], SSM_GROUPS, SSM_STATE, 2), jnp.float32)
            yp, st_p = ssm_mixer(hp, h0, *sw)
            ys, st_s = ssm_mixer(hs, state_ssm[j], *sw)
            ssm_p.append(st_p)
            ssm_s.append(st_s)
        xp = xp + mp[2] * rmsnorm(yp, norm_g[i, 1])
        xs = xs + ms[2] * rmsnorm(ys, norm_g[i, 1])
        hp = modulate(xp, norm_g[i, 2], mp[3], mp[4])
        hs = modulate(xs, norm_g[i, 2], ms[3], ms[4])
        xp = xp + mp[5] * rmsnorm(swiglu(hp, ffn_w_gate[i], ffn_w_up[i], ffn_w_down[i]), norm_g[i, 3])
        xs = xs + ms[5] * rmsnorm(swiglu(hs, ffn_w_gate[i], ffn_w_up[i], ffn_w_down[i]), norm_g[i, 3])
    return (xp, xs, jnp.stack(cmp_p), jnp.stack(cmp_s), jnp.stack(slc_p), jnp.stack(slc_s),
            jnp.stack(win_p), jnp.stack(win_s), jnp.stack(ssm_p), jnp.stack(ssm_s), jnp.stack(gv_s))
```

```python
import functools
import math

import jax
import jax.numpy as jnp
from jax import lax
from jax.experimental import pallas as pl
from jax.experimental.pallas import tpu as pltpu

F32 = jnp.float32
BF16 = jnp.bfloat16

RMS_EPS = 1.0e-6
LN_EPS = 1.0e-5

V7X_VMEM_BYTES = 64 * 1024 * 1024
VMEM_LIMIT_BYTES = V7X_VMEM_BYTES - 8 * 1024 * 1024

N_MIXERS = 3
GMLP_GROUPS = 8
GMLP_CHUNK = 128
NSA_HEAD_DIM = 64
NSA_KV_HEADS = 4
NSA_BLOCK = 64
NSA_TOPK = 16
NSA_WINDOW = 512
SEL_FORCE = 1.0e4
SEL_MASKED = -1.0
SSM_GROUP_WIDTH = 16
SSM_STATE = 64
NEG_BIG = -1.0e30


def _params(*sem):
    return pltpu.CompilerParams(dimension_semantics=sem, vmem_limit_bytes=VMEM_LIMIT_BYTES)


def _dot(a, b):
    return jnp.dot(a.astype(BF16), b.astype(BF16), preferred_element_type=F32)


def _dot_nt(a, b):
    return lax.dot_general(a.astype(BF16), b.astype(BF16), (((1,), (1,)), ((), ())),
                           preferred_element_type=F32)


def _rms(x, g):
    return x * lax.rsqrt(jnp.mean(x * x, axis=-1, keepdims=True) + RMS_EPS) * g


def _modulate(x, g, shift, scale):
    return _rms(x, g) * (1.0 + scale) + shift


def _const_spec(a, n_grid=1, single=False):
    nd = a.ndim
    idx = {1: lambda i: (0,) * nd, 2: lambda i, j: (0,) * nd, 3: lambda i, j, k: (0,) * nd}[n_grid]
    if single:
        return pl.BlockSpec(a.shape, idx, pipeline_mode=pl.Buffered(1))
    return pl.BlockSpec(a.shape, idx)


def _mod_spec(mod, n_tiles):
    _, n_seq, rows, d = mod.shape
    tiles_per_seq = n_tiles // n_seq
    return pl.BlockSpec((6, None, rows, d), lambda i: (0, i // tiles_per_seq, 0, 0))


def _adaln_body(c_ref, w_ref, b_ref, o_ref):
    c = c_ref[...]
    o_ref[...] = _dot(c * jax.nn.sigmoid(c), w_ref[...]) + b_ref[...]


def _adaln(c_all, w_mod, b_mod):
    depth, d, d6 = w_mod.shape
    m = c_all.shape[0]
    tn = 2048
    return pl.pallas_call(
        _adaln_body,
        grid=(depth, d6 // tn),
        in_specs=[pl.BlockSpec((m, d), lambda l, j: (0, 0)),
                  pl.BlockSpec((None, d, tn), lambda l, j: (l, 0, j)),
                  pl.BlockSpec((None, 1, tn), lambda l, j: (l, 0, j))],
        out_specs=pl.BlockSpec((None, m, tn), lambda l, j: (l, 0, j)),
        out_shape=jax.ShapeDtypeStruct((depth, m, d6), F32),
        compiler_params=_params("arbitrary", "arbitrary"),
        name="adaln",
    )(c_all, w_mod, b_mod.reshape(depth, 1, d6))


def _ffn_body(x_ref, m_ref, g_ref, wg_ref, wu_ref, wd_ref, o_ref, *, n_chunks):
    x = x_ref[...]
    h = _modulate(x, g_ref[2:3], m_ref[3], m_ref[4]).astype(BF16)
    fc = wg_ref.shape[1] // n_chunks
    acc = None
    for c in range(n_chunks):
        a = _dot(h, wg_ref[:, c * fc:(c + 1) * fc])
        b = _dot(h, wu_ref[:, c * fc:(c + 1) * fc])
        y = _dot(a * jax.nn.sigmoid(a) * b, wd_ref[c * fc:(c + 1) * fc, :])
        acc = y if acc is None else acc + y
    o_ref[...] = x + m_ref[5] * _rms(acc, g_ref[3:4])


def _ffn(x, mod, g, wg, wu, wd, tm):
    n, d = x.shape
    n_tiles = n // tm
    return pl.pallas_call(
        functools.partial(_ffn_body, n_chunks=2),
        grid=(n_tiles,),
        in_specs=[pl.BlockSpec((tm, d), lambda i: (i, 0)), _mod_spec(mod, n_tiles), _const_spec(g),
                  _const_spec(wg, single=True), _const_spec(wu, single=True), _const_spec(wd, single=True)],
        out_specs=pl.BlockSpec((tm, d), lambda i: (i, 0)),
        out_shape=jax.ShapeDtypeStruct((n, d), F32),
        compiler_params=_params("arbitrary"),
        name="ffn",
    )(x, mod, g, wg, wu, wd)


def _gmlp_front(x_ref, m_ref, g_ref, win_ref, bin_ref, lng_ref, lnb_ref):
    x = x_ref[...]
    h = _modulate(x, g_ref[0:1], m_ref[0], m_ref[1])
    z = jax.nn.gelu(_dot(h, win_ref[...]) + bin_ref[...])
    half = z.shape[1] // 2
    u, v = z[:, :half], z[:, half:]
    mu = jnp.mean(v, axis=-1, keepdims=True)
    var = jnp.mean(jnp.square(v - mu), axis=-1, keepdims=True)
    v = (v - mu) * lax.rsqrt(var + LN_EPS) * lng_ref[...] + lnb_ref[...]
    return x, u, v


def _gmlp_prompt_body(x_ref, m_ref, g_ref, win_ref, bin_ref, lng_ref, lnb_ref, ws_ref, bs_ref, wout_ref,
                      o_ref, um_ref):
    x, u, v = _gmlp_front(x_ref, m_ref, g_ref, win_ref, bin_ref, lng_ref, lnb_ref)
    vb = v.astype(BF16)
    n_groups, chunk, _ = ws_ref.shape
    gw = v.shape[1] // n_groups
    causal = (lax.broadcasted_iota(jnp.int32, (chunk, chunk), 0)
              >= lax.broadcasted_iota(jnp.int32, (chunk, chunk), 1))
    for grp in range(n_groups):
        w = jnp.where(causal, ws_ref[grp], 0.0).astype(BF16)
        cols = slice(grp * gw, (grp + 1) * gw)
        for k in range(x.shape[0] // chunk):
            rows = slice(k * chunk, (k + 1) * chunk)
            mixed = _dot(w, vb[rows, cols]) + bs_ref[:, grp:grp + 1]
            um_ref[rows, cols] = (u[rows, cols] * mixed).astype(BF16)
    y = _dot(um_ref[...], wout_ref[...])
    o_ref[...] = x + m_ref[2] * _rms(y, g_ref[1:2])


def _gmlp_sample_body(x_ref, m_ref, g_ref, win_ref, bin_ref, lng_ref, lnb_ref, ws_ref, bs_ref, wout_ref,
                      o_ref, v_ref):
    x, u, v = _gmlp_front(x_ref, m_ref, g_ref, win_ref, bin_ref, lng_ref, lnb_ref)
    v_ref[...] = v
    y = _dot(u * (ws_ref[...] * v + bs_ref[...]), wout_ref[...])
    o_ref[...] = x + m_ref[2] * _rms(y, g_ref[1:2])


def _gmlp_prompt(x, mod, g, w_in, b_in, ln_g, ln_b, w_s, b_s, w_out, tm):
    n, d = x.shape
    n_tiles = n // tm
    half = w_out.shape[0]
    args = (x, mod, g, w_in, b_in[None], ln_g[None], ln_b[None], w_s, b_s.T, w_out)
    return pl.pallas_call(
        _gmlp_prompt_body,
        grid=(n_tiles,),
        in_specs=[pl.BlockSpec((tm, d), lambda i: (i, 0)), _mod_spec(mod, n_tiles)]
        + [_const_spec(a) for a in args[2:]],
        out_specs=pl.BlockSpec((tm, d), lambda i: (i, 0)),
        out_shape=jax.ShapeDtypeStruct((n, d), F32),
        scratch_shapes=[pltpu.VMEM((tm, half), BF16)],
        compiler_params=_params("arbitrary"),
        name="gmlp_prompt",
    )(*args)


def _gmlp_sample(x, mod, g, w_in, b_in, ln_g, ln_b, w_s, b_s, w_out):
    n, d = x.shape
    half = w_out.shape[0]
    gw = half // w_s.shape[0]
    args = (x, mod, g, w_in, b_in[None], ln_g[None], ln_b[None],
            jnp.repeat(w_s[:, 0, 0], gw)[None], jnp.repeat(b_s[:, 0], gw)[None], w_out)
    return pl.pallas_call(
        _gmlp_sample_body,
        grid=(1,),
        in_specs=[pl.BlockSpec((n, d), lambda i: (0, 0)), _mod_spec(mod, 1)]
        + [_const_spec(a) for a in args[2:]],
        out_specs=[pl.BlockSpec((n, d), lambda i: (0, 0)), pl.BlockSpec((n, half), lambda i: (0, 0))],
        out_shape=[jax.ShapeDtypeStruct((n, d), F32), jax.ShapeDtypeStruct((n, half), F32)],
        compiler_params=_params("arbitrary"),
        name="gmlp_sample",
    )(*args)


def _cmul(ar, ai, br, bi):
    return ar * br - ai * bi, ar * bi + ai * br


def _ssm_prep_body(lr_ref, li_ref, ls_ref, br_ref, bi_ref, pwr_ref, pwi_ref, bbr_ref, bbi_ref):
    lr, li = lr_ref[...], li_ref[...]
    dt = jnp.exp(ls_ref[...])
    mag = jnp.exp(lr * dt)
    ab_re, ab_im = mag * jnp.cos(li * dt), mag * jnp.sin(li * dt)
    den = lr * lr + li * li
    f_re = ((ab_re - 1.0) * lr + ab_im * li) / den
    f_im = (ab_im * lr - (ab_re - 1.0) * li) / den
    bbr_ref[...] = f_re[:, None, :] * br_ref[...] - f_im[:, None, :] * bi_ref[...]
    bbi_ref[...] = f_re[:, None, :] * bi_ref[...] + f_im[:, None, :] * br_ref[...]
    p = [(ab_re, ab_im)]
    for n in range(2, 9):
        p.append(_cmul(*p[n // 2 - 1], *p[n - n // 2 - 1]))
    for n in range(8):
        pwr_ref[n] = p[n][0]
        pwi_ref[n] = p[n][1]


def _ssm_prep(lam_re, lam_im, log_step, b_re, b_im):
    g, p = lam_re.shape
    w = b_re.shape[2]
    args = (lam_re, lam_im, log_step[:, None], jnp.swapaxes(b_re, 1, 2), jnp.swapaxes(b_im, 1, 2))
    return pl.pallas_call(
        _ssm_prep_body,
        out_shape=[jax.ShapeDtypeStruct((8, g, p), F32), jax.ShapeDtypeStruct((8, g, p), F32),
                   jax.ShapeDtypeStruct((g, w, p), F32), jax.ShapeDtypeStruct((g, w, p), F32)],
        name="ssm_prep",
    )(*args)


def _ssm_input(x_ref, m_ref, g_ref, bbr_ref, bbi_ref, xr_ref, xi_ref):
    x = x_ref[...]
    u = _modulate(x, g_ref[0:1], m_ref[0], m_ref[1])
    ub = u.astype(BF16)
    n_kb, kin, kout = bbr_ref.shape
    for kb in range(n_kb):
        xr_ref[:, kb * kout:(kb + 1) * kout] = _dot(ub[:, kb * kin:(kb + 1) * kin], bbr_ref[kb])
        xi_ref[:, kb * kout:(kb + 1) * kout] = _dot(ub[:, kb * kin:(kb + 1) * kin], bbi_ref[kb])
    return x, u


def _ssm_output(x, u, m_ref, g_ref, xr_ref, xi_ref, cr_ref, ci_ref, d_ref, w1_ref, b1_ref, w2_ref, b2_ref,
                o_ref, y_ref):
    n_kb, kin, kout = cr_ref.shape
    for kb in range(n_kb):
        y_ref[:, kb * kout:(kb + 1) * kout] = (
            _dot(xr_ref[:, kb * kin:(kb + 1) * kin], cr_ref[kb])
            - _dot(xi_ref[:, kb * kin:(kb + 1) * kin], ci_ref[kb]))
    gl = jax.nn.gelu(y_ref[...] + d_ref[...] * u)
    out = (_dot(gl, w1_ref[...]) + b1_ref[...]) * jax.nn.sigmoid(_dot(gl, w2_ref[...]) + b2_ref[...])
    o_ref[...] = x + m_ref[2] * _rms(out, g_ref[1:2])


def _ssm_prompt_body(x_ref, m_ref, g_ref, bbr_ref, bbi_ref, pwr_ref, pwi_ref, cr_ref, ci_ref, d_ref,
                     w1_ref, b1_ref, w2_ref, b2_ref, o_ref, sr_ref, si_ref,
                     xr_ref, xi_ref, car_ref, cai_ref, y_ref, *, lane_block):
    @pl.when(pl.program_id(1) == 0)
    def _():
        car_ref[...] = jnp.zeros_like(car_ref)
        cai_ref[...] = jnp.zeros_like(cai_ref)

    x, u = _ssm_input(x_ref, m_ref, g_ref, bbr_ref, bbi_ref, xr_ref, xi_ref)
    rows, n_state = xr_ref.shape
    row = lax.broadcasted_iota(jnp.int32, (8, lane_block), 0)

    def group(j, carry):
        r0 = pl.multiple_of(j * 8, 8)
        for cb in range(n_state // lane_block):
            cols = slice(cb * lane_block, (cb + 1) * lane_block)
            vr, vi = xr_ref[pl.ds(r0, 8), cols], xi_ref[pl.ds(r0, 8), cols]
            for s in (1, 2, 4):
                keep = row >= s
                pr = jnp.where(keep, pltpu.roll(vr, s, axis=0), 0.0)
                pi = jnp.where(keep, pltpu.roll(vi, s, axis=0), 0.0)
                dr, di = _cmul(pwr_ref[s - 1:s, cols], pwi_ref[s - 1:s, cols], pr, pi)
                vr, vi = vr + dr, vi + di
            dr, di = _cmul(pwr_ref[:, cols], pwi_ref[:, cols], car_ref[:, cols], cai_ref[:, cols])
            vr, vi = vr + dr, vi + di
            xr_ref[pl.ds(r0, 8), cols] = vr
            xi_ref[pl.ds(r0, 8), cols] = vi
            car_ref[:, cols] = jnp.broadcast_to(vr[7:8], vr.shape)
            cai_ref[:, cols] = jnp.broadcast_to(vi[7:8], vi.shape)
        return carry

    lax.fori_loop(0, rows // 8, group, 0)
    sr_ref[...] = car_ref[0:1, :]
    si_ref[...] = cai_ref[0:1, :]
    _ssm_output(x, u, m_ref, g_ref, xr_ref, xi_ref, cr_ref, ci_ref, d_ref, w1_ref, b1_ref, w2_ref, b2_ref,
                o_ref, y_ref)


def _ssm_sample_body(x_ref, m_ref, g_ref, bbr_ref, bbi_ref, pwr_ref, pwi_ref, cr_ref, ci_ref, d_ref,
                     w1_ref, b1_ref, w2_ref, b2_ref, hr_ref, hi_ref, o_ref, sr_ref, si_ref,
                     xr_ref, xi_ref, y_ref):
    x, u = _ssm_input(x_ref, m_ref, g_ref, bbr_ref, bbi_ref, xr_ref, xi_ref)
    dr, di = _cmul(pwr_ref[0:1, :], pwi_ref[0:1, :], hr_ref[...], hi_ref[...])
    xr_ref[...] = xr_ref[...] + dr
    xi_ref[...] = xi_ref[...] + di
    sr_ref[...] = xr_ref[...]
    si_ref[...] = xi_ref[...]
    _ssm_output(x, u, m_ref, g_ref, xr_ref, xi_ref, cr_ref, ci_ref, d_ref, w1_ref, b1_ref, w2_ref, b2_ref,
                o_ref, y_ref)


def _ssm_tables(lam_re, lam_im, b_re, b_im, c_re, c_im, log_step):
    g, p = lam_re.shape
    w = b_re.shape[2]
    pwr, pwi, bbr, bbi = _ssm_prep(lam_re, lam_im, log_step, b_re, b_im)
    gb = 256 // w
    eye = jnp.eye(gb, dtype=F32)

    def bd_in(a):
        return jnp.einsum("kgip,gh->kgihp", a.reshape(g // gb, gb, w, p), eye).reshape(g // gb, gb * w, gb * p)

    def bd_out(a):
        return jnp.einsum("kgip,gh->kgphi", a.reshape(g // gb, gb, w, p), eye).reshape(g // gb, gb * p, gb * w)

    return (bd_in(bbr).astype(BF16), bd_in(bbi).astype(BF16), pwr.reshape(8, g * p), pwi.reshape(8, g * p),
            bd_out(c_re).astype(BF16), bd_out(c_im).astype(BF16))


def _ssm_prompt(x, mod, g, tables, d_skip, w1, b1, w2, b2, n_seq, tl):
    n, d = x.shape
    t = n // n_seq
    n_state = tables[2].shape[1]
    consts = tables + (d_skip[None], w1, b1[None], w2, b2[None])
    row_spec = pl.BlockSpec((tl, d), lambda b, c: (b * (t // tl) + c, 0))
    st_spec = pl.BlockSpec((None, 1, n_state), lambda b, c: (b, 0, 0))
    return pl.pallas_call(
        functools.partial(_ssm_prompt_body, lane_block=1024),
        grid=(n_seq, t // tl),
        in_specs=[row_spec, pl.BlockSpec((6, None, 1, d), lambda b, c: (0, b, 0, 0)), _const_spec(g, 2)]
        + [_const_spec(a, 2) for a in consts],
        out_specs=[row_spec, st_spec, st_spec],
        out_shape=[jax.ShapeDtypeStruct((n, d), F32), jax.ShapeDtypeStruct((n_seq, 1, n_state), F32),
                   jax.ShapeDtypeStruct((n_seq, 1, n_state), F32)],
        scratch_shapes=[pltpu.VMEM((tl, n_state), F32), pltpu.VMEM((tl, n_state), F32),
                        pltpu.VMEM((8, n_state), F32), pltpu.VMEM((8, n_state), F32), pltpu.VMEM((tl, d), F32)],
        compiler_params=_params("arbitrary", "arbitrary"),
        name="ssm_prompt",
    )(x, mod, g, *consts)


def _ssm_sample(x, mod, g, tables, d_skip, w1, b1, w2, b2, h_re, h_im):
    n, d = x.shape
    n_state = tables[2].shape[1]
    consts = tables + (d_skip[None], w1, b1[None], w2, b2[None], h_re, h_im)
    full = pl.BlockSpec((n, d), lambda i: (0, 0))
    st = pl.BlockSpec((n, n_state), lambda i: (0, 0))
    return pl.pallas_call(
        _ssm_sample_body,
        grid=(1,),
        in_specs=[full, _mod_spec(mod, 1), _const_spec(g)] + [_const_spec(a) for a in consts],
        out_specs=[full, st, st],
        out_shape=[jax.ShapeDtypeStruct((n, d), F32), jax.ShapeDtypeStruct((n, n_state), F32),
                   jax.ShapeDtypeStruct((n, n_state), F32)],
        scratch_shapes=[pltpu.VMEM((n, n_state), F32), pltpu.VMEM((n, n_state), F32), pltpu.VMEM((n, d), F32)],
        compiler_params=_params("arbitrary"),
        name="ssm_sample",
    )(x, mod, g, *consts)


def _nsa_proj_body(x_ref, m_ref, g_ref, w_ref, q_ref, kc_ref, ks_ref, kw_ref, gt_ref, ksb_ref, kwb_ref):
    h = _modulate(x_ref[...], g_ref[0:1], m_ref[0], m_ref[1]).astype(BF16)
    qc, kc = q_ref.shape[1], kc_ref.shape[1]
    q_ref[...] = (_dot(h, w_ref[:, :qc]) * (NSA_HEAD_DIM ** -0.5)).astype(BF16)
    kc_ref[...] = _dot(h, w_ref[:, qc:qc + kc])
    ks = _dot(h, w_ref[:, qc + kc:qc + 2 * kc])
    kw = _dot(h, w_ref[:, qc + 2 * kc:qc + 3 * kc])
    ks_ref[...] = ks
    kw_ref[...] = kw
    ksb_ref[...] = ks.astype(BF16)
    kwb_ref[...] = kw.astype(BF16)
    gt_ref[...] = jax.nn.sigmoid(_dot(h, w_ref[:, qc + 3 * kc:]))


def _nsa_proj(x, mod, g, w_in, tm):
    n, d = x.shape
    n_tiles = n // tm
    kc = 2 * NSA_KV_HEADS * NSA_HEAD_DIM
    ng = (w_in.shape[1] - d - 3 * kc)
    widths = [(d, BF16), (kc, F32), (kc, F32), (kc, F32), (ng, F32), (kc, BF16), (kc, BF16)]
    return pl.pallas_call(
        _nsa_proj_body,
        grid=(n_tiles,),
        in_specs=[pl.BlockSpec((tm, d), lambda i: (i, 0)), _mod_spec(mod, n_tiles), _const_spec(g),
                  _const_spec(w_in)],
        out_specs=[pl.BlockSpec((tm, w), lambda i: (i, 0)) for w, _ in widths],
        out_shape=[jax.ShapeDtypeStruct((n, w), dt) for w, dt in widths],
        compiler_params=_params("arbitrary"),
        name="nsa_proj",
    )(x, mod, g, w_in)


def _compress_step(x_of, ls, pe_ref, w1_ref, w2_ref, o_ref, acc_ref, n_l):
    dh = NSA_HEAD_DIM
    hid = w1_ref.shape[3]
    n_zg = 2 * NSA_KV_HEADS

    @pl.when(ls == 0)
    def _():
        acc_ref[...] = jnp.zeros_like(acc_ref)

    for ll in range(n_l):
        l = ls * n_l + ll
        xb = (x_of(ll) + pe_ref[pl.ds(l, 1), :]).astype(BF16)
        for zg in range(n_zg):
            acc_ref[:, zg * hid:(zg + 1) * hid] += _dot(xb[:, zg * dh:(zg + 1) * dh], w1_ref[zg // NSA_KV_HEADS, l])

    @pl.when(ls == pl.num_programs(1) - 1)
    def _():
        a = acc_ref[...]
        hidv = (a * jax.nn.sigmoid(a)).astype(BF16)
        for zg in range(n_zg):
            o_ref[:, zg * dh:(zg + 1) * dh] = _dot(hidv[:, zg * hid:(zg + 1) * hid], w2_ref[zg // NSA_KV_HEADS])


def _compress_prompt_body(x_ref, pe_ref, w1_ref, w2_ref, o_ref, acc_ref, *, n_l):
    _compress_step(lambda ll: x_ref[:, ll, :], pl.program_id(1), pe_ref, w1_ref, w2_ref, o_ref, acc_ref, n_l)


def _compress_sample_body(pt_ref, cache_ref, pe_ref, w1_ref, w2_ref, o_ref, buf_ref, sem, acc_ref, *,
                          n_l, pages_per_step, per_page):
    bg, ls = pl.program_id(0), pl.program_id(1)

    def page_copy(p):
        page = pt_ref[bg * pages_per_step + p]
        return pltpu.make_async_copy(cache_ref.at[page, :, pl.ds(ls * n_l, n_l), :],
                                     buf_ref.at[pl.ds(p * per_page, per_page)], sem)

    def start(p, c):
        page_copy(p).start()
        return c

    def wait(p, c):
        page_copy(p).wait()
        return c

    lax.fori_loop(0, pages_per_step, start, 0)
    lax.fori_loop(0, pages_per_step, wait, 0)
    _compress_step(lambda ll: buf_ref[:, ll, :], ls, pe_ref, w1_ref, w2_ref, o_ref, acc_ref, n_l)


def _pe_rows(pe):
    blk = pe.shape[1]
    return jnp.broadcast_to(pe.transpose(1, 0, 2)[:, :, None, :], (blk, 2, NSA_KV_HEADS, pe.shape[2])).reshape(blk, -1)


_COMPRESS_ROWS = 16


def _compress_prompt(kc, pe, w1, w2, nbt):
    n, c = kc.shape
    nblk = n // NSA_BLOCK
    n_l = _COMPRESS_ROWS
    x3 = kc.reshape(nblk, NSA_BLOCK, c)
    return pl.pallas_call(
        functools.partial(_compress_prompt_body, n_l=n_l),
        grid=(nblk // nbt, NSA_BLOCK // n_l),
        in_specs=[pl.BlockSpec((nbt, n_l, c), lambda i, l: (i, l, 0)), _const_spec(pe, 2), _const_spec(w1, 2),
                  _const_spec(w2, 2)],
        out_specs=pl.BlockSpec((nbt, c), lambda i, l: (i, 0)),
        out_shape=jax.ShapeDtypeStruct((nblk, c), F32),
        scratch_shapes=[pltpu.VMEM((nbt, 2 * NSA_KV_HEADS * w1.shape[3]), F32)],
        compiler_params=_params("arbitrary", "arbitrary"),
        name="compress_prompt",
    )(x3, pe, w1, w2)


def _compress_sample(page_table, cache, pe, w1, w2, pages_per_step):
    n_pool, per_page, blk, c = cache.shape
    n_l = _COMPRESS_ROWS
    n_pages_total = page_table.size
    nbt = pages_per_step * per_page
    grid_spec = pltpu.PrefetchScalarGridSpec(
        num_scalar_prefetch=1,
        grid=(n_pages_total // pages_per_step, blk // n_l),
        in_specs=[pl.BlockSpec(memory_space=pl.ANY),
                  pl.BlockSpec(pe.shape, lambda i, l, pt: (0, 0)),
                  pl.BlockSpec(w1.shape, lambda i, l, pt: (0, 0, 0, 0)),
                  pl.BlockSpec(w2.shape, lambda i, l, pt: (0, 0, 0))],
        out_specs=pl.BlockSpec((nbt, c), lambda i, l, pt: (i, 0)),
        scratch_shapes=[pltpu.VMEM((nbt, n_l, c), F32), pltpu.SemaphoreType.DMA(()),
                        pltpu.VMEM((nbt, 2 * NSA_KV_HEADS * w1.shape[3]), F32)],
    )
    return pl.pallas_call(
        functools.partial(_compress_sample_body, n_l=n_l, pages_per_step=pages_per_step, per_page=per_page),
        grid_spec=grid_spec,
        out_shape=jax.ShapeDtypeStruct((n_pages_total * per_page, c), F32),
        compiler_params=_params("arbitrary", "arbitrary"),
        name="compress_sample",
    )(page_table.reshape(-1), cache, pe, w1, w2)


def _stack_heads(q, grp):
    dh = NSA_HEAD_DIM
    rep = q.shape[1] // (NSA_KV_HEADS * dh)
    base = grp * rep * dh
    return jnp.concatenate([q[:, base + r * dh:base + (r + 1) * dh] for r in range(rep)], axis=0)


def _cmp_branch(qs, cmpv, grp, t_row, rep):
    dh = NSA_HEAD_DIM
    kv = NSA_KV_HEADS * dh
    kc = cmpv[:, grp * dh:(grp + 1) * dh]
    vc = cmpv[:, kv + grp * dh:kv + (grp + 1) * dh]
    s = _dot_nt(qs, kc)
    n = lax.broadcasted_iota(jnp.int32, s.shape, 1)
    mask = (n + 1) * NSA_BLOCK <= t_row + 1
    s = jnp.where(mask, s, NEG_BIG)
    e = jnp.where(mask, jnp.exp(s - jnp.max(s, axis=-1, keepdims=True)), 0.0)
    p = e / jnp.maximum(jnp.sum(e, axis=-1, keepdims=True), 1e-30)
    o = _dot(p, vc)
    t = p.shape[0] // rep
    imp = p[0:t]
    for r in range(1, rep):
        imp = imp + p[r * t:(r + 1) * t]
    return o, imp


def _topk_mask(score, axis):
    idx = lax.broadcasted_iota(jnp.int32, score.shape, axis).astype(F32)
    n = float(score.shape[axis])
    sel = jnp.zeros(score.shape, F32)
    x = score
    for _ in range(NSA_TOPK):
        m = jnp.max(x, axis=axis, keepdims=True)
        first = jnp.min(jnp.where(x == m, idx, n), axis=axis, keepdims=True)
        pick = idx == first
        sel = jnp.where(pick, 1.0, sel)
        x = jnp.where(pick, -jnp.inf, x)
    return sel


def _cmpattn_prompt_body(q_ref, cmp_ref, o_ref, sel_ref):
    tq = q_ref.shape[0]
    nb = cmp_ref.shape[0]
    dh = NSA_HEAD_DIM
    rep = q_ref.shape[1] // (NSA_KV_HEADS * dh)
    t0 = pl.program_id(1) * tq
    q = q_ref[...]
    cmpv = cmp_ref[...]
    t_row = t0 + lax.broadcasted_iota(jnp.int32, (rep * tq, 1), 0) % tq
    blk = lax.broadcasted_iota(jnp.int32, (nb, tq), 0)
    jt = (t0 + lax.broadcasted_iota(jnp.int32, (nb, tq), 1)) // NSA_BLOCK
    forced = (blk == 0) | (blk == jt) | (blk == jt - 1)
    for grp in range(NSA_KV_HEADS):
        o, imp = _cmp_branch(_stack_heads(q, grp), cmpv, grp, t_row, rep)
        for r in range(rep):
            h = grp * rep + r
            o_ref[:, h * dh:(h + 1) * dh] = o[r * tq:(r + 1) * tq]
        score = jnp.where(blk <= jt, jnp.where(forced, SEL_FORCE, imp.T), SEL_MASKED)
        sel = _topk_mask(score, 0) * (score > 0.5 * SEL_MASKED).astype(F32)
        sel_ref[:, grp * nb:(grp + 1) * nb] = sel.T.astype(BF16)


def _cmpattn_prompt(q, cmp, n_seq, tq):
    n, qc = q.shape
    t = n // n_seq
    nb = cmp.shape[0] // n_seq
    return pl.pallas_call(
        _cmpattn_prompt_body,
        grid=(n_seq, t // tq),
        in_specs=[pl.BlockSpec((tq, qc), lambda b, i: (b * (t // tq) + i, 0)),
                  pl.BlockSpec((nb, cmp.shape[1]), lambda b, i: (b, 0))],
        out_specs=[pl.BlockSpec((tq, qc), lambda b, i: (b * (t // tq) + i, 0)),
                   pl.BlockSpec((tq, NSA_KV_HEADS * nb), lambda b, i: (b * (t // tq) + i, 0))],
        out_shape=[jax.ShapeDtypeStruct((n, qc), F32), jax.ShapeDtypeStruct((n, NSA_KV_HEADS * nb), BF16)],
        compiler_params=_params("arbitrary", "arbitrary"),
        name="cmpattn_prompt",
    )(q, cmp)


def _cmpattn_sample_body(q_ref, cmp_ref, o_ref, idx_ref, *, t_pos, n_cand):
    nb = cmp_ref.shape[0]
    dh = NSA_HEAD_DIM
    rep = q_ref.shape[1] // (NSA_KV_HEADS * dh)
    q = q_ref[...]
    cmpv = cmp_ref[...]
    t_row = jnp.full((rep, 1), t_pos, jnp.int32)
    width = idx_ref.shape[1]
    lanes = ((n_cand + 127) // 128) * 128
    blk = lax.broadcasted_iota(jnp.int32, (1, lanes), 1)
    jt = t_pos // NSA_BLOCK
    forced = (blk == 0) | (blk == jt) | (blk == jt - 1)
    col = lax.broadcasted_iota(jnp.int32, (1, width), 1)
    blk_f = blk.astype(F32)
    for grp in range(NSA_KV_HEADS):
        o, imp = _cmp_branch(_stack_heads(q, grp), cmpv, grp, t_row, rep)
        for r in range(rep):
            h = grp * rep + r
            o_ref[:, h * dh:(h + 1) * dh] = o[r:r + 1]
        imp = jnp.concatenate([imp, jnp.zeros((1, lanes - nb), F32)], axis=1)
        score = jnp.where(blk <= jt, jnp.where(forced, SEL_FORCE, imp), SEL_MASKED)
        x = jnp.where(blk < n_cand, score, -jnp.inf)
        row = jnp.full((1, width), -1, jnp.int32)
        for k in range(NSA_TOPK):
            m = jnp.max(x, axis=1, keepdims=True)
            first = jnp.min(jnp.where(x == m, blk_f, float(lanes)), axis=1, keepdims=True)
            chosen = jnp.where(m > 0.5 * SEL_MASKED, first, -1.0).astype(jnp.int32)
            row = jnp.where(col == k, chosen, row)
            x = jnp.where(blk_f == first, -jnp.inf, x)
        idx_ref[grp:grp + 1, :] = row


def _cmpattn_sample(q, cmp, t_pos, n_cand):
    n_seq, qc = q.shape
    nb = cmp.shape[0] // n_seq
    return pl.pallas_call(
        functools.partial(_cmpattn_sample_body, t_pos=t_pos, n_cand=n_cand),
        grid=(n_seq,),
        in_specs=[pl.BlockSpec((None, 1, qc), lambda b: (b, 0, 0)), pl.BlockSpec((nb, cmp.shape[1]), lambda b: (b, 0))],
        out_specs=[pl.BlockSpec((None, 1, qc), lambda b: (b, 0, 0)),
                   pl.BlockSpec((None, NSA_KV_HEADS, 128), lambda b: (b, 0, 0))],
        out_shape=[jax.ShapeDtypeStruct((n_seq, 1, qc), F32), jax.ShapeDtypeStruct((n_seq, NSA_KV_HEADS, 128), jnp.int32)],
        compiler_params=_params("arbitrary"),
        name="cmpattn_sample",
    )(q[:, None, :], cmp)


def _attn_prompt_body(q_ref, sel_ref, ks_ref, kw_ref, oslc_ref, owin_ref, m_ref, l_ref, acc_ref, *, tk):
    tq = q_ref.shape[0]
    dh = NSA_HEAD_DIM
    kv = NSA_KV_HEADS * dh
    rep = q_ref.shape[1] // kv
    nb = sel_ref.shape[1] // NSA_KV_HEADS
    t0 = pl.program_id(1) * tq
    q = q_ref[...]
    t_col = t0 + lax.broadcasted_iota(jnp.int32, (tq, 1), 0)
    n_kv = (t0 + tq + tk - 1) // tk
    blk_row = lax.broadcasted_iota(jnp.int32, (nb, tk), 0)
    lane_k = lax.broadcasted_iota(jnp.int32, (nb, tk), 1)
    key_lane = lax.broadcasted_iota(jnp.int32, (1, tk), 1)
    win_len = NSA_WINDOW + tq
    w0 = pl.multiple_of(jnp.maximum(t0 - NSA_WINDOW, 0), tq)
    wpos = w0 + lax.broadcasted_iota(jnp.int32, (1, win_len), 1)
    win_ok = (wpos <= t_col) & (wpos >= t_col - NSA_WINDOW)
    win_bias = jnp.where(win_ok, 0.0, NEG_BIG)
    for grp in range(NSA_KV_HEADS):
        qs = _stack_heads(q, grp)
        sel = sel_ref[:, grp * nb:(grp + 1) * nb]
        m_ref[...] = jnp.full(m_ref.shape, 0.1 * NEG_BIG, F32)
        l_ref[...] = jnp.zeros_like(l_ref)
        acc_ref[...] = jnp.zeros_like(acc_ref)

        def kv_tile(j, carry):
            k0 = pl.multiple_of(j * tk, tk)
            k = ks_ref[pl.ds(k0, tk), grp * dh:(grp + 1) * dh]
            v = ks_ref[pl.ds(k0, tk), kv + grp * dh:kv + (grp + 1) * dh]
            s = _dot_nt(qs, k)
            expand = (blk_row == (k0 + lane_k) // NSA_BLOCK).astype(BF16)
            picked = _dot(sel, expand)
            ok = (picked > 0.5) & (k0 + key_lane <= t_col)
            s = s.reshape(rep, tq, tk) + jnp.where(ok, 0.0, NEG_BIG)[None]
            m_old = m_ref[...]
            m_new = jnp.maximum(m_old, jnp.max(s, axis=-1, keepdims=True))
            p = jnp.exp(s - m_new)
            alpha = jnp.exp(m_old - m_new)
            l_ref[...] = alpha * l_ref[...] + jnp.sum(p, axis=-1, keepdims=True)
            pv = _dot(p.reshape(rep * tq, tk), v).reshape(rep, tq, dh)
            acc_ref[...] = alpha * acc_ref[...] + pv
            m_ref[...] = m_new
            return carry

        lax.fori_loop(0, n_kv, kv_tile, 0)
        o = acc_ref[...] / jnp.maximum(l_ref[...], 1e-30)
        for r in range(rep):
            h = grp * rep + r
            oslc_ref[:, h * dh:(h + 1) * dh] = o[r]

        k = kw_ref[pl.ds(w0, win_len), grp * dh:(grp + 1) * dh]
        v = kw_ref[pl.ds(w0, win_len), kv + grp * dh:kv + (grp + 1) * dh]
        s = _dot_nt(qs, k).reshape(rep, tq, win_len) + win_bias[None]
        m = jnp.maximum(jnp.max(s, axis=-1, keepdims=True), 0.1 * NEG_BIG)
        e = jnp.exp(s - m)
        p = e / jnp.maximum(jnp.sum(e, axis=-1, keepdims=True), 1e-30)
        o = _dot(p.reshape(rep * tq, win_len), v)
        for r in range(rep):
            h = grp * rep + r
            owin_ref[:, h * dh:(h + 1) * dh] = o[r * tq:(r + 1) * tq]


def _attn_prompt(q, sel, ksb, kwb, n_seq, tq, tk):
    n, qc = q.shape
    t = n // n_seq
    dh = NSA_HEAD_DIM
    rep = qc // (NSA_KV_HEADS * dh)
    assert t % tk == 0 and tk % tq == 0 and t >= NSA_WINDOW + tq and NSA_WINDOW % tq == 0
    tile = pl.BlockSpec((tq, qc), lambda b, i: (b * (t // tq) + i, 0))
    seq = pl.BlockSpec((t, ksb.shape[1]), lambda b, i: (b, 0))
    return pl.pallas_call(
        functools.partial(_attn_prompt_body, tk=tk),
        grid=(n_seq, t // tq),
        in_specs=[tile, pl.BlockSpec((tq, sel.shape[1]), lambda b, i: (b * (t // tq) + i, 0)), seq, seq],
        out_specs=[tile, tile],
        out_shape=[jax.ShapeDtypeStruct((n, qc), F32), jax.ShapeDtypeStruct((n, qc), F32)],
        scratch_shapes=[pltpu.VMEM((rep, tq, 1), F32), pltpu.VMEM((rep, tq, 1), F32), pltpu.VMEM((rep, tq, dh), F32)],
        compiler_params=_params("arbitrary", "arbitrary"),
        name="attn_prompt",
    )(q, sel, ksb, kwb)


def _attn_sample_body(pt_ref, idx_ref, q_ref, ksn_ref, kwn_ref, win_ref, cache_ref, oslc_ref, owin_ref,
                      kbuf_ref, sem, wbuf_ref, *, t_pos, nb_past, per_page, n_pages):
    b = pl.program_id(0)
    dh = NSA_HEAD_DIM
    kv = NSA_KV_HEADS * dh
    rep = q_ref.shape[1] // kv
    n_sel = NSA_TOPK
    q = q_ref[...]
    ks_new = ksn_ref[...]

    def block_copy(grp, k, idx):
        page = pt_ref[b * n_pages + jnp.minimum(idx // per_page, n_pages - 1)]
        return pltpu.make_async_copy(cache_ref.at[page, idx % per_page], kbuf_ref.at[grp * n_sel + k], sem)

    for grp in range(NSA_KV_HEADS):
        for k in range(n_sel):
            idx = idx_ref[(b * NSA_KV_HEADS + grp) * 128 + k]

            @pl.when((idx >= 0) & (idx < nb_past))
            def _():
                block_copy(grp, k, idx).start()

            @pl.when(jnp.logical_not((idx >= 0) & (idx < nb_past)))
            def _():
                row = lax.broadcasted_iota(jnp.int32, (NSA_BLOCK, 2 * kv), 0)
                kbuf_ref[grp * n_sel + k] = jnp.where(row == 0, ks_new, 0.0)

    for grp in range(NSA_KV_HEADS):
        for k in range(n_sel):
            idx = idx_ref[(b * NSA_KV_HEADS + grp) * 128 + k]

            @pl.when((idx >= 0) & (idx < nb_past))
            def _():
                block_copy(grp, k, idx).wait()

    lane = lax.broadcasted_iota(jnp.int32, (1, n_sel * NSA_BLOCK), 1)
    wb = win_ref.shape[0]
    wbuf_ref[0:wb, :] = win_ref[...]
    pad_row = lax.broadcasted_iota(jnp.int32, (8, 2 * kv), 0)
    wbuf_ref[wb:wb + 8, :] = jnp.where(pad_row == 0, kwn_ref[...], 0.0)
    wpos = t_pos - wb + lax.broadcasted_iota(jnp.int32, (1, wb + 8), 1)
    win_ok = (wpos <= t_pos) & (wpos >= t_pos - NSA_WINDOW) & (wpos >= 0)
    for grp in range(NSA_KV_HEADS):
        qs = _stack_heads(q, grp)
        kvb = kbuf_ref[grp * n_sel:(grp + 1) * n_sel].reshape(n_sel * NSA_BLOCK, 2 * kv)
        ok = jnp.zeros((1, n_sel * NSA_BLOCK), jnp.bool_)
        for k in range(n_sel):
            idx = idx_ref[(b * NSA_KV_HEADS + grp) * 128 + k]
            kpos = idx * NSA_BLOCK + (lane - k * NSA_BLOCK)
            ok = ok | ((lane // NSA_BLOCK == k) & (idx >= 0) & (kpos <= t_pos))
        s = jnp.where(ok, _dot_nt(qs, kvb[:, grp * dh:(grp + 1) * dh]), NEG_BIG)
        e = jnp.where(ok, jnp.exp(s - jnp.max(s, axis=-1, keepdims=True)), 0.0)
        p = e / jnp.maximum(jnp.sum(e, axis=-1, keepdims=True), 1e-30)
        o = _dot(p, kvb[:, kv + grp * dh:kv + (grp + 1) * dh])
        for r in range(rep):
            h = grp * rep + r
            oslc_ref[:, h * dh:(h + 1) * dh] = o[r:r + 1]

        s = jnp.where(win_ok, _dot_nt(qs, wbuf_ref[:, grp * dh:(grp + 1) * dh]), NEG_BIG)
        e = jnp.where(win_ok, jnp.exp(s - jnp.max(s, axis=-1, keepdims=True)), 0.0)
        p = e / jnp.maximum(jnp.sum(e, axis=-1, keepdims=True), 1e-30)
        o = _dot(p, wbuf_ref[:, kv + grp * dh:kv + (grp + 1) * dh])
        for r in range(rep):
            h = grp * rep + r
            owin_ref[:, h * dh:(h + 1) * dh] = o[r:r + 1]


def _attn_sample(page_table, sel_idx, q, ks_new, kw_new, win, cache, t_pos):
    n_seq, qc = q.shape
    n_pool, per_page, blk, c = cache.shape
    n_pages = page_table.shape[1]
    wb = win.shape[1]
    row3 = lambda w: pl.BlockSpec((None, 1, w), lambda b, pt, ix: (b, 0, 0))
    grid_spec = pltpu.PrefetchScalarGridSpec(
        num_scalar_prefetch=2,
        grid=(n_seq,),
        in_specs=[row3(qc), row3(c), row3(c), pl.BlockSpec((None, wb, c), lambda b, pt, ix: (b, 0, 0)),
                  pl.BlockSpec(memory_space=pl.ANY)],
        out_specs=[row3(qc), row3(qc)],
        scratch_shapes=[pltpu.VMEM((NSA_KV_HEADS * NSA_TOPK, blk, c), F32), pltpu.SemaphoreType.DMA(()),
                        pltpu.VMEM((wb + 8, c), F32)],
    )
    return pl.pallas_call(
        functools.partial(_attn_sample_body, t_pos=t_pos, nb_past=t_pos // NSA_BLOCK, per_page=per_page,
                          n_pages=n_pages),
        grid_spec=grid_spec,
        out_shape=[jax.ShapeDtypeStruct((n_seq, 1, qc), F32), jax.ShapeDtypeStruct((n_seq, 1, qc), F32)],
        compiler_params=_params("arbitrary"),
        name="attn_sample",
    )(page_table.reshape(-1), sel_idx.reshape(-1), q[:, None, :], ks_new[:, None, :], kw_new[:, None, :], win, cache)


def _nsa_merge_body(x_ref, m_ref, g_ref, oc_ref, os_ref, ow_ref, gt_ref, wout_ref, o_ref, om_ref):
    dh = NSA_HEAD_DIM
    n_heads = oc_ref.shape[1] // dh
    gt = gt_ref[...]
    for h in range(n_heads):
        c = slice(h * dh, (h + 1) * dh)
        o = (gt[:, h:h + 1] * oc_ref[:, c] + gt[:, n_heads + h:n_heads + h + 1] * os_ref[:, c]
             + gt[:, 2 * n_heads + h:2 * n_heads + h + 1] * ow_ref[:, c])
        om_ref[:, c] = o.astype(BF16)
    x = x_ref[...]
    o_ref[...] = x + m_ref[2] * _rms(_dot(om_ref[...], wout_ref[...]), g_ref[1:2])


def _nsa_merge(x, mod, g, o_cmp, o_slc, o_win, gates, w_out, tm):
    n, d = x.shape
    n_tiles = n // tm
    qc = o_cmp.shape[1]
    tile = lambda w: pl.BlockSpec((tm, w), lambda i: (i, 0))
    return pl.pallas_call(
        _nsa_merge_body,
        grid=(n_tiles,),
        in_specs=[tile(d), _mod_spec(mod, n_tiles), _const_spec(g), tile(qc), tile(qc), tile(qc),
                  tile(gates.shape[1]), _const_spec(w_out)],
        out_specs=tile(d),
        out_shape=jax.ShapeDtypeStruct((n, d), F32),
        scratch_shapes=[pltpu.VMEM((tm, qc), BF16)],
        compiler_params=_params("arbitrary"),
        name="nsa_merge",
    )(x, mod, g, o_cmp, o_slc, o_win, gates, w_out)


def _nsa_layer(xp, xs, mod_p, mod_s, g, n_seq, cache_cmp, cache_slc, cache_win, page_table,
               w_in, w1, w2, pe, w_out, tm):
    n_s = xs.shape[0]
    t = xp.shape[0] // n_seq
    n_pool, page_size = cache_cmp.shape[0], cache_cmp.shape[1]
    per_page = page_size // NSA_BLOCK
    past = page_table.shape[1] * page_size
    c = 2 * NSA_KV_HEADS * NSA_HEAD_DIM
    assert t % NSA_BLOCK == 0 and past % NSA_BLOCK == 0 and page_size % NSA_BLOCK == 0
    w_in_b, w_out_b = w_in.astype(BF16), w_out.astype(BF16)
    w1_b, w2_b = w1.astype(BF16), w2.astype(BF16)
    pe_rows = _pe_rows(pe)

    q, kc, ks, kw, gates, ksb, kwb = _nsa_proj(xp, mod_p, g, w_in_b, tm)
    cmp_p = _compress_prompt(kc, pe_rows, w1_b, w2_b, min(256, kc.shape[0] // NSA_BLOCK))
    o_cmp, sel = _cmpattn_prompt(q, cmp_p, n_seq, 128)
    o_slc, o_win = _attn_prompt(q, sel, ksb, kwb, n_seq, 128, 256)
    xp = _nsa_merge(xp, mod_p, g, o_cmp, o_slc, o_win, gates, w_out_b, tm)

    q_s, kc_s, ks_s, kw_s, gates_s, _, _ = _nsa_proj(xs, mod_s, g, w_in_b, n_s)
    cmp_s = _compress_sample(page_table, cache_cmp.reshape(n_pool, per_page, NSA_BLOCK, c), pe_rows, w1_b, w2_b,
                             min(128, page_table.size))
    n_cand = -(-(past + 1) // NSA_BLOCK)
    o_cmp_s, sel_idx = _cmpattn_sample(q_s, cmp_s, past, n_cand)
    wb = cache_win.shape[1]
    o_slc_s, o_win_s = _attn_sample(page_table, sel_idx, q_s, ks_s, kw_s, cache_win.reshape(n_s, wb, c),
                                    cache_slc.reshape(n_pool, per_page, NSA_BLOCK, c), past)
    xs = _nsa_merge(xs, mod_s, g, o_cmp_s.reshape(n_s, -1), o_slc_s.reshape(n_s, -1), o_win_s.reshape(n_s, -1),
                    gates_s, w_out_b, n_s)
    return xp, xs, (kc, ks, kw), (kc_s, ks_s, kw_s)


def kernel(x_prompt, x_sample, cache_nsa_cmp, cache_nsa_slc, cache_nsa_win, state_ssm, page_table, c_prompt, c_sample, w_mod, b_mod, norm_g, ffn_w_gate, ffn_w_up, ffn_w_down, gmlp_w_in, gmlp_b_in, gmlp_ln_g, gmlp_ln_b, gmlp_w_s, gmlp_b_s, gmlp_w_out, nsa_w_in, nsa_w_cmp1, nsa_w_cmp2, nsa_pe_cmp, nsa_w_out, ssm_lambda_re, ssm_lambda_im, ssm_b_re, ssm_b_im, ssm_c_re, ssm_c_im, ssm_d, ssm_log_step, ssm_w_glu1, ssm_b_glu1, ssm_w_glu2, ssm_b_glu2):
    n_seq, t, d = x_prompt.shape
    n_s, t_s, _ = x_sample.shape
    assert t_s == 1
    depth = w_mod.shape[0]
    tm = 512 if t % 512 == 0 else 256
    kv_shape = (2, NSA_KV_HEADS, NSA_HEAD_DIM)

    xp = x_prompt.reshape(n_seq * t, d)
    xs = x_sample.reshape(n_s, d)
    m_all = _adaln(jnp.concatenate([c_prompt, c_sample], axis=0), w_mod, b_mod)
    mods_p = m_all[:, :n_seq].reshape(depth, n_seq, 6, 1, d).transpose(0, 2, 1, 3, 4)
    mods_s = m_all[:, n_seq:].reshape(depth, n_s, 6, d).transpose(0, 2, 1, 3)[:, :, None]

    cmp_p, cmp_s, slc_p, slc_s, win_p, win_s, ssm_p, ssm_s, gv_s = [], [], [], [], [], [], [], [], []
    for i in range(depth):
        j = i // N_MIXERS
        mp, ms, g = mods_p[i], mods_s[i], norm_g[i]
        if i % N_MIXERS == 0:
            gw = (gmlp_w_in[j].astype(BF16), gmlp_b_in[j], gmlp_ln_g[j], gmlp_ln_b[j], gmlp_w_s[j], gmlp_b_s[j],
                  gmlp_w_out[j].astype(BF16))
            assert t % GMLP_CHUNK == 0 and gmlp_w_s.shape[2] == GMLP_CHUNK
            xp = _gmlp_prompt(xp, mp, g, *gw, tm)
            xs, v_new = _gmlp_sample(xs, ms, g, *gw)
            gv_s.append(v_new.reshape(n_s, 1, -1))
        elif i % N_MIXERS == 1:
            xp, xs, kv_p, kv_s = _nsa_layer(xp, xs, mp, ms, g, n_seq, cache_nsa_cmp[j], cache_nsa_slc[j],
                                            cache_nsa_win[j], page_table, nsa_w_in[j], nsa_w_cmp1[j],
                                            nsa_w_cmp2[j], nsa_pe_cmp[j], nsa_w_out[j], tm)
            cmp_p.append(kv_p[0].reshape((n_seq, t) + kv_shape))
            slc_p.append(kv_p[1].reshape((n_seq, t) + kv_shape))
            keep = min(NSA_WINDOW, t)
            win_p.append(kv_p[2].reshape((n_seq, t) + kv_shape)[:, t - keep:])
            cmp_s.append(kv_s[0].reshape((n_s, 1) + kv_shape))
            slc_s.append(kv_s[1].reshape((n_s, 1) + kv_shape))
            past = page_table.shape[1] * cache_nsa_cmp.shape[2]
            win = jnp.concatenate([cache_nsa_win[j], kv_s[2].reshape((n_s, 1) + kv_shape)], axis=1)
            win_s.append(win[:, win.shape[1] - min(NSA_WINDOW, past + 1):])
        else:
            tables = _ssm_tables(ssm_lambda_re[j], ssm_lambda_im[j], ssm_b_re[j], ssm_b_im[j], ssm_c_re[j],
                                 ssm_c_im[j], ssm_log_step[j])
            glu = (ssm_d[j], ssm_w_glu1[j].astype(BF16), ssm_b_glu1[j], ssm_w_glu2[j].astype(BF16), ssm_b_glu2[j])
            n_grp, n_st = ssm_lambda_re.shape[1:]
            xp, sr, si = _ssm_prompt(xp, mp, g, tables, *glu, n_seq, 256)
            ssm_p.append(jnp.stack([sr.reshape(n_seq, n_grp, n_st), si.reshape(n_seq, n_grp, n_st)], axis=-1))
            h0 = state_ssm[j].reshape(n_s, n_grp * n_st, 2)
            xs, sr, si = _ssm_sample(xs, ms, g, tables, *glu, h0[..., 0], h0[..., 1])
            ssm_s.append(jnp.stack([sr.reshape(n_s, n_grp, n_st), si.reshape(n_s, n_grp, n_st)], axis=-1))
        ffn_w = (ffn_w_gate[i].astype(BF16), ffn_w_up[i].astype(BF16), ffn_w_down[i].astype(BF16))
        xp = _ffn(xp, mp, g, *ffn_w, tm)
        xs = _ffn(xs, ms, g, *ffn_w, n_s)
    return (xp.reshape(n_seq, t, d), xs.reshape(n_s, 1, d), jnp.stack(cmp_p), jnp.stack(cmp_s), jnp.stack(slc_p),
            jnp.stack(slc_s), jnp.stack(win_p), jnp.stack(win_s), jnp.stack(ssm_p), jnp.stack(ssm_s), jnp.stack(gv_s))
```

```python
import functools
import math

import jax
import jax.numpy as jnp
from jax import lax
from jax.experimental import pallas as pl
from jax.experimental.pallas import tpu as pltpu

F32 = jnp.float32
BF16 = jnp.bfloat16

RMS_EPS = 1.0e-6
LN_EPS = 1.0e-5

V7X_VMEM_BYTES = 64 * 1024 * 1024
VMEM_LIMIT_BYTES = V7X_VMEM_BYTES - 8 * 1024 * 1024

N_MIXERS = 3
GMLP_GROUPS = 8
GMLP_CHUNK = 128
NSA_HEAD_DIM = 64
NSA_KV_HEADS = 4
NSA_BLOCK = 64
NSA_TOPK = 16
NSA_WINDOW = 512
SEL_FORCE = 1.0e4
SEL_MASKED = -1.0
SSM_GROUP_WIDTH = 16
SSM_STATE = 64
NEG_BIG = -1.0e30


def _params(*sem):
    return pltpu.CompilerParams(dimension_semantics=sem, vmem_limit_bytes=VMEM_LIMIT_BYTES)


def _dot(a, b):
    return jnp.dot(a.astype(BF16), b.astype(BF16), preferred_element_type=F32)


def _dot_nt(a, b):
    return lax.dot_general(a.astype(BF16), b.astype(BF16), (((1,), (1,)), ((), ())),
                           preferred_element_type=F32)


def _rms(x, g):
    return x * lax.rsqrt(jnp.mean(x * x, axis=-1, keepdims=True) + RMS_EPS) * g


def _modulate(x, g, shift, scale):
    return _rms(x, g) * (1.0 + scale) + shift


def _const_spec(a, n_grid=1, single=False):
    nd = a.ndim
    idx = {1: lambda i: (0,) * nd, 2: lambda i, j: (0,) * nd, 3: lambda i, j, k: (0,) * nd}[n_grid]
    if single:
        return pl.BlockSpec(a.shape, idx, pipeline_mode=pl.Buffered(1))
    return pl.BlockSpec(a.shape, idx)


def _mod_spec(mod, n_tiles):
    _, n_seq, rows, d = mod.shape
    tiles_per_seq = n_tiles // n_seq
    return pl.BlockSpec((6, None, rows, d), lambda i: (0, i // tiles_per_seq, 0, 0))


def _adaln_body(c_ref, w_ref, b_ref, o_ref):
    c = c_ref[...]
    o_ref[...] = _dot(c * jax.nn.sigmoid(c), w_ref[...]) + b_ref[...]


def _adaln(c_all, w_mod, b_mod):
    depth, d, d6 = w_mod.shape
    m = c_all.shape[0]
    tn = 2048
    return pl.pallas_call(
        _adaln_body,
        grid=(depth, d6 // tn),
        in_specs=[pl.BlockSpec((m, d), lambda l, j: (0, 0)),
                  pl.BlockSpec((None, d, tn), lambda l, j: (l, 0, j)),
                  pl.BlockSpec((None, 1, tn), lambda l, j: (l, 0, j))],
        out_specs=pl.BlockSpec((None, m, tn), lambda l, j: (l, 0, j)),
        out_shape=jax.ShapeDtypeStruct((depth, m, d6), F32),
        compiler_params=_params("arbitrary", "arbitrary"),
        name="adaln",
    )(c_all, w_mod, b_mod.reshape(depth, 1, d6))


def _ffn_body(x_ref, m_ref, g_ref, wg_ref, wu_ref, wd_ref, o_ref, *, n_chunks):
    x = x_ref[...]
    h = _modulate(x, g_ref[2:3], m_ref[3], m_ref[4]).astype(BF16)
    fc = wg_ref.shape[1] // n_chunks
    acc = None
    for c in range(n_chunks):
        a = _dot(h, wg_ref[:, c * fc:(c + 1) * fc])
        b = _dot(h, wu_ref[:, c * fc:(c + 1) * fc])
        y = _dot(a * jax.nn.sigmoid(a) * b, wd_ref[c * fc:(c + 1) * fc, :])
        acc = y if acc is None else acc + y
    o_ref[...] = x + m_ref[5] * _rms(acc, g_ref[3:4])


def _ffn(x, mod, g, wg, wu, wd, tm):
    n, d = x.shape
    n_tiles = n // tm
    return pl.pallas_call(
        functools.partial(_ffn_body, n_chunks=2),
        grid=(n_tiles,),
        in_specs=[pl.BlockSpec((tm, d), lambda i: (i, 0)), _mod_spec(mod, n_tiles), _const_spec(g),
                  _const_spec(wg, single=True), _const_spec(wu, single=True), _const_spec(wd, single=True)],
        out_specs=pl.BlockSpec((tm, d), lambda i: (i, 0)),
        out_shape=jax.ShapeDtypeStruct((n, d), F32),
        compiler_params=_params("arbitrary"),
        name="ffn",
    )(x, mod, g, wg, wu, wd)


def _gmlp_front(x_ref, m_ref, g_ref, win_ref, bin_ref, lng_ref, lnb_ref):
    x = x_ref[...]
    h = _modulate(x, g_ref[0:1], m_ref[0], m_ref[1])
    z = jax.nn.gelu(_dot(h, win_ref[...]) + bin_ref[...])
    half = z.shape[1] // 2
    u, v = z[:, :half], z[:, half:]
    mu = jnp.mean(v, axis=-1, keepdims=True)
    var = jnp.mean(jnp.square(v - mu), axis=-1, keepdims=True)
    v = (v - mu) * lax.rsqrt(var + LN_EPS) * lng_ref[...] + lnb_ref[...]
    return x, u, v


def _gmlp_prompt_body(x_ref, m_ref, g_ref, win_ref, bin_ref, lng_ref, lnb_ref, ws_ref, bs_ref, wout_ref,
                      o_ref, um_ref):
    x, u, v = _gmlp_front(x_ref, m_ref, g_ref, win_ref, bin_ref, lng_ref, lnb_ref)
    vb = v.astype(BF16)
    n_groups, chunk, _ = ws_ref.shape
    gw = v.shape[1] // n_groups
    causal = (lax.broadcasted_iota(jnp.int32, (chunk, chunk), 0)
              >= lax.broadcasted_iota(jnp.int32, (chunk, chunk), 1))
    for grp in range(n_groups):
        w = jnp.where(causal, ws_ref[grp], 0.0).astype(BF16)
        cols = slice(grp * gw, (grp + 1) * gw)
        for k in range(x.shape[0] // chunk):
            rows = slice(k * chunk, (k + 1) * chunk)
            mixed = _dot(w, vb[rows, cols]) + bs_ref[:, grp:grp + 1]
            um_ref[rows, cols] = (u[rows, cols] * mixed).astype(BF16)
    y = _dot(um_ref[...], wout_ref[...])
    o_ref[...] = x + m_ref[2] * _rms(y, g_ref[1:2])


def _gmlp_sample_body(x_ref, m_ref, g_ref, win_ref, bin_ref, lng_ref, lnb_ref, ws_ref, bs_ref, wout_ref,
                      o_ref, v_ref):
    x, u, v = _gmlp_front(x_ref, m_ref, g_ref, win_ref, bin_ref, lng_ref, lnb_ref)
    v_ref[...] = v
    y = _dot(u * (ws_ref[...] * v + bs_ref[...]), wout_ref[...])
    o_ref[...] = x + m_ref[2] * _rms(y, g_ref[1:2])


def _gmlp_prompt(x, mod, g, w_in, b_in, ln_g, ln_b, w_s, b_s, w_out, tm):
    n, d = x.shape
    n_tiles = n // tm
    half = w_out.shape[0]
    args = (x, mod, g, w_in, b_in[None], ln_g[None], ln_b[None], w_s, b_s.T, w_out)
    return pl.pallas_call(
        _gmlp_prompt_body,
        grid=(n_tiles,),
        in_specs=[pl.BlockSpec((tm, d), lambda i: (i, 0)), _mod_spec(mod, n_tiles)]
        + [_const_spec(a) for a in args[2:]],
        out_specs=pl.BlockSpec((tm, d), lambda i: (i, 0)),
        out_shape=jax.ShapeDtypeStruct((n, d), F32),
        scratch_shapes=[pltpu.VMEM((tm, half), BF16)],
        compiler_params=_params("arbitrary"),
        name="gmlp_prompt",
    )(*args)


def _gmlp_sample(x, mod, g, w_in, b_in, ln_g, ln_b, w_s, b_s, w_out):
    n, d = x.shape
    half = w_out.shape[0]
    gw = half // w_s.shape[0]
    args = (x, mod, g, w_in, b_in[None], ln_g[None], ln_b[None],
            jnp.repeat(w_s[:, 0, 0], gw)[None], jnp.repeat(b_s[:, 0], gw)[None], w_out)
    return pl.pallas_call(
        _gmlp_sample_body,
        grid=(1,),
        in_specs=[pl.BlockSpec((n, d), lambda i: (0, 0)), _mod_spec(mod, 1)]
        + [_const_spec(a) for a in args[2:]],
        out_specs=[pl.BlockSpec((n, d), lambda i: (0, 0)), pl.BlockSpec((n, half), lambda i: (0, 0))],
        out_shape=[jax.ShapeDtypeStruct((n, d), F32), jax.ShapeDtypeStruct((n, half), F32)],
        compiler_params=_params("arbitrary"),
        name="gmlp_sample",
    )(*args)


def _cmul(ar, ai, br, bi):
    return ar * br - ai * bi, ar * bi + ai * br


def _ssm_prep_body(lr_ref, li_ref, ls_ref, br_ref, bi_ref, pwr_ref, pwi_ref, bbr_ref, bbi_ref):
    lr, li = lr_ref[...], li_ref[...]
    dt = jnp.exp(ls_ref[...])
    mag = jnp.exp(lr * dt)
    ab_re, ab_im = mag * jnp.cos(li * dt), mag * jnp.sin(li * dt)
    den = lr * lr + li * li
    f_re = ((ab_re - 1.0) * lr + ab_im * li) / den
    f_im = (ab_im * lr - (ab_re - 1.0) * li) / den
    bbr_ref[...] = f_re[:, None, :] * br_ref[...] - f_im[:, None, :] * bi_ref[...]
    bbi_ref[...] = f_re[:, None, :] * bi_ref[...] + f_im[:, None, :] * br_ref[...]
    p = [(ab_re, ab_im)]
    for n in range(2, 9):
        p.append(_cmul(*p[n // 2 - 1], *p[n - n // 2 - 1]))
    for n in range(8):
        pwr_ref[n] = p[n][0]
        pwi_ref[n] = p[n][1]


def _ssm_prep(lam_re, lam_im, log_step, b_re, b_im):
    g, p = lam_re.shape
    w = b_re.shape[2]
    args = (lam_re, lam_im, log_step[:, None], jnp.swapaxes(b_re, 1, 2), jnp.swapaxes(b_im, 1, 2))
    return pl.pallas_call(
        _ssm_prep_body,
        out_shape=[jax.ShapeDtypeStruct((8, g, p), F32), jax.ShapeDtypeStruct((8, g, p), F32),
                   jax.ShapeDtypeStruct((g, w, p), F32), jax.ShapeDtypeStruct((g, w, p), F32)],
        name="ssm_prep",
    )(*args)


def _ssm_input(x_ref, m_ref, g_ref, bbr_ref, bbi_ref, xr_ref, xi_ref):
    x = x_ref[...]
    u = _modulate(x, g_ref[0:1], m_ref[0], m_ref[1])
    ub = u.astype(BF16)
    n_kb, kin, kout = bbr_ref.shape
    for kb in range(n_kb):
        xr_ref[:, kb * kout:(kb + 1) * kout] = _dot(ub[:, kb * kin:(kb + 1) * kin], bbr_ref[kb])
        xi_ref[:, kb * kout:(kb + 1) * kout] = _dot(ub[:, kb * kin:(kb + 1) * kin], bbi_ref[kb])
    return x, u


def _ssm_output(x, u, m_ref, g_ref, xr_ref, xi_ref, cr_ref, ci_ref, d_ref, w1_ref, b1_ref, w2_ref, b2_ref,
                o_ref, y_ref):
    n_kb, kin, kout = cr_ref.shape
    for kb in range(n_kb):
        y_ref[:, kb * kout:(kb + 1) * kout] = (
            _dot(xr_ref[:, kb * kin:(kb + 1) * kin], cr_ref[kb])
            - _dot(xi_ref[:, kb * kin:(kb + 1) * kin], ci_ref[kb]))
    gl = jax.nn.gelu(y_ref[...] + d_ref[...] * u)
    out = (_dot(gl, w1_ref[...]) + b1_ref[...]) * jax.nn.sigmoid(_dot(gl, w2_ref[...]) + b2_ref[...])
    o_ref[...] = x + m_ref[2] * _rms(out, g_ref[1:2])


def _ssm_prompt_body(x_ref, m_ref, g_ref, bbr_ref, bbi_ref, pwr_ref, pwi_ref, cr_ref, ci_ref, d_ref,
                     w1_ref, b1_ref, w2_ref, b2_ref, o_ref, sr_ref, si_ref,
                     xr_ref, xi_ref, car_ref, cai_ref, y_ref, *, lane_block):
    @pl.when(pl.program_id(1) == 0)
    def _():
        car_ref[...] = jnp.zeros_like(car_ref)
        cai_ref[...] = jnp.zeros_like(cai_ref)

    x, u = _ssm_input(x_ref, m_ref, g_ref, bbr_ref, bbi_ref, xr_ref, xi_ref)
    rows, n_state = xr_ref.shape
    row = lax.broadcasted_iota(jnp.int32, (8, lane_block), 0)

    def group(j, carry):
        r0 = pl.multiple_of(j * 8, 8)
        for cb in range(n_state // lane_block):
            cols = slice(cb * lane_block, (cb + 1) * lane_block)
            vr, vi = xr_ref[pl.ds(r0, 8), cols], xi_ref[pl.ds(r0, 8), cols]
            for s in (1, 2, 4):
                keep = row >= s
                pr = jnp.where(keep, pltpu.roll(vr, s, axis=0), 0.0)
                pi = jnp.where(keep, pltpu.roll(vi, s, axis=0), 0.0)
                dr, di = _cmul(pwr_ref[s - 1:s, cols], pwi_ref[s - 1:s, cols], pr, pi)
                vr, vi = vr + dr, vi + di
            dr, di = _cmul(pwr_ref[:, cols], pwi_ref[:, cols], car_ref[:, cols], cai_ref[:, cols])
            vr, vi = vr + dr, vi + di
            xr_ref[pl.ds(r0, 8), cols] = vr
            xi_ref[pl.ds(r0, 8), cols] = vi
            car_ref[:, cols] = jnp.broadcast_to(vr[7:8], vr.shape)
            cai_ref[:, cols] = jnp.broadcast_to(vi[7:8], vi.shape)
        return carry

    lax.fori_loop(0, rows // 8, group, 0)
    sr_ref[...] = car_ref[0:1, :]
    si_ref[...] = cai_ref[0:1, :]
    _ssm_output(x, u, m_ref, g_ref, xr_ref, xi_ref, cr_ref, ci_ref, d_ref, w1_ref, b1_ref, w2_ref, b2_ref,
                o_ref, y_ref)


def _ssm_sample_body(x_ref, m_ref, g_ref, bbr_ref, bbi_ref, pwr_ref, pwi_ref, cr_ref, ci_ref, d_ref,
                     w1_ref, b1_ref, w2_ref, b2_ref, hr_ref, hi_ref, o_ref, sr_ref, si_ref,
                     xr_ref, xi_ref, y_ref):
    x, u = _ssm_input(x_ref, m_ref, g_ref, bbr_ref, bbi_ref, xr_ref, xi_ref)
    dr, di = _cmul(pwr_ref[0:1, :], pwi_ref[0:1, :], hr_ref[...], hi_ref[...])
    xr_ref[...] = xr_ref[...] + dr
    xi_ref[...] = xi_ref[...] + di
    sr_ref[...] = xr_ref[...]
    si_ref[...] = xi_ref[...]
    _ssm_output(x, u, m_ref, g_ref, xr_ref, xi_ref, cr_ref, ci_ref, d_ref, w1_ref, b1_ref, w2_ref, b2_ref,
                o_ref, y_ref)


def _ssm_tables(lam_re, lam_im, b_re, b_im, c_re, c_im, log_step):
    g, p = lam_re.shape
    w = b_re.shape[2]
    pwr, pwi, bbr, bbi = _ssm_prep(lam_re, lam_im, log_step, b_re, b_im)
    gb = 256 // w
    eye = jnp.eye(gb, dtype=F32)

    def bd_in(a):
        return jnp.einsum("kgip,gh->kgihp", a.reshape(g // gb, gb, w, p), eye).reshape(g // gb, gb * w, gb * p)

    def bd_out(a):
        return jnp.einsum("kgip,gh->kgphi", a.reshape(g // gb, gb, w, p), eye).reshape(g // gb, gb * p, gb * w)

    return (bd_in(bbr).astype(BF16), bd_in(bbi).astype(BF16), pwr.reshape(8, g * p), pwi.reshape(8, g * p),
            bd_out(c_re).astype(BF16), bd_out(c_im).astype(BF16))


def _ssm_prompt(x, mod, g, tables, d_skip, w1, b1, w2, b2, n_seq, tl):
    n, d = x.shape
    t = n // n_seq
    n_state = tables[2].shape[1]
    consts = tables + (d_skip[None], w1, b1[None], w2, b2[None])
    row_spec = pl.BlockSpec((tl, d), lambda b, c: (b * (t // tl) + c, 0))
    st_spec = pl.BlockSpec((None, 1, n_state), lambda b, c: (b, 0, 0))
    return pl.pallas_call(
        functools.partial(_ssm_prompt_body, lane_block=1024),
        grid=(n_seq, t // tl),
        in_specs=[row_spec, pl.BlockSpec((6, None, 1, d), lambda b, c: (0, b, 0, 0)), _const_spec(g, 2)]
        + [_const_spec(a, 2) for a in consts],
        out_specs=[row_spec, st_spec, st_spec],
        out_shape=[jax.ShapeDtypeStruct((n, d), F32), jax.ShapeDtypeStruct((n_seq, 1, n_state), F32),
                   jax.ShapeDtypeStruct((n_seq, 1, n_state), F32)],
        scratch_shapes=[pltpu.VMEM((tl, n_state), F32), pltpu.VMEM((tl, n_state), F32),
                        pltpu.VMEM((8, n_state), F32), pltpu.VMEM((8, n_state), F32), pltpu.VMEM((tl, d), F32)],
        compiler_params=_params("arbitrary", "arbitrary"),
        name="ssm_prompt",
    )(x, mod, g, *consts)


def _ssm_sample(x, mod, g, tables, d_skip, w1, b1, w2, b2, h_re, h_im):
    n, d = x.shape
    n_state = tables[2].shape[1]
    consts = tables + (d_skip[None], w1, b1[None], w2, b2[None], h_re, h_im)
    full = pl.BlockSpec((n, d), lambda i: (0, 0))
    st = pl.BlockSpec((n, n_state), lambda i: (0, 0))
    return pl.pallas_call(
        _ssm_sample_body,
        grid=(1,),
        in_specs=[full, _mod_spec(mod, 1), _const_spec(g)] + [_const_spec(a) for a in consts],
        out_specs=[full, st, st],
        out_shape=[jax.ShapeDtypeStruct((n, d), F32), jax.ShapeDtypeStruct((n, n_state), F32),
                   jax.ShapeDtypeStruct((n, n_state), F32)],
        scratch_shapes=[pltpu.VMEM((n, n_state), F32), pltpu.VMEM((n, n_state), F32), pltpu.VMEM((n, d), F32)],
        compiler_params=_params("arbitrary"),
        name="ssm_sample",
    )(x, mod, g, *consts)


def _nsa_proj_body(x_ref, m_ref, g_ref, w_ref, q_ref, kc_ref, ks_ref, kw_ref, gt_ref, ksb_ref, kwb_ref):
    h = _modulate(x_ref[...], g_ref[0:1], m_ref[0], m_ref[1]).astype(BF16)
    qc, kc = q_ref.shape[1], kc_ref.shape[1]
    q_ref[...] = (_dot(h, w_ref[:, :qc]) * (NSA_HEAD_DIM ** -0.5)).astype(BF16)
    kc_ref[...] = _dot(h, w_ref[:, qc:qc + kc])
    ks = _dot(h, w_ref[:, qc + kc:qc + 2 * kc])
    kw = _dot(h, w_ref[:, qc + 2 * kc:qc + 3 * kc])
    ks_ref[...] = ks
    kw_ref[...] = kw
    ksb_ref[...] = ks.astype(BF16)
    kwb_ref[...] = kw.astype(BF16)
    gt_ref[...] = jax.nn.sigmoid(_dot(h, w_ref[:, qc + 3 * kc:]))


def _nsa_proj(x, mod, g, w_in, tm):
    n, d = x.shape
    n_tiles = n // tm
    kc = 2 * NSA_KV_HEADS * NSA_HEAD_DIM
    ng = (w_in.shape[1] - d - 3 * kc)
    widths = [(d, BF16), (kc, F32), (kc, F32), (kc, F32), (ng, F32), (kc, BF16), (kc, BF16)]
    return pl.pallas_call(
        _nsa_proj_body,
        grid=(n_tiles,),
        in_specs=[pl.BlockSpec((tm, d), lambda i: (i, 0)), _mod_spec(mod, n_tiles), _const_spec(g),
                  _const_spec(w_in)],
        out_specs=[pl.BlockSpec((tm, w), lambda i: (i, 0)) for w, _ in widths],
        out_shape=[jax.ShapeDtypeStruct((n, w), dt) for w, dt in widths],
        compiler_params=_params("arbitrary"),
        name="nsa_proj",
    )(x, mod, g, w_in)


def _compress_step(x_of, ls, pe_ref, w1_ref, w2_ref, o_ref, acc_ref, n_l):
    dh = NSA_HEAD_DIM
    hid = w1_ref.shape[3]
    n_zg = 2 * NSA_KV_HEADS

    @pl.when(ls == 0)
    def _():
        acc_ref[...] = jnp.zeros_like(acc_ref)

    for ll in range(n_l):
        l = ls * n_l + ll
        xb = (x_of(ll) + pe_ref[pl.ds(l, 1), :]).astype(BF16)
        for zg in range(n_zg):
            acc_ref[:, zg * hid:(zg + 1) * hid] += _dot(xb[:, zg * dh:(zg + 1) * dh], w1_ref[zg // NSA_KV_HEADS, l])

    @pl.when(ls == pl.num_programs(1) - 1)
    def _():
        a = acc_ref[...]
        hidv = (a * jax.nn.sigmoid(a)).astype(BF16)
        for zg in range(n_zg):
            o_ref[:, zg * dh:(zg + 1) * dh] = _dot(hidv[:, zg * hid:(zg + 1) * hid], w2_ref[zg // NSA_KV_HEADS])


def _compress_prompt_body(x_ref, pe_ref, w1_ref, w2_ref, o_ref, acc_ref, *, n_l):
    _compress_step(lambda ll: x_ref[:, ll, :], pl.program_id(1), pe_ref, w1_ref, w2_ref, o_ref, acc_ref, n_l)


def _compress_sample_body(pt_ref, cache_ref, pe_ref, w1_ref, w2_ref, o_ref, buf_ref, sem, acc_ref, *,
                          pages_per_step, d_tiles):
    pg, ds = pl.program_id(0), pl.program_id(1)
    n_zg = 2 * NSA_KV_HEADS
    dh = NSA_HEAD_DIM
    per_page = o_ref.shape[0]
    hid = w1_ref.shape[3] // per_page

    def page_copy(p):
        page = pt_ref[pg * pages_per_step + p]
        return pltpu.make_async_copy(cache_ref.at[page, :, pl.ds(ds * d_tiles, d_tiles)],
                                     buf_ref.at[:, :, pl.ds(p * 8, 8), :], sem)

    def start(p, c):
        page_copy(p).start()
        return c

    def wait(p, c):
        page_copy(p).wait()
        return c

    lax.fori_loop(0, pages_per_step, start, 0)

    @pl.when(ds == 0)
    def _():
        acc_ref[...] = jnp.zeros_like(acc_ref)

    lax.fori_loop(0, pages_per_step, wait, 0)
    for zg in range(n_zg):
        z = zg // NSA_KV_HEADS
        for dt in range(d_tiles):
            for dd in range(8):
                d = dt * 8 + dd
                x = buf_ref[zg, dt, pl.ds(dd, pages_per_step, stride=8), :] + pe_ref[zg, d:d + 1, :]
                acc_ref[zg] += _dot(x, w1_ref[z, d])

    @pl.when(ds == pl.num_programs(1) - 1)
    def _():
        for zg in range(n_zg):
            a = acc_ref[zg]
            hidv = (a * jax.nn.sigmoid(a)).astype(BF16)
            for n in range(per_page):
                o_ref[n, :, zg * dh:(zg + 1) * dh] = _dot(hidv[:, n * hid:(n + 1) * hid], w2_ref[zg // NSA_KV_HEADS])


def _pe_rows(pe):
    blk = pe.shape[1]
    return jnp.broadcast_to(pe.transpose(1, 0, 2)[:, :, None, :], (blk, 2, NSA_KV_HEADS, pe.shape[2])).reshape(blk, -1)


_COMPRESS_ROWS = 16


def _compress_prompt(kc, pe, w1, w2, nbt):
    n, c = kc.shape
    nblk = n // NSA_BLOCK
    n_l = _COMPRESS_ROWS
    x3 = kc.reshape(nblk, NSA_BLOCK, c)
    return pl.pallas_call(
        functools.partial(_compress_prompt_body, n_l=n_l),
        grid=(nblk // nbt, NSA_BLOCK // n_l),
        in_specs=[pl.BlockSpec((nbt, n_l, c), lambda i, l: (i, l, 0)), _const_spec(pe, 2), _const_spec(w1, 2),
                  _const_spec(w2, 2)],
        out_specs=pl.BlockSpec((nbt, c), lambda i, l: (i, 0)),
        out_shape=jax.ShapeDtypeStruct((nblk, c), F32),
        scratch_shapes=[pltpu.VMEM((nbt, 2 * NSA_KV_HEADS * w1.shape[3]), F32)],
        compiler_params=_params("arbitrary", "arbitrary"),
        name="compress_prompt",
    )(x3, pe, w1, w2)


def _rows_on_lanes(cache):
    return jnp.transpose(cache, (0, 2, 3, 4, 1))


def _compress_sample(page_table, cache, pe, w1, w2, pages_per_step):
    n_pool, page = cache.shape[:2]
    dh = NSA_HEAD_DIM
    n_zg = 2 * NSA_KV_HEADS
    c = n_zg * dh
    per_page = page // NSA_BLOCK
    hid = w1.shape[3]
    d_tiles = 2
    n_pages_total = page_table.size
    cache_t = _rows_on_lanes(cache).reshape(n_pool, n_zg, dh // 8, 8, page)
    pe_t = jnp.tile(jnp.repeat(jnp.swapaxes(pe, 1, 2), NSA_KV_HEADS, axis=0), (1, 1, per_page))
    w1_t = jnp.einsum("zlde,nm->zdnlme", w1, jnp.eye(per_page, dtype=w1.dtype)).reshape(2, dh, page, per_page * hid)
    grid_spec = pltpu.PrefetchScalarGridSpec(
        num_scalar_prefetch=1,
        grid=(n_pages_total // pages_per_step, dh // (8 * d_tiles)),
        in_specs=[pl.BlockSpec(memory_space=pl.ANY),
                  pl.BlockSpec((n_zg, 8 * d_tiles, page), lambda i, s, pt: (0, s, 0)),
                  pl.BlockSpec((2, 8 * d_tiles, page, per_page * hid), lambda i, s, pt: (0, s, 0, 0)),
                  pl.BlockSpec(w2.shape, lambda i, s, pt: (0, 0, 0))],
        out_specs=pl.BlockSpec((per_page, pages_per_step, c), lambda i, s, pt: (0, i, 0)),
        scratch_shapes=[pltpu.VMEM((n_zg, d_tiles, pages_per_step * 8, page), F32), pltpu.SemaphoreType.DMA(()),
                        pltpu.VMEM((n_zg, pages_per_step, per_page * hid), F32)],
    )
    out = pl.pallas_call(
        functools.partial(_compress_sample_body, pages_per_step=pages_per_step, d_tiles=d_tiles),
        grid_spec=grid_spec,
        out_shape=jax.ShapeDtypeStruct((per_page, n_pages_total, c), F32),
        compiler_params=_params("arbitrary", "arbitrary"),
        name="compress_sample",
    )(page_table.reshape(-1), cache_t, pe_t, w1_t, w2)
    return jnp.swapaxes(out, 0, 1).reshape(n_pages_total * per_page, c)


def _stack_heads(q, grp):
    dh = NSA_HEAD_DIM
    rep = q.shape[1] // (NSA_KV_HEADS * dh)
    base = grp * rep * dh
    return jnp.concatenate([q[:, base + r * dh:base + (r + 1) * dh] for r in range(rep)], axis=0)


def _cmp_branch(qs, cmpv, grp, t_row, rep):
    dh = NSA_HEAD_DIM
    kv = NSA_KV_HEADS * dh
    kc = cmpv[:, grp * dh:(grp + 1) * dh]
    vc = cmpv[:, kv + grp * dh:kv + (grp + 1) * dh]
    s = _dot_nt(qs, kc)
    n = lax.broadcasted_iota(jnp.int32, s.shape, 1)
    mask = (n + 1) * NSA_BLOCK <= t_row + 1
    s = jnp.where(mask, s, NEG_BIG)
    e = jnp.where(mask, jnp.exp(s - jnp.max(s, axis=-1, keepdims=True)), 0.0)
    p = e / jnp.maximum(jnp.sum(e, axis=-1, keepdims=True), 1e-30)
    o = _dot(p, vc)
    t = p.shape[0] // rep
    imp = p[0:t]
    for r in range(1, rep):
        imp = imp + p[r * t:(r + 1) * t]
    return o, imp


def _topk_mask(score, axis):
    idx = lax.broadcasted_iota(jnp.int32, score.shape, axis).astype(F32)
    n = float(score.shape[axis])
    sel = jnp.zeros(score.shape, F32)
    x = score
    for _ in range(NSA_TOPK):
        m = jnp.max(x, axis=axis, keepdims=True)
        first = jnp.min(jnp.where(x == m, idx, n), axis=axis, keepdims=True)
        pick = idx == first
        sel = jnp.where(pick, 1.0, sel)
        x = jnp.where(pick, -jnp.inf, x)
    return sel


def _cmpattn_prompt_body(q_ref, cmp_ref, o_ref, sel_ref):
    tq = q_ref.shape[0]
    nb = cmp_ref.shape[0]
    dh = NSA_HEAD_DIM
    rep = q_ref.shape[1] // (NSA_KV_HEADS * dh)
    t0 = pl.program_id(1) * tq
    q = q_ref[...]
    cmpv = cmp_ref[...]
    t_row = t0 + lax.broadcasted_iota(jnp.int32, (rep * tq, 1), 0) % tq
    blk = lax.broadcasted_iota(jnp.int32, (nb, tq), 0)
    jt = (t0 + lax.broadcasted_iota(jnp.int32, (nb, tq), 1)) // NSA_BLOCK
    forced = (blk == 0) | (blk == jt) | (blk == jt - 1)
    for grp in range(NSA_KV_HEADS):
        o, imp = _cmp_branch(_stack_heads(q, grp), cmpv, grp, t_row, rep)
        for r in range(rep):
            h = grp * rep + r
            o_ref[:, h * dh:(h + 1) * dh] = o[r * tq:(r + 1) * tq]
        score = jnp.where(blk <= jt, jnp.where(forced, SEL_FORCE, imp.T), SEL_MASKED)
        sel = _topk_mask(score, 0) * (score > 0.5 * SEL_MASKED).astype(F32)
        sel_ref[:, grp * nb:(grp + 1) * nb] = jnp.where(sel.T > 0.5, 0.0, NEG_BIG).astype(BF16)


def _cmpattn_prompt(q, cmp, n_seq, tq):
    n, qc = q.shape
    t = n // n_seq
    nb = cmp.shape[0] // n_seq
    return pl.pallas_call(
        _cmpattn_prompt_body,
        grid=(n_seq, t // tq),
        in_specs=[pl.BlockSpec((tq, qc), lambda b, i: (b * (t // tq) + i, 0)),
                  pl.BlockSpec((nb, cmp.shape[1]), lambda b, i: (b, 0))],
        out_specs=[pl.BlockSpec((tq, qc), lambda b, i: (b * (t // tq) + i, 0)),
                   pl.BlockSpec((tq, NSA_KV_HEADS * nb), lambda b, i: (b * (t // tq) + i, 0))],
        out_shape=[jax.ShapeDtypeStruct((n, qc), F32), jax.ShapeDtypeStruct((n, NSA_KV_HEADS * nb), BF16)],
        compiler_params=_params("arbitrary", "arbitrary"),
        name="cmpattn_prompt",
    )(q, cmp)


def _cmpattn_sample_body(q_ref, cmp_ref, o_ref, idx_ref, *, t_pos, n_cand):
    nb = cmp_ref.shape[0]
    dh = NSA_HEAD_DIM
    rep = q_ref.shape[1] // (NSA_KV_HEADS * dh)
    q = q_ref[...]
    cmpv = cmp_ref[...]
    t_row = jnp.full((rep, 1), t_pos, jnp.int32)
    width = idx_ref.shape[1]
    lanes = ((n_cand + 127) // 128) * 128
    blk = lax.broadcasted_iota(jnp.int32, (1, lanes), 1)
    jt = t_pos // NSA_BLOCK
    forced = (blk == 0) | (blk == jt) | (blk == jt - 1)
    col = lax.broadcasted_iota(jnp.int32, (1, width), 1)
    blk_f = blk.astype(F32)
    for grp in range(NSA_KV_HEADS):
        o, imp = _cmp_branch(_stack_heads(q, grp), cmpv, grp, t_row, rep)
        for r in range(rep):
            h = grp * rep + r
            o_ref[:, h * dh:(h + 1) * dh] = o[r:r + 1]
        imp = jnp.concatenate([imp, jnp.zeros((1, lanes - nb), F32)], axis=1)
        score = jnp.where(blk <= jt, jnp.where(forced, SEL_FORCE, imp), SEL_MASKED)
        x = jnp.where(blk < n_cand, score, -jnp.inf)
        row = jnp.full((1, width), -1, jnp.int32)
        for k in range(NSA_TOPK):
            m = jnp.max(x, axis=1, keepdims=True)
            first = jnp.min(jnp.where(x == m, blk_f, float(lanes)), axis=1, keepdims=True)
            chosen = jnp.where(m > 0.5 * SEL_MASKED, first, -1.0).astype(jnp.int32)
            row = jnp.where(col == k, chosen, row)
            x = jnp.where(blk_f == first, -jnp.inf, x)
        idx_ref[grp:grp + 1, :] = row


def _cmpattn_sample(q, cmp, t_pos, n_cand):
    n_seq, qc = q.shape
    nb = cmp.shape[0] // n_seq
    return pl.pallas_call(
        functools.partial(_cmpattn_sample_body, t_pos=t_pos, n_cand=n_cand),
        grid=(n_seq,),
        in_specs=[pl.BlockSpec((None, 1, qc), lambda b: (b, 0, 0)), pl.BlockSpec((nb, cmp.shape[1]), lambda b: (b, 0))],
        out_specs=[pl.BlockSpec((None, 1, qc), lambda b: (b, 0, 0)),
                   pl.BlockSpec((None, NSA_KV_HEADS, 128), lambda b: (b, 0, 0))],
        out_shape=[jax.ShapeDtypeStruct((n_seq, 1, qc), F32), jax.ShapeDtypeStruct((n_seq, NSA_KV_HEADS, 128), jnp.int32)],
        compiler_params=_params("arbitrary"),
        name="cmpattn_sample",
    )(q[:, None, :], cmp)


def _attn_prompt_body(q_ref, sel_ref, ks_ref, kw_ref, oslc_ref, owin_ref, qa_ref, m_ref, l_ref, acc_ref, *, tk):
    tq = q_ref.shape[0]
    dh = NSA_HEAD_DIM
    slab = 2 * dh
    kv = NSA_KV_HEADS * dh
    rep = q_ref.shape[1] // kv
    rows = rep * tq
    nb = sel_ref.shape[1] // NSA_KV_HEADS
    t0 = pl.program_id(1) * tq
    t_col = t0 + lax.broadcasted_iota(jnp.int32, (tq, 1), 0)
    lane = lax.broadcasted_iota(jnp.int32, (tq, slab), 1)

    for grp in range(NSA_KV_HEADS):
        off = (grp % 2) * dh
        parts = []
        for r in range(rep):
            h = grp * rep + r
            x = q_ref[:, (h // 2) * slab:(h // 2 + 1) * slab].astype(F32)
            if h % 2 != grp % 2:
                x = pltpu.roll(x, dh, axis=1)
            parts.append(jnp.where((lane >= off) & (lane < off + dh), x, 0.0))
        qa_ref[grp, :, 0:slab] = jnp.concatenate(parts, axis=0).astype(BF16)
        qa_ref[grp, :, slab:slab + nb] = jnp.concatenate([sel_ref[:, grp * nb:(grp + 1) * nb]] * rep, axis=0)
    m_ref[...] = jnp.full(m_ref.shape, 0.1 * NEG_BIG, F32)
    l_ref[...] = jnp.zeros_like(l_ref)
    acc_ref[...] = jnp.zeros_like(acc_ref)

    def kv_tile(j, causal):
        k0 = pl.multiple_of(j * tk, tk)
        key_blk = (k0 + lax.broadcasted_iota(jnp.int32, (tk, nb), 0)) // NSA_BLOCK
        onehot = (key_blk == lax.broadcasted_iota(jnp.int32, (tk, nb), 1)).astype(BF16)
        if causal:
            late = k0 + lax.broadcasted_iota(jnp.int32, (1, tk), 1) > t_col
            causal_bias = jnp.where(late, NEG_BIG, 0.0)
        for grp in range(NSA_KV_HEADS):
            pair = (grp // 2) * slab
            k_aug = jnp.concatenate([ks_ref[pl.ds(k0, tk), pair:pair + slab], onehot], axis=1)
            s = _dot_nt(qa_ref[grp], k_aug)
            if causal:
                s = (s.reshape(rep, tq, tk) + causal_bias[None]).reshape(rows, tk)
            cols = [s[:, c * slab:(c + 1) * slab] for c in range(tk // slab)]
            mx = functools.reduce(jnp.maximum, cols)
            m_old = m_ref[grp]
            m_new = jnp.maximum(m_old, jnp.max(mx, axis=-1, keepdims=True))
            alpha = jnp.exp(m_old - m_new)
            ps = [jnp.exp(c - m_new) for c in cols]
            l_ref[grp] = alpha * l_ref[grp] + functools.reduce(jnp.add, ps)
            p = jnp.concatenate([c.astype(BF16) for c in ps], axis=1)
            acc_ref[grp] = alpha * acc_ref[grp] + _dot(p, ks_ref[pl.ds(k0, tk), kv + pair:kv + pair + slab])
            m_ref[grp] = m_new

    n_full = t0 // tk

    def full_tile(j, carry):
        kv_tile(j, False)
        return carry

    lax.fori_loop(0, n_full, full_tile, 0)
    kv_tile(n_full, True)

    win_len = NSA_WINDOW + tq
    w0 = pl.multiple_of(jnp.maximum(t0 - NSA_WINDOW, 0), tq)
    wpos = w0 + lax.broadcasted_iota(jnp.int32, (1, win_len), 1)
    win_bias = jnp.where((wpos <= t_col) & (wpos >= t_col - NSA_WINDOW), 0.0, NEG_BIG)
    for grp in range(NSA_KV_HEADS):
        off = (grp % 2) * dh
        pair = (grp // 2) * slab
        l_sum = jnp.sum(l_ref[grp], axis=-1, keepdims=True)
        o = acc_ref[grp] / jnp.maximum(l_sum, 1e-30)
        for r in range(rep):
            h = grp * rep + r
            oslc_ref[:, h * dh:(h + 1) * dh] = o[r * tq:(r + 1) * tq, off:off + dh]

        s = _dot_nt(qa_ref[grp, :, 0:slab], kw_ref[pl.ds(w0, win_len), pair:pair + slab])
        s = s.reshape(rep, tq, win_len) + win_bias[None]
        m = jnp.maximum(jnp.max(s, axis=-1, keepdims=True), 0.1 * NEG_BIG)
        e = jnp.exp(s - m)
        o = _dot(e.reshape(rows, win_len), kw_ref[pl.ds(w0, win_len), kv + pair:kv + pair + slab])
        o = o / jnp.maximum(jnp.sum(e, axis=-1, keepdims=True).reshape(rows, 1), 1e-30)
        for r in range(rep):
            h = grp * rep + r
            owin_ref[:, h * dh:(h + 1) * dh] = o[r * tq:(r + 1) * tq, off:off + dh]


def _attn_prompt(q, sel, ksb, kwb, n_seq, tq, tk):
    n, qc = q.shape
    t = n // n_seq
    dh = NSA_HEAD_DIM
    rep = qc // (NSA_KV_HEADS * dh)
    assert t % tk == 0 and tk % tq == 0 and t >= NSA_WINDOW + tq and NSA_WINDOW % tq == 0
    tile = pl.BlockSpec((tq, qc), lambda b, i: (b * (t // tq) + i, 0))
    seq = pl.BlockSpec((t, ksb.shape[1]), lambda b, i: (b, 0))
    return pl.pallas_call(
        functools.partial(_attn_prompt_body, tk=tk),
        grid=(n_seq, t // tq),
        in_specs=[tile, pl.BlockSpec((tq, sel.shape[1]), lambda b, i: (b * (t // tq) + i, 0)), seq, seq],
        out_specs=[tile, tile],
        out_shape=[jax.ShapeDtypeStruct((n, qc), F32), jax.ShapeDtypeStruct((n, qc), F32)],
        scratch_shapes=[pltpu.VMEM((NSA_KV_HEADS, rep * tq, 2 * dh + sel.shape[1] // NSA_KV_HEADS), BF16)]
        + [pltpu.VMEM((NSA_KV_HEADS, rep * tq, 2 * dh), F32)] * 3,
        compiler_params=_params("arbitrary", "arbitrary"),
        name="attn_prompt",
    )(q, sel, ksb, kwb)


def _softmax_with_new_key(s, ok, s_new, new_ok):
    s = jnp.where(ok, s, NEG_BIG)
    s_new = jnp.where(new_ok, s_new, NEG_BIG)
    m = jnp.maximum(jnp.max(s, axis=-1, keepdims=True), s_new)
    e = jnp.where(ok, jnp.exp(s - m), 0.0)
    e_new = jnp.where(new_ok, jnp.exp(s_new - m), 0.0)
    return e, e_new, jnp.maximum(jnp.sum(e, axis=-1, keepdims=True) + e_new, 1e-30)


def _bf16_round(x):
    return x.astype(BF16).astype(F32)


def _attn_sample_body(pt_ref, idx_ref, q_ref, ksn_ref, kwn_ref, win_ref, cache_ref, oslc_ref, owin_ref,
                      kbuf_ref, sem, *, t_pos, nb_past, n_pages):
    b = pl.program_id(0)
    dh = NSA_HEAD_DIM
    kv = NSA_KV_HEADS * dh
    rep = q_ref.shape[1] // kv
    n_sel = NSA_TOPK
    page = cache_ref.shape[4]
    per_page = page // NSA_BLOCK
    q = q_ref[...]

    def sel_index(grp, k):
        return idx_ref[(b * NSA_KV_HEADS + grp) * 128 + k]

    def in_pool(idx):
        return (idx >= 0) & (idx < nb_past)

    def page_copy(grp, k, idx):
        phys = pt_ref[b * n_pages + jnp.minimum(idx // per_page, n_pages - 1)]
        return pltpu.make_async_copy(cache_ref.at[phys, :, grp], kbuf_ref.at[grp, :, :, pl.ds(k * page, page)], sem)

    for grp in range(NSA_KV_HEADS):
        for k in range(n_sel):
            idx = sel_index(grp, k)

            @pl.when(in_pool(idx))
            def _():
                page_copy(grp, k, idx).start()

            @pl.when(jnp.logical_not(in_pool(idx)))
            def _():
                kbuf_ref[grp, :, :, k * page:(k + 1) * page] = jnp.zeros((2, dh, page), F32)

    for grp in range(NSA_KV_HEADS):
        for k in range(n_sel):
            idx = sel_index(grp, k)

            @pl.when(in_pool(idx))
            def _():
                page_copy(grp, k, idx).wait()

    lane = lax.broadcasted_iota(jnp.int32, (1, n_sel * page), 1)
    wb = win_ref.shape[3]
    wpos = t_pos - wb + lax.broadcasted_iota(jnp.int32, (1, wb), 1)
    win_ok = (wpos <= t_pos) & (wpos >= t_pos - NSA_WINDOW) & (wpos >= 0)
    for grp in range(NSA_KV_HEADS):
        qs = _stack_heads(q, grp)
        qf = qs.astype(F32)
        ok = jnp.zeros((1, n_sel * page), jnp.bool_)
        has_new = False
        for k in range(n_sel):
            idx = sel_index(grp, k)
            row = lane - k * page
            kpos = (idx // per_page) * page + row
            ok = ok | ((lane // page == k) & in_pool(idx) & (row // NSA_BLOCK == idx % per_page) & (kpos <= t_pos))
            has_new = has_new | (idx >= nb_past)
        new_ok = has_new & (nb_past * NSA_BLOCK <= t_pos)
        k_new = _bf16_round(ksn_ref[:, grp * dh:(grp + 1) * dh])
        v_new = _bf16_round(ksn_ref[:, kv + grp * dh:kv + (grp + 1) * dh])
        s_new = jnp.sum(qf * k_new, axis=-1, keepdims=True)
        e, e_new, den = _softmax_with_new_key(_dot(qs, kbuf_ref[grp, 0]), ok, s_new, new_ok)
        o = (_dot_nt(e, kbuf_ref[grp, 1]) + _bf16_round(e_new) * v_new) / den
        for r in range(rep):
            h = grp * rep + r
            oslc_ref[:, h * dh:(h + 1) * dh] = o[r:r + 1]

        k_new = _bf16_round(kwn_ref[:, grp * dh:(grp + 1) * dh])
        v_new = _bf16_round(kwn_ref[:, kv + grp * dh:kv + (grp + 1) * dh])
        s_new = jnp.sum(qf * k_new, axis=-1, keepdims=True)
        e, e_new, den = _softmax_with_new_key(_dot(qs, win_ref[0, grp]), win_ok, s_new, True)
        o = (_dot_nt(e, win_ref[1, grp]) + _bf16_round(e_new) * v_new) / den
        for r in range(rep):
            h = grp * rep + r
            owin_ref[:, h * dh:(h + 1) * dh] = o[r:r + 1]


def _attn_sample(page_table, sel_idx, q, ks_new, kw_new, win, cache, t_pos):
    n_seq, qc = q.shape
    n_pool, page = cache.shape[:2]
    dh = NSA_HEAD_DIM
    c = 2 * NSA_KV_HEADS * dh
    n_pages = page_table.shape[1]
    wb = win.shape[1]
    row3 = lambda w: pl.BlockSpec((None, 1, w), lambda b, pt, ix: (b, 0, 0))
    grid_spec = pltpu.PrefetchScalarGridSpec(
        num_scalar_prefetch=2,
        grid=(n_seq,),
        in_specs=[row3(qc), row3(c), row3(c),
                  pl.BlockSpec((None, 2, NSA_KV_HEADS, dh, wb), lambda b, pt, ix: (b, 0, 0, 0, 0)),
                  pl.BlockSpec(memory_space=pl.ANY)],
        out_specs=[row3(qc), row3(qc)],
        scratch_shapes=[pltpu.VMEM((NSA_KV_HEADS, 2, dh, NSA_TOPK * page), F32), pltpu.SemaphoreType.DMA(())],
    )
    win, cache = _rows_on_lanes(win), _rows_on_lanes(cache)
    return pl.pallas_call(
        functools.partial(_attn_sample_body, t_pos=t_pos, nb_past=t_pos // NSA_BLOCK, n_pages=n_pages),
        grid_spec=grid_spec,
        out_shape=[jax.ShapeDtypeStruct((n_seq, 1, qc), F32), jax.ShapeDtypeStruct((n_seq, 1, qc), F32)],
        compiler_params=_params("arbitrary"),
        name="attn_sample",
    )(page_table.reshape(-1), sel_idx.reshape(-1), q[:, None, :], ks_new[:, None, :], kw_new[:, None, :], win, cache)


def _nsa_merge_body(x_ref, m_ref, g_ref, oc_ref, os_ref, ow_ref, gt_ref, wout_ref, o_ref, om_ref):
    dh = NSA_HEAD_DIM
    n_heads = oc_ref.shape[1] // dh
    gt = gt_ref[...]
    for h in range(n_heads):
        c = slice(h * dh, (h + 1) * dh)
        o = (gt[:, h:h + 1] * oc_ref[:, c] + gt[:, n_heads + h:n_heads + h + 1] * os_ref[:, c]
             + gt[:, 2 * n_heads + h:2 * n_heads + h + 1] * ow_ref[:, c])
        om_ref[:, c] = o.astype(BF16)
    x = x_ref[...]
    o_ref[...] = x + m_ref[2] * _rms(_dot(om_ref[...], wout_ref[...]), g_ref[1:2])


def _nsa_merge(x, mod, g, o_cmp, o_slc, o_win, gates, w_out, tm):
    n, d = x.shape
    n_tiles = n // tm
    qc = o_cmp.shape[1]
    tile = lambda w: pl.BlockSpec((tm, w), lambda i: (i, 0))
    return pl.pallas_call(
        _nsa_merge_body,
        grid=(n_tiles,),
        in_specs=[tile(d), _mod_spec(mod, n_tiles), _const_spec(g), tile(qc), tile(qc), tile(qc),
                  tile(gates.shape[1]), _const_spec(w_out)],
        out_specs=tile(d),
        out_shape=jax.ShapeDtypeStruct((n, d), F32),
        scratch_shapes=[pltpu.VMEM((tm, qc), BF16)],
        compiler_params=_params("arbitrary"),
        name="nsa_merge",
    )(x, mod, g, o_cmp, o_slc, o_win, gates, w_out)


def _nsa_layer(xp, xs, mod_p, mod_s, g, n_seq, cache_cmp, cache_slc, cache_win, page_table,
               w_in, w1, w2, pe, w_out, tm):
    n_s = xs.shape[0]
    t = xp.shape[0] // n_seq
    page_size = cache_cmp.shape[1]
    past = page_table.shape[1] * page_size
    assert t % NSA_BLOCK == 0 and past % NSA_BLOCK == 0 and page_size % NSA_BLOCK == 0
    w_in_b, w_out_b = w_in.astype(BF16), w_out.astype(BF16)
    w1_b, w2_b = w1.astype(BF16), w2.astype(BF16)
    pe_rows = _pe_rows(pe)

    q, kc, ks, kw, gates, ksb, kwb = _nsa_proj(xp, mod_p, g, w_in_b, tm)
    cmp_p = _compress_prompt(kc, pe_rows, w1_b, w2_b, min(256, kc.shape[0] // NSA_BLOCK))
    o_cmp, sel = _cmpattn_prompt(q, cmp_p, n_seq, 128)
    o_slc, o_win = _attn_prompt(q, sel, ksb, kwb, n_seq, 128, 512)
    xp = _nsa_merge(xp, mod_p, g, o_cmp, o_slc, o_win, gates, w_out_b, tm)

    q_s, kc_s, ks_s, kw_s, gates_s, _, _ = _nsa_proj(xs, mod_s, g, w_in_b, n_s)
    cmp_s = _compress_sample(page_table, cache_cmp, pe.astype(F32), w1_b, w2_b, min(256, page_table.size))
    n_cand = -(-(past + 1) // NSA_BLOCK)
    o_cmp_s, sel_idx = _cmpattn_sample(q_s, cmp_s, past, n_cand)
    o_slc_s, o_win_s = _attn_sample(page_table, sel_idx, q_s, ks_s, kw_s, cache_win, cache_slc, past)
    xs = _nsa_merge(xs, mod_s, g, o_cmp_s.reshape(n_s, -1), o_slc_s.reshape(n_s, -1), o_win_s.reshape(n_s, -1),
                    gates_s, w_out_b, n_s)
    return xp, xs, (kc, ks, kw), (kc_s, ks_s, kw_s)


def kernel(x_prompt, x_sample, cache_nsa_cmp, cache_nsa_slc, cache_nsa_win, state_ssm, page_table, c_prompt, c_sample, w_mod, b_mod, norm_g, ffn_w_gate, ffn_w_up, ffn_w_down, gmlp_w_in, gmlp_b_in, gmlp_ln_g, gmlp_ln_b, gmlp_w_s, gmlp_b_s, gmlp_w_out, nsa_w_in, nsa_w_cmp1, nsa_w_cmp2, nsa_pe_cmp, nsa_w_out, ssm_lambda_re, ssm_lambda_im, ssm_b_re, ssm_b_im, ssm_c_re, ssm_c_im, ssm_d, ssm_log_step, ssm_w_glu1, ssm_b_glu1, ssm_w_glu2, ssm_b_glu2):
    n_seq, t, d = x_prompt.shape
    n_s, t_s, _ = x_sample.shape
    assert t_s == 1
    depth = w_mod.shape[0]
    tm = 512 if t % 512 == 0 else 256
    kv_shape = (2, NSA_KV_HEADS, NSA_HEAD_DIM)

    xp = x_prompt.reshape(n_seq * t, d)
    xs = x_sample.reshape(n_s, d)
    m_all = _adaln(jnp.concatenate([c_prompt, c_sample], axis=0), w_mod, b_mod)
    mods_p = m_all[:, :n_seq].reshape(depth, n_seq, 6, 1, d).transpose(0, 2, 1, 3, 4)
    mods_s = m_all[:, n_seq:].reshape(depth, n_s, 6, d).transpose(0, 2, 1, 3)[:, :, None]

    cmp_p, cmp_s, slc_p, slc_s, win_p, win_s, ssm_p, ssm_s, gv_s = [], [], [], [], [], [], [], [], []
    for i in range(depth):
        j = i // N_MIXERS
        mp, ms, g = mods_p[i], mods_s[i], norm_g[i]
        if i % N_MIXERS == 0:
            gw = (gmlp_w_in[j].astype(BF16), gmlp_b_in[j], gmlp_ln_g[j], gmlp_ln_b[j], gmlp_w_s[j], gmlp_b_s[j],
                  gmlp_w_out[j].astype(BF16))
            assert t % GMLP_CHUNK == 0 and gmlp_w_s.shape[2] == GMLP_CHUNK
            xp = _gmlp_prompt(xp, mp, g, *gw, tm)
            xs, v_new = _gmlp_sample(xs, ms, g, *gw)
            gv_s.append(v_new.reshape(n_s, 1, -1))
        elif i % N_MIXERS == 1:
            xp, xs, kv_p, kv_s = _nsa_layer(xp, xs, mp, ms, g, n_seq, cache_nsa_cmp[j], cache_nsa_slc[j],
                                            cache_nsa_win[j], page_table, nsa_w_in[j], nsa_w_cmp1[j],
                                            nsa_w_cmp2[j], nsa_pe_cmp[j], nsa_w_out[j], tm)
            cmp_p.append(kv_p[0].reshape((n_seq, t) + kv_shape))
            slc_p.append(kv_p[1].reshape((n_seq, t) + kv_shape))
            keep = min(NSA_WINDOW, t)
            win_p.append(kv_p[2].reshape((n_seq, t) + kv_shape)[:, t - keep:])
            cmp_s.append(kv_s[0].reshape((n_s, 1) + kv_shape))
            slc_s.append(kv_s[1].reshape((n_s, 1) + kv_shape))
            past = page_table.shape[1] * cache_nsa_cmp.shape[2]
            win = jnp.concatenate([cache_nsa_win[j], kv_s[2].reshape((n_s, 1) + kv_shape)], axis=1)
            win_s.append(win[:, win.shape[1] - min(NSA_WINDOW, past + 1):])
        else:
            tables = _ssm_tables(ssm_lambda_re[j], ssm_lambda_im[j], ssm_b_re[j], ssm_b_im[j], ssm_c_re[j],
                                 ssm_c_im[j], ssm_log_step[j])
            glu = (ssm_d[j], ssm_w_glu1[j].astype(BF16), ssm_b_glu1[j], ssm_w_glu2[j].astype(BF16), ssm_b_glu2[j])
            n_grp, n_st = ssm_lambda_re.shape[1:]
            xp, sr, si = _ssm_prompt(xp, mp, g, tables, *glu, n_seq, 256)
            ssm_p.append(jnp.stack([sr.reshape(n_seq, n_grp, n_st), si.reshape(n_seq, n_grp, n_st)], axis=-1))
            h0 = state_ssm[j].reshape(n_s, n_grp * n_st, 2)
            xs, sr, si = _ssm_sample(xs, ms, g, tables, *glu, h0[..., 0], h0[..., 1])
            ssm_s.append(jnp.stack([sr.reshape(n_s, n_grp, n_st), si.reshape(n_s, n_grp, n_st)], axis=-1))
        ffn_w = (ffn_w_gate[i].astype(BF16), ffn_w_up[i].astype(BF16), ffn_w_down[i].astype(BF16))
        xp = _ffn(xp, mp, g, *ffn_w, tm)
        xs = _ffn(xs, ms, g, *ffn_w, n_s)
    return (xp.reshape(n_seq, t, d), xs.reshape(n_s, 1, d), jnp.stack(cmp_p), jnp.stack(cmp_s), jnp.stack(slc_p),
            jnp.stack(slc_s), jnp.stack(win_p), jnp.stack(win_s), jnp.stack(ssm_p), jnp.stack(ssm_s), jnp.stack(gv_s))
```

```python
import functools
import math

import jax
import jax.numpy as jnp
from jax import lax
from jax.experimental import pallas as pl
from jax.experimental.pallas import tpu as pltpu

F32 = jnp.float32
BF16 = jnp.bfloat16

RMS_EPS = 1.0e-6
LN_EPS = 1.0e-5

V7X_LANES = 128
V7X_VMEM_BYTES = 64 * 1024 * 1024
VMEM_LIMIT_BYTES = V7X_VMEM_BYTES - 8 * 1024 * 1024

N_MIXERS = 3
GMLP_GROUPS = 8
GMLP_CHUNK = 128
NSA_HEAD_DIM = 64
NSA_KV_HEADS = 4
NSA_BLOCK = 64
NSA_TOPK = 16
NSA_WINDOW = 512
SEL_FORCE = 1.0e4
SEL_MASKED = -1.0
SSM_GROUP_WIDTH = 16
SSM_STATE = 64
SSM_SEGMENTS = 8
SSM_CHUNK = 256
NEG_BIG = -1.0e30


def _params(*sem):
    return pltpu.CompilerParams(dimension_semantics=sem, vmem_limit_bytes=VMEM_LIMIT_BYTES)


def _dot(a, b):
    return jnp.dot(a.astype(BF16), b.astype(BF16), preferred_element_type=F32)


def _dot_nt(a, b):
    return lax.dot_general(a.astype(BF16), b.astype(BF16), (((1,), (1,)), ((), ())),
                           preferred_element_type=F32)


def _rms(x, g):
    return x * lax.rsqrt(jnp.mean(x * x, axis=-1, keepdims=True) + RMS_EPS) * g


def _modulate(x, g, shift, scale):
    return _rms(x, g) * (1.0 + scale) + shift


def _const_spec(a, n_grid=1, single=False):
    nd = a.ndim
    idx = {1: lambda i: (0,) * nd, 2: lambda i, j: (0,) * nd, 3: lambda i, j, k: (0,) * nd}[n_grid]
    if single:
        return pl.BlockSpec(a.shape, idx, pipeline_mode=pl.Buffered(1))
    return pl.BlockSpec(a.shape, idx)


def _mod_spec(mod, n_tiles):
    _, n_seq, rows, d = mod.shape
    tiles_per_seq = n_tiles // n_seq
    return pl.BlockSpec((6, None, rows, d), lambda i: (0, i // tiles_per_seq, 0, 0))


def _adaln_body(c_ref, w_ref, b_ref, o_ref):
    c = c_ref[...]
    o_ref[...] = _dot(c * jax.nn.sigmoid(c), w_ref[...]) + b_ref[...]


def _adaln(c_all, w_mod, b_mod):
    depth, d, d6 = w_mod.shape
    m = c_all.shape[0]
    tn = 2048
    return pl.pallas_call(
        _adaln_body,
        grid=(depth, d6 // tn),
        in_specs=[pl.BlockSpec((m, d), lambda l, j: (0, 0)),
                  pl.BlockSpec((None, d, tn), lambda l, j: (l, 0, j)),
                  pl.BlockSpec((None, 1, tn), lambda l, j: (l, 0, j))],
        out_specs=pl.BlockSpec((None, m, tn), lambda l, j: (l, 0, j)),
        out_shape=jax.ShapeDtypeStruct((depth, m, d6), F32),
        compiler_params=_params("arbitrary", "arbitrary"),
        name="adaln",
    )(c_all, w_mod, b_mod.reshape(depth, 1, d6))


def _ffn_body(x_ref, m_ref, g_ref, wg_ref, wu_ref, wd_ref, o_ref, *, n_chunks):
    x = x_ref[...]
    h = _modulate(x, g_ref[2:3], m_ref[3], m_ref[4]).astype(BF16)
    fc = wg_ref.shape[1] // n_chunks
    acc = None
    for c in range(n_chunks):
        a = _dot(h, wg_ref[:, c * fc:(c + 1) * fc])
        b = _dot(h, wu_ref[:, c * fc:(c + 1) * fc])
        y = _dot(a * jax.nn.sigmoid(a) * b, wd_ref[c * fc:(c + 1) * fc, :])
        acc = y if acc is None else acc + y
    o_ref[...] = x + m_ref[5] * _rms(acc, g_ref[3:4])


def _ffn(x, mod, g, wg, wu, wd, tm):
    n, d = x.shape
    n_tiles = n // tm
    return pl.pallas_call(
        functools.partial(_ffn_body, n_chunks=2),
        grid=(n_tiles,),
        in_specs=[pl.BlockSpec((tm, d), lambda i: (i, 0)), _mod_spec(mod, n_tiles), _const_spec(g),
                  _const_spec(wg, single=True), _const_spec(wu, single=True), _const_spec(wd, single=True)],
        out_specs=pl.BlockSpec((tm, d), lambda i: (i, 0)),
        out_shape=jax.ShapeDtypeStruct((n, d), F32),
        compiler_params=_params("arbitrary"),
        name="ffn",
    )(x, mod, g, wg, wu, wd)


def _gmlp_front(x_ref, m_ref, g_ref, win_ref, bin_ref, lng_ref, lnb_ref):
    x = x_ref[...]
    h = _modulate(x, g_ref[0:1], m_ref[0], m_ref[1])
    z = jax.nn.gelu(_dot(h, win_ref[...]) + bin_ref[...])
    half = z.shape[1] // 2
    u, v = z[:, :half], z[:, half:]
    mu = jnp.mean(v, axis=-1, keepdims=True)
    var = jnp.mean(jnp.square(v - mu), axis=-1, keepdims=True)
    v = (v - mu) * lax.rsqrt(var + LN_EPS) * lng_ref[...] + lnb_ref[...]
    return x, u, v


def _gmlp_prompt_body(x_ref, m_ref, g_ref, win_ref, bin_ref, lng_ref, lnb_ref, ws_ref, bs_ref, wout_ref,
                      o_ref, um_ref):
    x, u, v = _gmlp_front(x_ref, m_ref, g_ref, win_ref, bin_ref, lng_ref, lnb_ref)
    vb = v.astype(BF16)
    n_groups, chunk, _ = ws_ref.shape
    gw = v.shape[1] // n_groups
    causal = (lax.broadcasted_iota(jnp.int32, (chunk, chunk), 0)
              >= lax.broadcasted_iota(jnp.int32, (chunk, chunk), 1))
    for grp in range(n_groups):
        w = jnp.where(causal, ws_ref[grp], 0.0).astype(BF16)
        cols = slice(grp * gw, (grp + 1) * gw)
        for k in range(x.shape[0] // chunk):
            rows = slice(k * chunk, (k + 1) * chunk)
            mixed = _dot(w, vb[rows, cols]) + bs_ref[:, grp:grp + 1]
            um_ref[rows, cols] = (u[rows, cols] * mixed).astype(BF16)
    y = _dot(um_ref[...], wout_ref[...])
    o_ref[...] = x + m_ref[2] * _rms(y, g_ref[1:2])


def _gmlp_sample_body(x_ref, m_ref, g_ref, win_ref, bin_ref, lng_ref, lnb_ref, ws_ref, bs_ref, wout_ref,
                      o_ref, v_ref):
    x, u, v = _gmlp_front(x_ref, m_ref, g_ref, win_ref, bin_ref, lng_ref, lnb_ref)
    v_ref[...] = v
    y = _dot(u * (ws_ref[...] * v + bs_ref[...]), wout_ref[...])
    o_ref[...] = x + m_ref[2] * _rms(y, g_ref[1:2])


def _gmlp_prompt(x, mod, g, w_in, b_in, ln_g, ln_b, w_s, b_s, w_out, tm):
    n, d = x.shape
    n_tiles = n // tm
    half = w_out.shape[0]
    args = (x, mod, g, w_in, b_in[None], ln_g[None], ln_b[None], w_s, b_s.T, w_out)
    return pl.pallas_call(
        _gmlp_prompt_body,
        grid=(n_tiles,),
        in_specs=[pl.BlockSpec((tm, d), lambda i: (i, 0)), _mod_spec(mod, n_tiles)]
        + [_const_spec(a) for a in args[2:]],
        out_specs=pl.BlockSpec((tm, d), lambda i: (i, 0)),
        out_shape=jax.ShapeDtypeStruct((n, d), F32),
        scratch_shapes=[pltpu.VMEM((tm, half), BF16)],
        compiler_params=_params("arbitrary"),
        name="gmlp_prompt",
    )(*args)


def _gmlp_sample(x, mod, g, w_in, b_in, ln_g, ln_b, w_s, b_s, w_out):
    n, d = x.shape
    half = w_out.shape[0]
    gw = half // w_s.shape[0]
    args = (x, mod, g, w_in, b_in[None], ln_g[None], ln_b[None],
            jnp.repeat(w_s[:, 0, 0], gw)[None], jnp.repeat(b_s[:, 0], gw)[None], w_out)
    return pl.pallas_call(
        _gmlp_sample_body,
        grid=(1,),
        in_specs=[pl.BlockSpec((n, d), lambda i: (0, 0)), _mod_spec(mod, 1)]
        + [_const_spec(a) for a in args[2:]],
        out_specs=[pl.BlockSpec((n, d), lambda i: (0, 0)), pl.BlockSpec((n, half), lambda i: (0, 0))],
        out_shape=[jax.ShapeDtypeStruct((n, d), F32), jax.ShapeDtypeStruct((n, half), F32)],
        compiler_params=_params("arbitrary"),
        name="gmlp_sample",
    )(*args)


def _cmul(ar, ai, br, bi):
    return ar * br - ai * bi, ar * bi + ai * br


def _powers(base, n):
    p = [base]
    for k in range(2, n + 1):
        p.append(_cmul(*p[k // 2 - 1], *p[k - k // 2 - 1]))
    return p


def _ssm_prep_body(lr_ref, li_ref, ls_ref, br_ref, bi_ref, pwr_ref, pwi_ref, par_ref, pai_ref, bbr_ref, bbi_ref):
    lr, li = lr_ref[...], li_ref[...]
    dt = jnp.exp(ls_ref[...])
    mag = jnp.exp(lr * dt)
    ab_re, ab_im = mag * jnp.cos(li * dt), mag * jnp.sin(li * dt)
    den = lr * lr + li * li
    f_re = ((ab_re - 1.0) * lr + ab_im * li) / den
    f_im = (ab_im * lr - (ab_re - 1.0) * li) / den
    bbr_ref[...] = f_re[:, None, :] * br_ref[...] - f_im[:, None, :] * bi_ref[...]
    bbi_ref[...] = f_re[:, None, :] * bi_ref[...] + f_im[:, None, :] * br_ref[...]
    n_steps = pwr_ref.shape[0]
    p = _powers((ab_re, ab_im), n_steps)
    for n in range(n_steps):
        pwr_ref[n] = p[n][0]
        pwi_ref[n] = p[n][1]
    a = _powers(p[n_steps - 1], SSM_SEGMENTS)
    par_ref[0] = jnp.ones_like(ab_re)
    pai_ref[0] = jnp.zeros_like(ab_re)
    for k in range(SSM_SEGMENTS):
        par_ref[k + 1] = a[k][0]
        pai_ref[k + 1] = a[k][1]


def _ssm_prep(lam_re, lam_im, log_step, b_re, b_im, n_steps):
    g, p = lam_re.shape
    w = b_re.shape[2]
    args = (lam_re, lam_im, log_step[:, None], jnp.swapaxes(b_re, 1, 2), jnp.swapaxes(b_im, 1, 2))
    return pl.pallas_call(
        _ssm_prep_body,
        out_shape=[jax.ShapeDtypeStruct((n_steps, g, p), F32)] * 2
        + [jax.ShapeDtypeStruct((SSM_SEGMENTS + 1, g, p), F32)] * 2
        + [jax.ShapeDtypeStruct((g, w, p), F32)] * 2,
        name="ssm_prep",
    )(*args)


def _ssm_input(x_ref, m_ref, g_ref, bbr_ref, bbi_ref, xr_ref, xi_ref):
    x = x_ref[...]
    u = _modulate(x, g_ref[0:1], m_ref[0], m_ref[1])
    ub = u.astype(BF16)
    n_kb, kin, kout = bbr_ref.shape
    for kb in range(n_kb):
        xr_ref[:, kb * kout:(kb + 1) * kout] = _dot(ub[:, kb * kin:(kb + 1) * kin], bbr_ref[kb])
        xi_ref[:, kb * kout:(kb + 1) * kout] = _dot(ub[:, kb * kin:(kb + 1) * kin], bbi_ref[kb])
    return x, u


def _ssm_readout(xr_ref, xi_ref, cr_ref, ci_ref):
    n_kb, kin, _ = cr_ref.shape
    return [_dot(xr_ref[:, kb * kin:(kb + 1) * kin], cr_ref[kb]) - _dot(xi_ref[:, kb * kin:(kb + 1) * kin], ci_ref[kb])
            for kb in range(n_kb)]


def _ssm_output(x, u, y, m_ref, g_ref, d_ref, w1_ref, b1_ref, w2_ref, b2_ref, o_ref):
    gl = jax.nn.gelu(y + d_ref[...] * u)
    out = (_dot(gl, w1_ref[...]) + b1_ref[...]) * jax.nn.sigmoid(_dot(gl, w2_ref[...]) + b2_ref[...])
    o_ref[...] = x + m_ref[2] * _rms(out, g_ref[1:2])


def _ssm_prompt_body(x_ref, m_ref, g_ref, bbr_ref, bbi_ref, pwr_ref, pwi_ref, cr_ref, ci_ref, d_ref,
                     w1_ref, b1_ref, w2_ref, b2_ref, par_ref, pai_ref, o_ref, sr_ref, si_ref,
                     perm_ref, xr_ref, xi_ref, car_ref, cai_ref, *, lane_block):
    n_seg = SSM_SEGMENTS

    @pl.when(pl.program_id(1) == 0)
    def _():
        car_ref[...] = jnp.zeros_like(car_ref)
        cai_ref[...] = jnp.zeros_like(cai_ref)

    rows, n_state = xr_ref.shape
    n_steps = rows // n_seg
    n_tiles, _, lanes = perm_ref.shape
    x = x_ref[...]
    u = _modulate(x, g_ref[0:1], m_ref[0], m_ref[1])
    for c in range(n_tiles):
        perm_ref[c] = u[:, c * lanes:(c + 1) * lanes]
    ub = jnp.concatenate(
        [jnp.concatenate([perm_ref[c, pl.ds(s, n_seg, stride=n_steps), :] for s in range(n_steps)], axis=0)
         for c in range(n_tiles)], axis=1).astype(BF16)
    n_kb, kin, kout = bbr_ref.shape
    for kb in range(n_kb):
        xr_ref[:, kb * kout:(kb + 1) * kout] = _dot(ub[:, kb * kin:(kb + 1) * kin], bbr_ref[kb])
        xi_ref[:, kb * kout:(kb + 1) * kout] = _dot(ub[:, kb * kin:(kb + 1) * kin], bbi_ref[kb])

    row = lax.broadcasted_iota(jnp.int32, (n_seg, lane_block), 0)

    def shifted(v, s):
        return jnp.where(row >= s, pltpu.roll(v, s, axis=0), 0.0)

    for cb in range(n_state // lane_block):
        cols = slice(cb * lane_block, (cb + 1) * lane_block)
        ar, ai = pwr_ref[0:1, cols], pwi_ref[0:1, cols]
        vr = vi = jnp.zeros((n_seg, lane_block), F32)
        for s in range(n_steps):
            dr, di = _cmul(ar, ai, vr, vi)
            vr = xr_ref[s * n_seg:(s + 1) * n_seg, cols] + dr
            vi = xi_ref[s * n_seg:(s + 1) * n_seg, cols] + di
            xr_ref[s * n_seg:(s + 1) * n_seg, cols] = vr
            xi_ref[s * n_seg:(s + 1) * n_seg, cols] = vi
        for s in (1, 2, 4):
            dr, di = _cmul(par_ref[s:s + 1, cols], pai_ref[s:s + 1, cols], shifted(vr, s), shifted(vi, s))
            vr, vi = vr + dr, vi + di
        in_r, in_i = car_ref[:, cols], cai_ref[:, cols]
        dr, di = _cmul(par_ref[0:n_seg, cols], pai_ref[0:n_seg, cols], in_r, in_i)
        seg_r, seg_i = shifted(vr, 1) + dr, shifted(vi, 1) + di
        dr, di = _cmul(par_ref[n_seg:n_seg + 1, cols], pai_ref[n_seg:n_seg + 1, cols], in_r, in_i)
        car_ref[:, cols] = jnp.broadcast_to(vr[n_seg - 1:n_seg], vr.shape) + dr
        cai_ref[:, cols] = jnp.broadcast_to(vi[n_seg - 1:n_seg], vi.shape) + di
        for s in range(n_steps):
            dr, di = _cmul(pwr_ref[s:s + 1, cols], pwi_ref[s:s + 1, cols], seg_r, seg_i)
            xr_ref[s * n_seg:(s + 1) * n_seg, cols] += dr
            xi_ref[s * n_seg:(s + 1) * n_seg, cols] += di
    sr_ref[...] = car_ref[0:1, :]
    si_ref[...] = cai_ref[0:1, :]
    y_perm = jnp.concatenate(_ssm_readout(xr_ref, xi_ref, cr_ref, ci_ref), axis=1)
    for c in range(n_tiles):
        perm_ref[c] = y_perm[:, c * lanes:(c + 1) * lanes]
    y = jnp.concatenate(
        [jnp.concatenate([perm_ref[c, pl.ds(seg, n_steps, stride=n_seg), :] for seg in range(n_seg)], axis=0)
         for c in range(n_tiles)], axis=1)
    _ssm_output(x, u, y, m_ref, g_ref, d_ref, w1_ref, b1_ref, w2_ref, b2_ref, o_ref)


def _ssm_sample_body(x_ref, m_ref, g_ref, bbr_ref, bbi_ref, pwr_ref, pwi_ref, cr_ref, ci_ref, d_ref,
                     w1_ref, b1_ref, w2_ref, b2_ref, hr_ref, hi_ref, o_ref, sr_ref, si_ref,
                     xr_ref, xi_ref):
    x, u = _ssm_input(x_ref, m_ref, g_ref, bbr_ref, bbi_ref, xr_ref, xi_ref)
    dr, di = _cmul(pwr_ref[0:1, :], pwi_ref[0:1, :], hr_ref[...], hi_ref[...])
    xr_ref[...] = xr_ref[...] + dr
    xi_ref[...] = xi_ref[...] + di
    sr_ref[...] = xr_ref[...]
    si_ref[...] = xi_ref[...]
    y = jnp.concatenate(_ssm_readout(xr_ref, xi_ref, cr_ref, ci_ref), axis=1)
    _ssm_output(x, u, y, m_ref, g_ref, d_ref, w1_ref, b1_ref, w2_ref, b2_ref, o_ref)


def _ssm_tables(lam_re, lam_im, b_re, b_im, c_re, c_im, log_step, n_steps):
    g, p = lam_re.shape
    w = b_re.shape[2]
    pwr, pwi, par, pai, bbr, bbi = _ssm_prep(lam_re, lam_im, log_step, b_re, b_im, n_steps)
    gb = 256 // w
    eye = jnp.eye(gb, dtype=F32)

    def bd_in(a):
        return jnp.einsum("kgip,gh->kgihp", a.reshape(g // gb, gb, w, p), eye).reshape(g // gb, gb * w, gb * p)

    def bd_out(a):
        return jnp.einsum("kgip,gh->kgphi", a.reshape(g // gb, gb, w, p), eye).reshape(g // gb, gb * p, gb * w)

    tables = (bd_in(bbr).astype(BF16), bd_in(bbi).astype(BF16), pwr.reshape(n_steps, g * p),
              pwi.reshape(n_steps, g * p), bd_out(c_re).astype(BF16), bd_out(c_im).astype(BF16))
    return tables, (par.reshape(-1, g * p), pai.reshape(-1, g * p))


def _ssm_prompt(x, mod, g, tables, seg_tables, d_skip, w1, b1, w2, b2, n_seq, tl):
    n, d = x.shape
    t = n // n_seq
    n_state = tables[2].shape[1]
    assert tl == SSM_SEGMENTS * tables[2].shape[0]
    consts = tables + (d_skip[None], w1, b1[None], w2, b2[None]) + seg_tables
    row_spec = pl.BlockSpec((tl, d), lambda b, c: (b * (t // tl) + c, 0))
    st_spec = pl.BlockSpec((None, 1, n_state), lambda b, c: (b, 0, 0))
    return pl.pallas_call(
        functools.partial(_ssm_prompt_body, lane_block=1024),
        grid=(n_seq, t // tl),
        in_specs=[row_spec, pl.BlockSpec((6, None, 1, d), lambda b, c: (0, b, 0, 0)), _const_spec(g, 2)]
        + [_const_spec(a, 2) for a in consts],
        out_specs=[row_spec, st_spec, st_spec],
        out_shape=[jax.ShapeDtypeStruct((n, d), F32), jax.ShapeDtypeStruct((n_seq, 1, n_state), F32),
                   jax.ShapeDtypeStruct((n_seq, 1, n_state), F32)],
        scratch_shapes=[pltpu.VMEM((d // V7X_LANES, tl, V7X_LANES), F32),
                        pltpu.VMEM((tl, n_state), F32), pltpu.VMEM((tl, n_state), F32),
                        pltpu.VMEM((SSM_SEGMENTS, n_state), F32), pltpu.VMEM((SSM_SEGMENTS, n_state), F32)],
        compiler_params=_params("arbitrary", "arbitrary"),
        name="ssm_prompt",
    )(x, mod, g, *consts)


def _ssm_sample(x, mod, g, tables, d_skip, w1, b1, w2, b2, h_re, h_im):
    n, d = x.shape
    n_state = tables[2].shape[1]
    consts = tables + (d_skip[None], w1, b1[None], w2, b2[None], h_re, h_im)
    full = pl.BlockSpec((n, d), lambda i: (0, 0))
    st = pl.BlockSpec((n, n_state), lambda i: (0, 0))
    return pl.pallas_call(
        _ssm_sample_body,
        grid=(1,),
        in_specs=[full, _mod_spec(mod, 1), _const_spec(g)] + [_const_spec(a) for a in consts],
        out_specs=[full, st, st],
        out_shape=[jax.ShapeDtypeStruct((n, d), F32), jax.ShapeDtypeStruct((n, n_state), F32),
                   jax.ShapeDtypeStruct((n, n_state), F32)],
        scratch_shapes=[pltpu.VMEM((n, n_state), F32), pltpu.VMEM((n, n_state), F32)],
        compiler_params=_params("arbitrary"),
        name="ssm_sample",
    )(x, mod, g, *consts)


def _nsa_proj_body(x_ref, m_ref, g_ref, w_ref, q_ref, kc_ref, ks_ref, kw_ref, gt_ref, ksb_ref, kwb_ref, *t_refs):
    h = _modulate(x_ref[...], g_ref[0:1], m_ref[0], m_ref[1]).astype(BF16)
    qc, kc = q_ref.shape[1], kc_ref.shape[1]
    q_ref[...] = (_dot(h, w_ref[:, :qc]) * (NSA_HEAD_DIM ** -0.5)).astype(BF16)
    kc_ref[...] = _dot(h, w_ref[:, qc:qc + kc])
    ks = _dot(h, w_ref[:, qc + kc:qc + 2 * kc])
    kw = _dot(h, w_ref[:, qc + 2 * kc:qc + 3 * kc])
    ks_ref[...] = ks
    kw_ref[...] = kw
    ksb_ref[...] = ks.astype(BF16)
    kwb_ref[...] = kw.astype(BF16)
    gt_ref[...] = jax.nn.sigmoid(_dot(h, w_ref[:, qc + 3 * kc:]))
    for t_ref, rows in zip(t_refs, (kc_ref[...], ks, kw)):
        t_ref[...] = rows.T


def _nsa_proj(x, mod, g, w_in, tm, n_seq=None):
    n, d = x.shape
    n_tiles = n // tm
    kc = 2 * NSA_KV_HEADS * NSA_HEAD_DIM
    ng = (w_in.shape[1] - d - 3 * kc)
    widths = [(d, BF16), (kc, F32), (kc, F32), (kc, F32), (ng, F32), (kc, BF16), (kc, BF16)]
    out_specs = [pl.BlockSpec((tm, w), lambda i: (i, 0)) for w, _ in widths]
    out_shape = [jax.ShapeDtypeStruct((n, w), dt) for w, dt in widths]
    if n_seq is not None:
        tps = n_tiles // n_seq
        out_specs += [pl.BlockSpec((None, kc, tm), lambda i: (i // tps, 0, i % tps))] * 3
        out_shape += [jax.ShapeDtypeStruct((n_seq, kc, n // n_seq), F32)] * 3
    return pl.pallas_call(
        _nsa_proj_body,
        grid=(n_tiles,),
        in_specs=[pl.BlockSpec((tm, d), lambda i: (i, 0)), _mod_spec(mod, n_tiles), _const_spec(g),
                  _const_spec(w_in)],
        out_specs=out_specs,
        out_shape=out_shape,
        compiler_params=_params("arbitrary"),
        name="nsa_proj",
    )(x, mod, g, w_in)


def _compress_step(x_of, ls, pe_ref, w1_ref, w2_ref, o_ref, acc_ref, n_l):
    dh = NSA_HEAD_DIM
    hid = w1_ref.shape[3]
    n_zg = 2 * NSA_KV_HEADS

    @pl.when(ls == 0)
    def _():
        acc_ref[...] = jnp.zeros_like(acc_ref)

    for ll in range(n_l):
        l = ls * n_l + ll
        xb = (x_of(ll) + pe_ref[pl.ds(l, 1), :]).astype(BF16)
        for zg in range(n_zg):
            acc_ref[:, zg * hid:(zg + 1) * hid] += _dot(xb[:, zg * dh:(zg + 1) * dh], w1_ref[zg // NSA_KV_HEADS, l])

    @pl.when(ls == pl.num_programs(1) - 1)
    def _():
        a = acc_ref[...]
        hidv = (a * jax.nn.sigmoid(a)).astype(BF16)
        for zg in range(n_zg):
            o_ref[:, zg * dh:(zg + 1) * dh] = _dot(hidv[:, zg * hid:(zg + 1) * hid], w2_ref[zg // NSA_KV_HEADS])


def _compress_prompt_body(x_ref, pe_ref, w1_ref, w2_ref, o_ref, acc_ref, *, n_l):
    _compress_step(lambda ll: x_ref[:, ll, :], pl.program_id(1), pe_ref, w1_ref, w2_ref, o_ref, acc_ref, n_l)


def _compress_sample_body(pt_ref, cache_ref, pe_ref, w1_ref, w2_ref, o_ref, buf_ref, sem, acc_ref, *,
                          pages_per_step, d_tiles):
    pg, ds = pl.program_id(0), pl.program_id(1)
    n_zg = 2 * NSA_KV_HEADS
    dh = NSA_HEAD_DIM
    per_page = o_ref.shape[0]
    hid = w1_ref.shape[3] // per_page

    def page_copy(p):
        page = pt_ref[pg * pages_per_step + p]
        return pltpu.make_async_copy(cache_ref.at[page, :, pl.ds(ds * d_tiles, d_tiles)],
                                     buf_ref.at[:, :, pl.ds(p * 8, 8), :], sem)

    def start(p, c):
        page_copy(p).start()
        return c

    def wait(p, c):
        page_copy(p).wait()
        return c

    lax.fori_loop(0, pages_per_step, start, 0)

    @pl.when(ds == 0)
    def _():
        acc_ref[...] = jnp.zeros_like(acc_ref)

    lax.fori_loop(0, pages_per_step, wait, 0)
    for zg in range(n_zg):
        z = zg // NSA_KV_HEADS
        for dt in range(d_tiles):
            for dd in range(8):
                d = dt * 8 + dd
                x = buf_ref[zg, dt, pl.ds(dd, pages_per_step, stride=8), :] + pe_ref[zg, d:d + 1, :]
                acc_ref[zg] += _dot(x, w1_ref[z, d])

    @pl.when(ds == pl.num_programs(1) - 1)
    def _():
        for zg in range(n_zg):
            a = acc_ref[zg]
            hidv = (a * jax.nn.sigmoid(a)).astype(BF16)
            for n in range(per_page):
                o_ref[n, :, zg * dh:(zg + 1) * dh] = _dot(hidv[:, n * hid:(n + 1) * hid], w2_ref[zg // NSA_KV_HEADS])


def _pe_rows(pe):
    blk = pe.shape[1]
    return jnp.broadcast_to(pe.transpose(1, 0, 2)[:, :, None, :], (blk, 2, NSA_KV_HEADS, pe.shape[2])).reshape(blk, -1)


_COMPRESS_ROWS = 16


def _compress_prompt(kc, pe, w1, w2, nbt):
    n, c = kc.shape
    nblk = n // NSA_BLOCK
    n_l = _COMPRESS_ROWS
    x3 = kc.reshape(nblk, NSA_BLOCK, c)
    return pl.pallas_call(
        functools.partial(_compress_prompt_body, n_l=n_l),
        grid=(nblk // nbt, NSA_BLOCK // n_l),
        in_specs=[pl.BlockSpec((nbt, n_l, c), lambda i, l: (i, l, 0)), _const_spec(pe, 2), _const_spec(w1, 2),
                  _const_spec(w2, 2)],
        out_specs=pl.BlockSpec((nbt, c), lambda i, l: (i, 0)),
        out_shape=jax.ShapeDtypeStruct((nblk, c), F32),
        scratch_shapes=[pltpu.VMEM((nbt, 2 * NSA_KV_HEADS * w1.shape[3]), F32)],
        compiler_params=_params("arbitrary", "arbitrary"),
        name="compress_prompt",
    )(x3, pe, w1, w2)


def _rows_on_lanes(cache):
    return jnp.transpose(cache, (0, 2, 3, 4, 1))


def _compress_sample(page_table, cache, pe, w1, w2, pages_per_step):
    n_pool, page = cache.shape[:2]
    dh = NSA_HEAD_DIM
    n_zg = 2 * NSA_KV_HEADS
    c = n_zg * dh
    per_page = page // NSA_BLOCK
    hid = w1.shape[3]
    d_tiles = 2
    n_pages_total = page_table.size
    cache_t = _rows_on_lanes(cache).reshape(n_pool, n_zg, dh // 8, 8, page)
    pe_t = jnp.tile(jnp.repeat(jnp.swapaxes(pe, 1, 2), NSA_KV_HEADS, axis=0), (1, 1, per_page))
    w1_t = jnp.einsum("zlde,nm->zdnlme", w1, jnp.eye(per_page, dtype=w1.dtype)).reshape(2, dh, page, per_page * hid)
    grid_spec = pltpu.PrefetchScalarGridSpec(
        num_scalar_prefetch=1,
        grid=(n_pages_total // pages_per_step, dh // (8 * d_tiles)),
        in_specs=[pl.BlockSpec(memory_space=pl.ANY),
                  pl.BlockSpec((n_zg, 8 * d_tiles, page), lambda i, s, pt: (0, s, 0)),
                  pl.BlockSpec((2, 8 * d_tiles, page, per_page * hid), lambda i, s, pt: (0, s, 0, 0)),
                  pl.BlockSpec(w2.shape, lambda i, s, pt: (0, 0, 0))],
        out_specs=pl.BlockSpec((per_page, pages_per_step, c), lambda i, s, pt: (0, i, 0)),
        scratch_shapes=[pltpu.VMEM((n_zg, d_tiles, pages_per_step * 8, page), F32), pltpu.SemaphoreType.DMA(()),
                        pltpu.VMEM((n_zg, pages_per_step, per_page * hid), F32)],
    )
    out = pl.pallas_call(
        functools.partial(_compress_sample_body, pages_per_step=pages_per_step, d_tiles=d_tiles),
        grid_spec=grid_spec,
        out_shape=jax.ShapeDtypeStruct((per_page, n_pages_total, c), F32),
        compiler_params=_params("arbitrary", "arbitrary"),
        name="compress_sample",
    )(page_table.reshape(-1), cache_t, pe_t, w1_t, w2)
    return jnp.swapaxes(out, 0, 1).reshape(n_pages_total * per_page, c)


def _stack_heads(q, grp):
    dh = NSA_HEAD_DIM
    rep = q.shape[1] // (NSA_KV_HEADS * dh)
    base = grp * rep * dh
    return jnp.concatenate([q[:, base + r * dh:base + (r + 1) * dh] for r in range(rep)], axis=0)


def _cmp_branch(qs, cmpv, grp, t_row, rep):
    dh = NSA_HEAD_DIM
    kv = NSA_KV_HEADS * dh
    kc = cmpv[:, grp * dh:(grp + 1) * dh]
    vc = cmpv[:, kv + grp * dh:kv + (grp + 1) * dh]
    s = _dot_nt(qs, kc)
    n = lax.broadcasted_iota(jnp.int32, s.shape, 1)
    mask = (n + 1) * NSA_BLOCK <= t_row + 1
    s = jnp.where(mask, s, NEG_BIG)
    e = jnp.where(mask, jnp.exp(s - jnp.max(s, axis=-1, keepdims=True)), 0.0)
    p = e / jnp.maximum(jnp.sum(e, axis=-1, keepdims=True), 1e-30)
    o = _dot(p, vc)
    t = p.shape[0] // rep
    imp = p[0:t]
    for r in range(1, rep):
        imp = imp + p[r * t:(r + 1) * t]
    return o, imp


def _topk_mask(score, axis):
    idx = lax.broadcasted_iota(jnp.int32, score.shape, axis).astype(F32)
    n = float(score.shape[axis])
    sel = jnp.zeros(score.shape, F32)
    x = score
    for _ in range(NSA_TOPK):
        m = jnp.max(x, axis=axis, keepdims=True)
        first = jnp.min(jnp.where(x == m, idx, n), axis=axis, keepdims=True)
        pick = idx == first
        sel = jnp.where(pick, 1.0, sel)
        x = jnp.where(pick, -jnp.inf, x)
    return sel


def _cmpattn_prompt_body(q_ref, cmp_ref, o_ref, sel_ref):
    tq = q_ref.shape[0]
    nb = cmp_ref.shape[0]
    dh = NSA_HEAD_DIM
    rep = q_ref.shape[1] // (NSA_KV_HEADS * dh)
    t0 = pl.program_id(1) * tq
    q = q_ref[...]
    cmpv = cmp_ref[...]
    t_row = t0 + lax.broadcasted_iota(jnp.int32, (rep * tq, 1), 0) % tq
    blk = lax.broadcasted_iota(jnp.int32, (nb, tq), 0)
    jt = (t0 + lax.broadcasted_iota(jnp.int32, (nb, tq), 1)) // NSA_BLOCK
    forced = (blk == 0) | (blk == jt) | (blk == jt - 1)
    for grp in range(NSA_KV_HEADS):
        o, imp = _cmp_branch(_stack_heads(q, grp), cmpv, grp, t_row, rep)
        for r in range(rep):
            h = grp * rep + r
            o_ref[:, h * dh:(h + 1) * dh] = o[r * tq:(r + 1) * tq]
        score = jnp.where(blk <= jt, jnp.where(forced, SEL_FORCE, imp.T), SEL_MASKED)
        sel = _topk_mask(score, 0) * (score > 0.5 * SEL_MASKED).astype(F32)
        sel_ref[:, grp * nb:(grp + 1) * nb] = jnp.where(sel.T > 0.5, 0.0, NEG_BIG).astype(BF16)


def _cmpattn_prompt(q, cmp, n_seq, tq):
    n, qc = q.shape
    t = n // n_seq
    nb = cmp.shape[0] // n_seq
    return pl.pallas_call(
        _cmpattn_prompt_body,
        grid=(n_seq, t // tq),
        in_specs=[pl.BlockSpec((tq, qc), lambda b, i: (b * (t // tq) + i, 0)),
                  pl.BlockSpec((nb, cmp.shape[1]), lambda b, i: (b, 0))],
        out_specs=[pl.BlockSpec((tq, qc), lambda b, i: (b * (t // tq) + i, 0)),
                   pl.BlockSpec((tq, NSA_KV_HEADS * nb), lambda b, i: (b * (t // tq) + i, 0))],
        out_shape=[jax.ShapeDtypeStruct((n, qc), F32), jax.ShapeDtypeStruct((n, NSA_KV_HEADS * nb), BF16)],
        compiler_params=_params("arbitrary", "arbitrary"),
        name="cmpattn_prompt",
    )(q, cmp)


def _cmpattn_sample_body(q_ref, cmp_ref, o_ref, idx_ref, *, t_pos, n_cand):
    nb = cmp_ref.shape[0]
    dh = NSA_HEAD_DIM
    rep = q_ref.shape[1] // (NSA_KV_HEADS * dh)
    q = q_ref[...]
    cmpv = cmp_ref[...]
    t_row = jnp.full((rep, 1), t_pos, jnp.int32)
    width = idx_ref.shape[1]
    lanes = ((n_cand + 127) // 128) * 128
    blk = lax.broadcasted_iota(jnp.int32, (1, lanes), 1)
    jt = t_pos // NSA_BLOCK
    forced = (blk == 0) | (blk == jt) | (blk == jt - 1)
    col = lax.broadcasted_iota(jnp.int32, (1, width), 1)
    blk_f = blk.astype(F32)
    for grp in range(NSA_KV_HEADS):
        o, imp = _cmp_branch(_stack_heads(q, grp), cmpv, grp, t_row, rep)
        for r in range(rep):
            h = grp * rep + r
            o_ref[:, h * dh:(h + 1) * dh] = o[r:r + 1]
        imp = jnp.concatenate([imp, jnp.zeros((1, lanes - nb), F32)], axis=1)
        score = jnp.where(blk <= jt, jnp.where(forced, SEL_FORCE, imp), SEL_MASKED)
        x = jnp.where(blk < n_cand, score, -jnp.inf)
        row = jnp.full((1, width), -1, jnp.int32)
        for k in range(NSA_TOPK):
            m = jnp.max(x, axis=1, keepdims=True)
            first = jnp.min(jnp.where(x == m, blk_f, float(lanes)), axis=1, keepdims=True)
            chosen = jnp.where(m > 0.5 * SEL_MASKED, first, -1.0).astype(jnp.int32)
            row = jnp.where(col == k, chosen, row)
            x = jnp.where(blk_f == first, -jnp.inf, x)
        idx_ref[grp:grp + 1, :] = row


def _cmpattn_sample(q, cmp, t_pos, n_cand):
    n_seq, qc = q.shape
    nb = cmp.shape[0] // n_seq
    return pl.pallas_call(
        functools.partial(_cmpattn_sample_body, t_pos=t_pos, n_cand=n_cand),
        grid=(n_seq,),
        in_specs=[pl.BlockSpec((None, 1, qc), lambda b: (b, 0, 0)), pl.BlockSpec((nb, cmp.shape[1]), lambda b: (b, 0))],
        out_specs=[pl.BlockSpec((None, 1, qc), lambda b: (b, 0, 0)),
                   pl.BlockSpec((None, NSA_KV_HEADS, 128), lambda b: (b, 0, 0))],
        out_shape=[jax.ShapeDtypeStruct((n_seq, 1, qc), F32), jax.ShapeDtypeStruct((n_seq, NSA_KV_HEADS, 128), jnp.int32)],
        compiler_params=_params("arbitrary"),
        name="cmpattn_sample",
    )(q[:, None, :], cmp)


def _attn_prompt_body(q_ref, sel_ref, ks_ref, kw_ref, oslc_ref, owin_ref, qa_ref, m_ref, acc_ref, *, tk):
    tq = q_ref.shape[0]
    dh = NSA_HEAD_DIM
    slab = 2 * dh
    kv = NSA_KV_HEADS * dh
    rep = q_ref.shape[1] // kv
    rows = rep * tq
    nb = sel_ref.shape[1] // NSA_KV_HEADS
    t0 = pl.program_id(1) * tq
    t_col = t0 + lax.broadcasted_iota(jnp.int32, (tq, 1), 0)
    lane = lax.broadcasted_iota(jnp.int32, (tq, slab), 1)

    for grp in range(NSA_KV_HEADS):
        off = (grp % 2) * dh
        parts = []
        for r in range(rep):
            h = grp * rep + r
            x = q_ref[:, (h // 2) * slab:(h // 2 + 1) * slab].astype(F32)
            if h % 2 != grp % 2:
                x = pltpu.roll(x, dh, axis=1)
            parts.append(jnp.where((lane >= off) & (lane < off + dh), x, 0.0))
        qa_ref[grp, :, 0:slab] = jnp.concatenate(parts, axis=0).astype(BF16)
        qa_ref[grp, :, slab:slab + nb] = jnp.concatenate([sel_ref[:, grp * nb:(grp + 1) * nb]] * rep, axis=0)
    m_ref[...] = jnp.full(m_ref.shape, 0.1 * NEG_BIG, F32)
    acc_ref[...] = jnp.zeros_like(acc_ref)
    lane_k = lax.broadcasted_iota(jnp.int32, (tk, slab), 1)
    own_half = [lane_k < dh, lane_k >= dh]

    def kv_tile(j, causal):
        k0 = pl.multiple_of(j * tk, tk)
        key_blk = (k0 + lax.broadcasted_iota(jnp.int32, (tk, nb), 0)) // NSA_BLOCK
        onehot = (key_blk == lax.broadcasted_iota(jnp.int32, (tk, nb), 1)).astype(BF16)
        if causal:
            late = k0 + lax.broadcasted_iota(jnp.int32, (1, tk), 1) > t_col
            causal_bias = jnp.where(late, NEG_BIG, 0.0)
        for grp in range(NSA_KV_HEADS):
            pair = (grp // 2) * slab
            k_aug = jnp.concatenate([ks_ref[pl.ds(k0, tk), pair:pair + slab], onehot], axis=1)
            s = _dot_nt(qa_ref[grp], k_aug)
            if causal:
                s = (s.reshape(rep, tq, tk) + causal_bias[None]).reshape(rows, tk)
            cols = [s[:, c * slab:(c + 1) * slab] for c in range(tk // slab)]
            mx = functools.reduce(jnp.maximum, cols)
            m_old = m_ref[grp]
            m_new = jnp.maximum(m_old, jnp.max(mx, axis=-1, keepdims=True))
            alpha = jnp.exp(m_old - m_new)
            p = jnp.concatenate([jnp.exp(c - m_new).astype(BF16) for c in cols], axis=1)
            v = ks_ref[pl.ds(k0, tk), kv + pair:kv + pair + slab]
            v = jnp.where(own_half[grp % 2], v, jnp.ones_like(v))
            acc_ref[grp] = alpha * acc_ref[grp] + _dot(p, v)
            m_ref[grp] = m_new

    n_full = t0 // tk

    def full_tile(j, carry):
        kv_tile(j, False)
        return carry

    lax.fori_loop(0, n_full, full_tile, 0)
    kv_tile(n_full, True)

    win_len = NSA_WINDOW + tq
    w0 = pl.multiple_of(jnp.maximum(t0 - NSA_WINDOW, 0), tq)
    wpos = w0 + lax.broadcasted_iota(jnp.int32, (1, win_len), 1)
    win_bias = jnp.where((wpos <= t_col) & (wpos >= t_col - NSA_WINDOW), 0.0, NEG_BIG)
    for grp in range(NSA_KV_HEADS):
        off = (grp % 2) * dh
        pair = (grp // 2) * slab
        acc = acc_ref[grp]
        o = acc / jnp.maximum(acc[:, dh - off:dh - off + 1], 1e-30)
        for r in range(rep):
            h = grp * rep + r
            oslc_ref[:, h * dh:(h + 1) * dh] = o[r * tq:(r + 1) * tq, off:off + dh]

        s = _dot_nt(qa_ref[grp, :, 0:slab], kw_ref[pl.ds(w0, win_len), pair:pair + slab])
        s = s.reshape(rep, tq, win_len) + win_bias[None]
        m = jnp.maximum(jnp.max(s, axis=-1, keepdims=True), 0.1 * NEG_BIG)
        e = jnp.exp(s - m)
        o = _dot(e.reshape(rows, win_len), kw_ref[pl.ds(w0, win_len), kv + pair:kv + pair + slab])
        o = o / jnp.maximum(jnp.sum(e, axis=-1, keepdims=True).reshape(rows, 1), 1e-30)
        for r in range(rep):
            h = grp * rep + r
            owin_ref[:, h * dh:(h + 1) * dh] = o[r * tq:(r + 1) * tq, off:off + dh]


def _attn_prompt(q, sel, ksb, kwb, n_seq, tq, tk):
    n, qc = q.shape
    t = n // n_seq
    dh = NSA_HEAD_DIM
    rep = qc // (NSA_KV_HEADS * dh)
    assert t % tk == 0 and tk % tq == 0 and t >= NSA_WINDOW + tq and NSA_WINDOW % tq == 0
    tile = pl.BlockSpec((tq, qc), lambda b, i: (b * (t // tq) + i, 0))
    seq = pl.BlockSpec((t, ksb.shape[1]), lambda b, i: (b, 0))
    return pl.pallas_call(
        functools.partial(_attn_prompt_body, tk=tk),
        grid=(n_seq, t // tq),
        in_specs=[tile, pl.BlockSpec((tq, sel.shape[1]), lambda b, i: (b * (t // tq) + i, 0)), seq, seq],
        out_specs=[tile, tile],
        out_shape=[jax.ShapeDtypeStruct((n, qc), F32), jax.ShapeDtypeStruct((n, qc), F32)],
        scratch_shapes=[pltpu.VMEM((NSA_KV_HEADS, rep * tq, 2 * dh + sel.shape[1] // NSA_KV_HEADS), BF16)]
        + [pltpu.VMEM((NSA_KV_HEADS, rep * tq, 2 * dh), F32)] * 2,
        compiler_params=_params("arbitrary", "arbitrary"),
        name="attn_prompt",
    )(q, sel, ksb, kwb)


def _softmax_with_new_key(s, ok, s_new, new_ok):
    s = jnp.where(ok, s, NEG_BIG)
    s_new = jnp.where(new_ok, s_new, NEG_BIG)
    m = jnp.maximum(jnp.max(s, axis=-1, keepdims=True), s_new)
    e = jnp.where(ok, jnp.exp(s - m), 0.0)
    e_new = jnp.where(new_ok, jnp.exp(s_new - m), 0.0)
    return e, e_new, jnp.maximum(jnp.sum(e, axis=-1, keepdims=True) + e_new, 1e-30)


def _bf16_round(x):
    return x.astype(BF16).astype(F32)


def _attn_sample_body(pt_ref, idx_ref, q_ref, ksn_ref, kwn_ref, win_ref, cache_ref, oslc_ref, owin_ref,
                      kbuf_ref, sem, *, t_pos, nb_past, n_pages):
    b = pl.program_id(0)
    dh = NSA_HEAD_DIM
    kv = NSA_KV_HEADS * dh
    rep = q_ref.shape[1] // kv
    n_sel = NSA_TOPK
    page = cache_ref.shape[4]
    per_page = page // NSA_BLOCK
    q = q_ref[...]

    def sel_index(grp, k):
        return idx_ref[(b * NSA_KV_HEADS + grp) * 128 + k]

    def in_pool(idx):
        return (idx >= 0) & (idx < nb_past)

    def page_copy(grp, k, idx):
        phys = pt_ref[b * n_pages + jnp.minimum(idx // per_page, n_pages - 1)]
        return pltpu.make_async_copy(cache_ref.at[phys, :, grp], kbuf_ref.at[grp, :, :, pl.ds(k * page, page)], sem)

    for grp in range(NSA_KV_HEADS):
        for k in range(n_sel):
            idx = sel_index(grp, k)

            @pl.when(in_pool(idx))
            def _():
                page_copy(grp, k, idx).start()

            @pl.when(jnp.logical_not(in_pool(idx)))
            def _():
                kbuf_ref[grp, :, :, k * page:(k + 1) * page] = jnp.zeros((2, dh, page), F32)

    for grp in range(NSA_KV_HEADS):
        for k in range(n_sel):
            idx = sel_index(grp, k)

            @pl.when(in_pool(idx))
            def _():
                page_copy(grp, k, idx).wait()

    lane = lax.broadcasted_iota(jnp.int32, (1, n_sel * page), 1)
    wb = win_ref.shape[3]
    wpos = t_pos - wb + lax.broadcasted_iota(jnp.int32, (1, wb), 1)
    win_ok = (wpos <= t_pos) & (wpos >= t_pos - NSA_WINDOW) & (wpos >= 0)
    for grp in range(NSA_KV_HEADS):
        qs = _stack_heads(q, grp)
        qf = qs.astype(F32)
        ok = jnp.zeros((1, n_sel * page), jnp.bool_)
        has_new = False
        for k in range(n_sel):
            idx = sel_index(grp, k)
            row = lane - k * page
            kpos = (idx // per_page) * page + row
            ok = ok | ((lane // page == k) & in_pool(idx) & (row // NSA_BLOCK == idx % per_page) & (kpos <= t_pos))
            has_new = has_new | (idx >= nb_past)
        new_ok = has_new & (nb_past * NSA_BLOCK <= t_pos)
        k_new = _bf16_round(ksn_ref[:, grp * dh:(grp + 1) * dh])
        v_new = _bf16_round(ksn_ref[:, kv + grp * dh:kv + (grp + 1) * dh])
        s_new = jnp.sum(qf * k_new, axis=-1, keepdims=True)
        e, e_new, den = _softmax_with_new_key(_dot(qs, kbuf_ref[grp, 0]), ok, s_new, new_ok)
        o = (_dot_nt(e, kbuf_ref[grp, 1]) + _bf16_round(e_new) * v_new) / den
        for r in range(rep):
            h = grp * rep + r
            oslc_ref[:, h * dh:(h + 1) * dh] = o[r:r + 1]

        k_new = _bf16_round(kwn_ref[:, grp * dh:(grp + 1) * dh])
        v_new = _bf16_round(kwn_ref[:, kv + grp * dh:kv + (grp + 1) * dh])
        s_new = jnp.sum(qf * k_new, axis=-1, keepdims=True)
        e, e_new, den = _softmax_with_new_key(_dot(qs, win_ref[0, grp]), win_ok, s_new, True)
        o = (_dot_nt(e, win_ref[1, grp]) + _bf16_round(e_new) * v_new) / den
        for r in range(rep):
            h = grp * rep + r
            owin_ref[:, h * dh:(h + 1) * dh] = o[r:r + 1]


def _attn_sample(page_table, sel_idx, q, ks_new, kw_new, win, cache, t_pos):
    n_seq, qc = q.shape
    n_pool, page = cache.shape[:2]
    dh = NSA_HEAD_DIM
    c = 2 * NSA_KV_HEADS * dh
    n_pages = page_table.shape[1]
    wb = win.shape[1]
    row3 = lambda w: pl.BlockSpec((None, 1, w), lambda b, pt, ix: (b, 0, 0))
    grid_spec = pltpu.PrefetchScalarGridSpec(
        num_scalar_prefetch=2,
        grid=(n_seq,),
        in_specs=[row3(qc), row3(c), row3(c),
                  pl.BlockSpec((None, 2, NSA_KV_HEADS, dh, wb), lambda b, pt, ix: (b, 0, 0, 0, 0)),
                  pl.BlockSpec(memory_space=pl.ANY)],
        out_specs=[row3(qc), row3(qc)],
        scratch_shapes=[pltpu.VMEM((NSA_KV_HEADS, 2, dh, NSA_TOPK * page), F32), pltpu.SemaphoreType.DMA(())],
    )
    win, cache = _rows_on_lanes(win), _rows_on_lanes(cache)
    return pl.pallas_call(
        functools.partial(_attn_sample_body, t_pos=t_pos, nb_past=t_pos // NSA_BLOCK, n_pages=n_pages),
        grid_spec=grid_spec,
        out_shape=[jax.ShapeDtypeStruct((n_seq, 1, qc), F32), jax.ShapeDtypeStruct((n_seq, 1, qc), F32)],
        compiler_params=_params("arbitrary"),
        name="attn_sample",
    )(page_table.reshape(-1), sel_idx.reshape(-1), q[:, None, :], ks_new[:, None, :], kw_new[:, None, :], win, cache)


def _nsa_merge_body(x_ref, m_ref, g_ref, oc_ref, os_ref, ow_ref, gt_ref, wout_ref, o_ref, om_ref):
    dh = NSA_HEAD_DIM
    n_heads = oc_ref.shape[1] // dh
    gt = gt_ref[...]
    for h in range(n_heads):
        c = slice(h * dh, (h + 1) * dh)
        o = (gt[:, h:h + 1] * oc_ref[:, c] + gt[:, n_heads + h:n_heads + h + 1] * os_ref[:, c]
             + gt[:, 2 * n_heads + h:2 * n_heads + h + 1] * ow_ref[:, c])
        om_ref[:, c] = o.astype(BF16)
    x = x_ref[...]
    o_ref[...] = x + m_ref[2] * _rms(_dot(om_ref[...], wout_ref[...]), g_ref[1:2])


def _nsa_merge(x, mod, g, o_cmp, o_slc, o_win, gates, w_out, tm):
    n, d = x.shape
    n_tiles = n // tm
    qc = o_cmp.shape[1]
    tile = lambda w: pl.BlockSpec((tm, w), lambda i: (i, 0))
    return pl.pallas_call(
        _nsa_merge_body,
        grid=(n_tiles,),
        in_specs=[tile(d), _mod_spec(mod, n_tiles), _const_spec(g), tile(qc), tile(qc), tile(qc),
                  tile(gates.shape[1]), _const_spec(w_out)],
        out_specs=tile(d),
        out_shape=jax.ShapeDtypeStruct((n, d), F32),
        scratch_shapes=[pltpu.VMEM((tm, qc), BF16)],
        compiler_params=_params("arbitrary"),
        name="nsa_merge",
    )(x, mod, g, o_cmp, o_slc, o_win, gates, w_out)


def _nsa_layer(xp, xs, mod_p, mod_s, g, n_seq, cache_cmp, cache_slc, cache_win, page_table,
               w_in, w1, w2, pe, w_out, tm):
    n_s = xs.shape[0]
    t = xp.shape[0] // n_seq
    page_size = cache_cmp.shape[1]
    past = page_table.shape[1] * page_size
    assert t % NSA_BLOCK == 0 and past % NSA_BLOCK == 0 and page_size % NSA_BLOCK == 0
    w_in_b, w_out_b = w_in.astype(BF16), w_out.astype(BF16)
    w1_b, w2_b = w1.astype(BF16), w2.astype(BF16)
    pe_rows = _pe_rows(pe)

    q, kc, _, _, gates, ksb, kwb, kc_t, ks_t, kw_t = _nsa_proj(xp, mod_p, g, w_in_b, tm, n_seq)
    cmp_p = _compress_prompt(kc, pe_rows, w1_b, w2_b, min(256, kc.shape[0] // NSA_BLOCK))
    o_cmp, sel = _cmpattn_prompt(q, cmp_p, n_seq, 128)
    o_slc, o_win = _attn_prompt(q, sel, ksb, kwb, n_seq, 128, 512)
    xp = _nsa_merge(xp, mod_p, g, o_cmp, o_slc, o_win, gates, w_out_b, tm)

    q_s, kc_s, ks_s, kw_s, gates_s, _, _ = _nsa_proj(xs, mod_s, g, w_in_b, n_s)
    cmp_s = _compress_sample(page_table, cache_cmp, pe.astype(F32), w1_b, w2_b, min(256, page_table.size))
    n_cand = -(-(past + 1) // NSA_BLOCK)
    o_cmp_s, sel_idx = _cmpattn_sample(q_s, cmp_s, past, n_cand)
    o_slc_s, o_win_s = _attn_sample(page_table, sel_idx, q_s, ks_s, kw_s, cache_win, cache_slc, past)
    xs = _nsa_merge(xs, mod_s, g, o_cmp_s.reshape(n_s, -1), o_slc_s.reshape(n_s, -1), o_win_s.reshape(n_s, -1),
                    gates_s, w_out_b, n_s)
    return xp, xs, (kc_t, ks_t, kw_t), (kc_s, ks_s, kw_s)


def kernel(x_prompt, x_sample, cache_nsa_cmp, cache_nsa_slc, cache_nsa_win, state_ssm, page_table, c_prompt, c_sample, w_mod, b_mod, norm_g, ffn_w_gate, ffn_w_up, ffn_w_down, gmlp_w_in, gmlp_b_in, gmlp_ln_g, gmlp_ln_b, gmlp_w_s, gmlp_b_s, gmlp_w_out, nsa_w_in, nsa_w_cmp1, nsa_w_cmp2, nsa_pe_cmp, nsa_w_out, ssm_lambda_re, ssm_lambda_im, ssm_b_re, ssm_b_im, ssm_c_re, ssm_c_im, ssm_d, ssm_log_step, ssm_w_glu1, ssm_b_glu1, ssm_w_glu2, ssm_b_glu2):
    n_seq, t, d = x_prompt.shape
    n_s, t_s, _ = x_sample.shape
    assert t_s == 1
    depth = w_mod.shape[0]
    tm = 512 if t % 512 == 0 else 256
    kv_shape = (2, NSA_KV_HEADS, NSA_HEAD_DIM)

    xp = x_prompt.reshape(n_seq * t, d)
    xs = x_sample.reshape(n_s, d)
    m_all = _adaln(jnp.concatenate([c_prompt, c_sample], axis=0), w_mod, b_mod)
    mods_p = m_all[:, :n_seq].reshape(depth, n_seq, 6, 1, d).transpose(0, 2, 1, 3, 4)
    mods_s = m_all[:, n_seq:].reshape(depth, n_s, 6, d).transpose(0, 2, 1, 3)[:, :, None]

    cmp_p, cmp_s, slc_p, slc_s, win_p, win_s, ssm_p, ssm_s, gv_s = [], [], [], [], [], [], [], [], []
    for i in range(depth):
        j = i // N_MIXERS
        mp, ms, g = mods_p[i], mods_s[i], norm_g[i]
        if i % N_MIXERS == 0:
            gw = (gmlp_w_in[j].astype(BF16), gmlp_b_in[j], gmlp_ln_g[j], gmlp_ln_b[j], gmlp_w_s[j], gmlp_b_s[j],
                  gmlp_w_out[j].astype(BF16))
            assert t % GMLP_CHUNK == 0 and gmlp_w_s.shape[2] == GMLP_CHUNK
            xp = _gmlp_prompt(xp, mp, g, *gw, tm)
            xs, v_new = _gmlp_sample(xs, ms, g, *gw)
            gv_s.append(v_new.reshape(n_s, 1, -1))
        elif i % N_MIXERS == 1:
            xp, xs, kv_p, kv_s = _nsa_layer(xp, xs, mp, ms, g, n_seq, cache_nsa_cmp[j], cache_nsa_slc[j],
                                            cache_nsa_win[j], page_table, nsa_w_in[j], nsa_w_cmp1[j],
                                            nsa_w_cmp2[j], nsa_pe_cmp[j], nsa_w_out[j], tm)
            rows_last = lambda a: jnp.transpose(a.reshape((n_seq,) + kv_shape + (a.shape[-1],)), (0, 4, 1, 2, 3))
            cmp_p.append(rows_last(kv_p[0]))
            slc_p.append(rows_last(kv_p[1]))
            win_p.append(rows_last(kv_p[2][:, :, t - min(NSA_WINDOW, t):]))
            cmp_s.append(kv_s[0].reshape((n_s, 1) + kv_shape))
            slc_s.append(kv_s[1].reshape((n_s, 1) + kv_shape))
            past = page_table.shape[1] * cache_nsa_cmp.shape[2]
            win = jnp.concatenate([cache_nsa_win[j], kv_s[2].reshape((n_s, 1) + kv_shape)], axis=1)
            win_s.append(win[:, win.shape[1] - min(NSA_WINDOW, past + 1):])
        else:
            assert t % SSM_CHUNK == 0
            tables, seg_tables = _ssm_tables(ssm_lambda_re[j], ssm_lambda_im[j], ssm_b_re[j], ssm_b_im[j],
                                             ssm_c_re[j], ssm_c_im[j], ssm_log_step[j], SSM_CHUNK // SSM_SEGMENTS)
            glu = (ssm_d[j], ssm_w_glu1[j].astype(BF16), ssm_b_glu1[j], ssm_w_glu2[j].astype(BF16), ssm_b_glu2[j])
            n_grp, n_st = ssm_lambda_re.shape[1:]
            xp, sr, si = _ssm_prompt(xp, mp, g, tables, seg_tables, *glu, n_seq, SSM_CHUNK)
            ssm_p.append(jnp.stack([sr.reshape(n_seq, n_grp, n_st), si.reshape(n_seq, n_grp, n_st)], axis=-1))
            h0 = state_ssm[j].reshape(n_s, n_grp * n_st, 2)
            xs, sr, si = _ssm_sample(xs, ms, g, tables, *glu, h0[..., 0], h0[..., 1])
            ssm_s.append(jnp.stack([sr.reshape(n_s, n_grp, n_st), si.reshape(n_s, n_grp, n_st)], axis=-1))
        ffn_w = (ffn_w_gate[i].astype(BF16), ffn_w_up[i].astype(BF16), ffn_w_down[i].astype(BF16))
        xp = _ffn(xp, mp, g, *ffn_w, tm)
        xs = _ffn(xs, ms, g, *ffn_w, n_s)
    return (xp.reshape(n_seq, t, d), xs.reshape(n_s, 1, d), jnp.stack(cmp_p), jnp.stack(cmp_s), jnp.stack(slc_p),
            jnp.stack(slc_s), jnp.stack(win_p), jnp.stack(win_s), jnp.stack(ssm_p), jnp.stack(ssm_s), jnp.stack(gv_s))
```

```python
import functools
import math

import jax
import jax.numpy as jnp
from jax import lax
from jax.experimental import pallas as pl
from jax.experimental.pallas import tpu as pltpu

F32 = jnp.float32
BF16 = jnp.bfloat16

RMS_EPS = 1.0e-6
LN_EPS = 1.0e-5

V7X_LANES = 128
V7X_VMEM_BYTES = 64 * 1024 * 1024
VMEM_LIMIT_BYTES = V7X_VMEM_BYTES - 8 * 1024 * 1024

N_MIXERS = 3
GMLP_GROUPS = 8
GMLP_CHUNK = 128
NSA_HEAD_DIM = 64
NSA_KV_HEADS = 4
NSA_BLOCK = 64
NSA_TOPK = 16
NSA_WINDOW = 512
ATTN_Q_TILE = 256
ATTN_Q_SUB = 128
ATTN_KV_TILE = 512
SEL_FORCE = 1.0e4
SEL_MASKED = -1.0
SSM_GROUP_WIDTH = 16
SSM_STATE = 64
SSM_SEGMENTS = 8
SSM_CHUNK = 256
NEG_BIG = -1.0e30


def _params(*sem):
    return pltpu.CompilerParams(dimension_semantics=sem, vmem_limit_bytes=VMEM_LIMIT_BYTES)


def _dot(a, b):
    return jnp.dot(a.astype(BF16), b.astype(BF16), preferred_element_type=F32)


def _dot_nt(a, b):
    return lax.dot_general(a.astype(BF16), b.astype(BF16), (((1,), (1,)), ((), ())),
                           preferred_element_type=F32)


def _rms(x, g):
    return x * lax.rsqrt(jnp.mean(x * x, axis=-1, keepdims=True) + RMS_EPS) * g


def _modulate(x, g, shift, scale):
    return _rms(x, g) * (1.0 + scale) + shift


def _const_spec(a, n_grid=1, single=False):
    nd = a.ndim
    idx = {1: lambda i: (0,) * nd, 2: lambda i, j: (0,) * nd, 3: lambda i, j, k: (0,) * nd}[n_grid]
    if single:
        return pl.BlockSpec(a.shape, idx, pipeline_mode=pl.Buffered(1))
    return pl.BlockSpec(a.shape, idx)


def _mod_spec(mod, n_tiles):
    _, n_seq, rows, d = mod.shape
    tiles_per_seq = n_tiles // n_seq
    return pl.BlockSpec((6, None, rows, d), lambda i: (0, i // tiles_per_seq, 0, 0))


def _adaln_body(c_ref, w_ref, b_ref, o_ref):
    c = c_ref[...]
    o_ref[...] = _dot(c * jax.nn.sigmoid(c), w_ref[...]) + b_ref[...]


def _adaln(c_all, w_mod, b_mod):
    depth, d, d6 = w_mod.shape
    m = c_all.shape[0]
    tn = 2048
    return pl.pallas_call(
        _adaln_body,
        grid=(depth, d6 // tn),
        in_specs=[pl.BlockSpec((m, d), lambda l, j: (0, 0)),
                  pl.BlockSpec((None, d, tn), lambda l, j: (l, 0, j)),
                  pl.BlockSpec((None, 1, tn), lambda l, j: (l, 0, j))],
        out_specs=pl.BlockSpec((None, m, tn), lambda l, j: (l, 0, j)),
        out_shape=jax.ShapeDtypeStruct((depth, m, d6), F32),
        compiler_params=_params("arbitrary", "arbitrary"),
        name="adaln",
    )(c_all, w_mod, b_mod.reshape(depth, 1, d6))


def _ffn_body(x_ref, m_ref, g_ref, wg_ref, wu_ref, wd_ref, o_ref, *, n_chunks):
    x = x_ref[...]
    h = _modulate(x, g_ref[2:3], m_ref[3], m_ref[4]).astype(BF16)
    fc = wg_ref.shape[1] // n_chunks
    acc = None
    for c in range(n_chunks):
        a = _dot(h, wg_ref[:, c * fc:(c + 1) * fc])
        b = _dot(h, wu_ref[:, c * fc:(c + 1) * fc])
        y = _dot(a * jax.nn.sigmoid(a) * b, wd_ref[c * fc:(c + 1) * fc, :])
        acc = y if acc is None else acc + y
    o_ref[...] = x + m_ref[5] * _rms(acc, g_ref[3:4])


def _ffn(x, mod, g, wg, wu, wd, tm):
    n, d = x.shape
    n_tiles = n // tm
    return pl.pallas_call(
        functools.partial(_ffn_body, n_chunks=2),
        grid=(n_tiles,),
        in_specs=[pl.BlockSpec((tm, d), lambda i: (i, 0)), _mod_spec(mod, n_tiles), _const_spec(g),
                  _const_spec(wg, single=True), _const_spec(wu, single=True), _const_spec(wd, single=True)],
        out_specs=pl.BlockSpec((tm, d), lambda i: (i, 0)),
        out_shape=jax.ShapeDtypeStruct((n, d), F32),
        compiler_params=_params("arbitrary"),
        name="ffn",
    )(x, mod, g, wg, wu, wd)


def _gmlp_front(x_ref, m_ref, g_ref, win_ref, bin_ref, lng_ref, lnb_ref):
    x = x_ref[...]
    h = _modulate(x, g_ref[0:1], m_ref[0], m_ref[1])
    z = jax.nn.gelu(_dot(h, win_ref[...]) + bin_ref[...])
    half = z.shape[1] // 2
    u, v = z[:, :half], z[:, half:]
    mu = jnp.mean(v, axis=-1, keepdims=True)
    var = jnp.mean(jnp.square(v - mu), axis=-1, keepdims=True)
    v = (v - mu) * lax.rsqrt(var + LN_EPS) * lng_ref[...] + lnb_ref[...]
    return x, u, v


def _gmlp_prompt_body(x_ref, m_ref, g_ref, win_ref, bin_ref, lng_ref, lnb_ref, ws_ref, bs_ref, wout_ref,
                      o_ref, um_ref):
    x, u, v = _gmlp_front(x_ref, m_ref, g_ref, win_ref, bin_ref, lng_ref, lnb_ref)
    vb = v.astype(BF16)
    n_groups, chunk, _ = ws_ref.shape
    gw = v.shape[1] // n_groups
    causal = (lax.broadcasted_iota(jnp.int32, (chunk, chunk), 0)
              >= lax.broadcasted_iota(jnp.int32, (chunk, chunk), 1))
    for grp in range(n_groups):
        w = jnp.where(causal, ws_ref[grp], 0.0).astype(BF16)
        cols = slice(grp * gw, (grp + 1) * gw)
        for k in range(x.shape[0] // chunk):
            rows = slice(k * chunk, (k + 1) * chunk)
            mixed = _dot(w, vb[rows, cols]) + bs_ref[:, grp:grp + 1]
            um_ref[rows, cols] = (u[rows, cols] * mixed).astype(BF16)
    y = _dot(um_ref[...], wout_ref[...])
    o_ref[...] = x + m_ref[2] * _rms(y, g_ref[1:2])


def _gmlp_sample_body(x_ref, m_ref, g_ref, win_ref, bin_ref, lng_ref, lnb_ref, ws_ref, bs_ref, wout_ref,
                      o_ref, v_ref):
    x, u, v = _gmlp_front(x_ref, m_ref, g_ref, win_ref, bin_ref, lng_ref, lnb_ref)
    v_ref[...] = v
    y = _dot(u * (ws_ref[...] * v + bs_ref[...]), wout_ref[...])
    o_ref[...] = x + m_ref[2] * _rms(y, g_ref[1:2])


def _gmlp_prompt(x, mod, g, w_in, b_in, ln_g, ln_b, w_s, b_s, w_out, tm):
    n, d = x.shape
    n_tiles = n // tm
    half = w_out.shape[0]
    args = (x, mod, g, w_in, b_in[None], ln_g[None], ln_b[None], w_s, b_s.T, w_out)
    return pl.pallas_call(
        _gmlp_prompt_body,
        grid=(n_tiles,),
        in_specs=[pl.BlockSpec((tm, d), lambda i: (i, 0)), _mod_spec(mod, n_tiles)]
        + [_const_spec(a) for a in args[2:]],
        out_specs=pl.BlockSpec((tm, d), lambda i: (i, 0)),
        out_shape=jax.ShapeDtypeStruct((n, d), F32),
        scratch_shapes=[pltpu.VMEM((tm, half), BF16)],
        compiler_params=_params("arbitrary"),
        name="gmlp_prompt",
    )(*args)


def _gmlp_sample(x, mod, g, w_in, b_in, ln_g, ln_b, w_s, b_s, w_out):
    n, d = x.shape
    half = w_out.shape[0]
    gw = half // w_s.shape[0]
    args = (x, mod, g, w_in, b_in[None], ln_g[None], ln_b[None],
            jnp.repeat(w_s[:, 0, 0], gw)[None], jnp.repeat(b_s[:, 0], gw)[None], w_out)
    return pl.pallas_call(
        _gmlp_sample_body,
        grid=(1,),
        in_specs=[pl.BlockSpec((n, d), lambda i: (0, 0)), _mod_spec(mod, 1)]
        + [_const_spec(a) for a in args[2:]],
        out_specs=[pl.BlockSpec((n, d), lambda i: (0, 0)), pl.BlockSpec((n, half), lambda i: (0, 0))],
        out_shape=[jax.ShapeDtypeStruct((n, d), F32), jax.ShapeDtypeStruct((n, half), F32)],
        compiler_params=_params("arbitrary"),
        name="gmlp_sample",
    )(*args)


def _cmul(ar, ai, br, bi):
    return ar * br - ai * bi, ar * bi + ai * br


def _powers(base, n):
    p = [base]
    for k in range(2, n + 1):
        p.append(_cmul(*p[k // 2 - 1], *p[k - k // 2 - 1]))
    return p


def _ssm_prep_body(lr_ref, li_ref, ls_ref, br_ref, bi_ref, pwr_ref, pwi_ref, par_ref, pai_ref, bbr_ref, bbi_ref):
    lr, li = lr_ref[...], li_ref[...]
    dt = jnp.exp(ls_ref[...])
    mag = jnp.exp(lr * dt)
    ab_re, ab_im = mag * jnp.cos(li * dt), mag * jnp.sin(li * dt)
    den = lr * lr + li * li
    f_re = ((ab_re - 1.0) * lr + ab_im * li) / den
    f_im = (ab_im * lr - (ab_re - 1.0) * li) / den
    bbr_ref[...] = f_re[:, None, :] * br_ref[...] - f_im[:, None, :] * bi_ref[...]
    bbi_ref[...] = f_re[:, None, :] * bi_ref[...] + f_im[:, None, :] * br_ref[...]
    n_steps = pwr_ref.shape[0]
    p = _powers((ab_re, ab_im), n_steps)
    for n in range(n_steps):
        pwr_ref[n] = p[n][0]
        pwi_ref[n] = p[n][1]
    a = _powers(p[n_steps - 1], SSM_SEGMENTS)
    par_ref[0] = jnp.ones_like(ab_re)
    pai_ref[0] = jnp.zeros_like(ab_re)
    for k in range(SSM_SEGMENTS):
        par_ref[k + 1] = a[k][0]
        pai_ref[k + 1] = a[k][1]


def _ssm_prep(lam_re, lam_im, log_step, b_re, b_im, n_steps):
    g, p = lam_re.shape
    w = b_re.shape[2]
    args = (lam_re, lam_im, log_step[:, None], jnp.swapaxes(b_re, 1, 2), jnp.swapaxes(b_im, 1, 2))
    return pl.pallas_call(
        _ssm_prep_body,
        out_shape=[jax.ShapeDtypeStruct((n_steps, g, p), F32)] * 2
        + [jax.ShapeDtypeStruct((SSM_SEGMENTS + 1, g, p), F32)] * 2
        + [jax.ShapeDtypeStruct((g, w, p), F32)] * 2,
        name="ssm_prep",
    )(*args)


def _ssm_input(x_ref, m_ref, g_ref, bbr_ref, bbi_ref, xr_ref, xi_ref):
    x = x_ref[...]
    u = _modulate(x, g_ref[0:1], m_ref[0], m_ref[1])
    ub = u.astype(BF16)
    n_kb, kin, kout = bbr_ref.shape
    for kb in range(n_kb):
        xr_ref[:, kb * kout:(kb + 1) * kout] = _dot(ub[:, kb * kin:(kb + 1) * kin], bbr_ref[kb])
        xi_ref[:, kb * kout:(kb + 1) * kout] = _dot(ub[:, kb * kin:(kb + 1) * kin], bbi_ref[kb])
    return x, u


def _ssm_readout(xr_ref, xi_ref, cr_ref, ci_ref):
    n_kb, kin, _ = cr_ref.shape
    return [_dot(xr_ref[:, kb * kin:(kb + 1) * kin], cr_ref[kb]) - _dot(xi_ref[:, kb * kin:(kb + 1) * kin], ci_ref[kb])
            for kb in range(n_kb)]


def _ssm_output(x, u, y, m_ref, g_ref, d_ref, w1_ref, b1_ref, w2_ref, b2_ref, o_ref):
    gl = jax.nn.gelu(y + d_ref[...] * u)
    out = (_dot(gl, w1_ref[...]) + b1_ref[...]) * jax.nn.sigmoid(_dot(gl, w2_ref[...]) + b2_ref[...])
    o_ref[...] = x + m_ref[2] * _rms(out, g_ref[1:2])


def _ssm_prompt_body(x_ref, m_ref, g_ref, bbr_ref, bbi_ref, pwr_ref, pwi_ref, cr_ref, ci_ref, d_ref,
                     w1_ref, b1_ref, w2_ref, b2_ref, par_ref, pai_ref, o_ref, sr_ref, si_ref,
                     perm_ref, xr_ref, xi_ref, car_ref, cai_ref, *, lane_block):
    n_seg = SSM_SEGMENTS

    @pl.when(pl.program_id(1) == 0)
    def _():
        car_ref[...] = jnp.zeros_like(car_ref)
        cai_ref[...] = jnp.zeros_like(cai_ref)

    rows, n_state = xr_ref.shape
    n_steps = rows // n_seg
    n_tiles, _, lanes = perm_ref.shape
    x = x_ref[...]
    u = _modulate(x, g_ref[0:1], m_ref[0], m_ref[1])
    for c in range(n_tiles):
        perm_ref[c] = u[:, c * lanes:(c + 1) * lanes]
    ub = jnp.concatenate(
        [jnp.concatenate([perm_ref[c, pl.ds(s, n_seg, stride=n_steps), :] for s in range(n_steps)], axis=0)
         for c in range(n_tiles)], axis=1).astype(BF16)
    n_kb, kin, kout = bbr_ref.shape
    for kb in range(n_kb):
        xr_ref[:, kb * kout:(kb + 1) * kout] = _dot(ub[:, kb * kin:(kb + 1) * kin], bbr_ref[kb])
        xi_ref[:, kb * kout:(kb + 1) * kout] = _dot(ub[:, kb * kin:(kb + 1) * kin], bbi_ref[kb])

    row = lax.broadcasted_iota(jnp.int32, (n_seg, lane_block), 0)

    def shifted(v, s):
        return jnp.where(row >= s, pltpu.roll(v, s, axis=0), 0.0)

    for cb in range(n_state // lane_block):
        cols = slice(cb * lane_block, (cb + 1) * lane_block)
        ar, ai = pwr_ref[0:1, cols], pwi_ref[0:1, cols]
        vr = vi = jnp.zeros((n_seg, lane_block), F32)
        for s in range(n_steps):
            dr, di = _cmul(ar, ai, vr, vi)
            vr = xr_ref[s * n_seg:(s + 1) * n_seg, cols] + dr
            vi = xi_ref[s * n_seg:(s + 1) * n_seg, cols] + di
            xr_ref[s * n_seg:(s + 1) * n_seg, cols] = vr
            xi_ref[s * n_seg:(s + 1) * n_seg, cols] = vi
        for s in (1, 2, 4):
            dr, di = _cmul(par_ref[s:s + 1, cols], pai_ref[s:s + 1, cols], shifted(vr, s), shifted(vi, s))
            vr, vi = vr + dr, vi + di
        in_r, in_i = car_ref[:, cols], cai_ref[:, cols]
        dr, di = _cmul(par_ref[0:n_seg, cols], pai_ref[0:n_seg, cols], in_r, in_i)
        seg_r, seg_i = shifted(vr, 1) + dr, shifted(vi, 1) + di
        dr, di = _cmul(par_ref[n_seg:n_seg + 1, cols], pai_ref[n_seg:n_seg + 1, cols], in_r, in_i)
        car_ref[:, cols] = jnp.broadcast_to(vr[n_seg - 1:n_seg], vr.shape) + dr
        cai_ref[:, cols] = jnp.broadcast_to(vi[n_seg - 1:n_seg], vi.shape) + di
        for s in range(n_steps):
            dr, di = _cmul(pwr_ref[s:s + 1, cols], pwi_ref[s:s + 1, cols], seg_r, seg_i)
            xr_ref[s * n_seg:(s + 1) * n_seg, cols] += dr
            xi_ref[s * n_seg:(s + 1) * n_seg, cols] += di
    sr_ref[...] = car_ref[0:1, :]
    si_ref[...] = cai_ref[0:1, :]
    y_perm = jnp.concatenate(_ssm_readout(xr_ref, xi_ref, cr_ref, ci_ref), axis=1)
    for c in range(n_tiles):
        perm_ref[c] = y_perm[:, c * lanes:(c + 1) * lanes]
    y = jnp.concatenate(
        [jnp.concatenate([perm_ref[c, pl.ds(seg, n_steps, stride=n_seg), :] for seg in range(n_seg)], axis=0)
         for c in range(n_tiles)], axis=1)
    _ssm_output(x, u, y, m_ref, g_ref, d_ref, w1_ref, b1_ref, w2_ref, b2_ref, o_ref)


def _ssm_sample_body(x_ref, m_ref, g_ref, bbr_ref, bbi_ref, pwr_ref, pwi_ref, cr_ref, ci_ref, d_ref,
                     w1_ref, b1_ref, w2_ref, b2_ref, hr_ref, hi_ref, o_ref, sr_ref, si_ref,
                     xr_ref, xi_ref):
    x, u = _ssm_input(x_ref, m_ref, g_ref, bbr_ref, bbi_ref, xr_ref, xi_ref)
    dr, di = _cmul(pwr_ref[0:1, :], pwi_ref[0:1, :], hr_ref[...], hi_ref[...])
    xr_ref[...] = xr_ref[...] + dr
    xi_ref[...] = xi_ref[...] + di
    sr_ref[...] = xr_ref[...]
    si_ref[...] = xi_ref[...]
    y = jnp.concatenate(_ssm_readout(xr_ref, xi_ref, cr_ref, ci_ref), axis=1)
    _ssm_output(x, u, y, m_ref, g_ref, d_ref, w1_ref, b1_ref, w2_ref, b2_ref, o_ref)


def _ssm_tables(lam_re, lam_im, b_re, b_im, c_re, c_im, log_step, n_steps):
    g, p = lam_re.shape
    w = b_re.shape[2]
    pwr, pwi, par, pai, bbr, bbi = _ssm_prep(lam_re, lam_im, log_step, b_re, b_im, n_steps)
    gb = 256 // w
    eye = jnp.eye(gb, dtype=F32)

    def bd_in(a):
        return jnp.einsum("kgip,gh->kgihp", a.reshape(g // gb, gb, w, p), eye).reshape(g // gb, gb * w, gb * p)

    def bd_out(a):
        return jnp.einsum("kgip,gh->kgphi", a.reshape(g // gb, gb, w, p), eye).reshape(g // gb, gb * p, gb * w)

    tables = (bd_in(bbr).astype(BF16), bd_in(bbi).astype(BF16), pwr.reshape(n_steps, g * p),
              pwi.reshape(n_steps, g * p), bd_out(c_re).astype(BF16), bd_out(c_im).astype(BF16))
    return tables, (par.reshape(-1, g * p), pai.reshape(-1, g * p))


def _ssm_prompt(x, mod, g, tables, seg_tables, d_skip, w1, b1, w2, b2, n_seq, tl):
    n, d = x.shape
    t = n // n_seq
    n_state = tables[2].shape[1]
    assert tl == SSM_SEGMENTS * tables[2].shape[0]
    consts = tables + (d_skip[None], w1, b1[None], w2, b2[None]) + seg_tables
    row_spec = pl.BlockSpec((tl, d), lambda b, c: (b * (t // tl) + c, 0))
    st_spec = pl.BlockSpec((None, 1, n_state), lambda b, c: (b, 0, 0))
    return pl.pallas_call(
        functools.partial(_ssm_prompt_body, lane_block=1024),
        grid=(n_seq, t // tl),
        in_specs=[row_spec, pl.BlockSpec((6, None, 1, d), lambda b, c: (0, b, 0, 0)), _const_spec(g, 2)]
        + [_const_spec(a, 2) for a in consts],
        out_specs=[row_spec, st_spec, st_spec],
        out_shape=[jax.ShapeDtypeStruct((n, d), F32), jax.ShapeDtypeStruct((n_seq, 1, n_state), F32),
                   jax.ShapeDtypeStruct((n_seq, 1, n_state), F32)],
        scratch_shapes=[pltpu.VMEM((d // V7X_LANES, tl, V7X_LANES), F32),
                        pltpu.VMEM((tl, n_state), F32), pltpu.VMEM((tl, n_state), F32),
                        pltpu.VMEM((SSM_SEGMENTS, n_state), F32), pltpu.VMEM((SSM_SEGMENTS, n_state), F32)],
        compiler_params=_params("arbitrary", "arbitrary"),
        name="ssm_prompt",
    )(x, mod, g, *consts)


def _ssm_sample(x, mod, g, tables, d_skip, w1, b1, w2, b2, h_re, h_im):
    n, d = x.shape
    n_state = tables[2].shape[1]
    consts = tables + (d_skip[None], w1, b1[None], w2, b2[None], h_re, h_im)
    full = pl.BlockSpec((n, d), lambda i: (0, 0))
    st = pl.BlockSpec((n, n_state), lambda i: (0, 0))
    return pl.pallas_call(
        _ssm_sample_body,
        grid=(1,),
        in_specs=[full, _mod_spec(mod, 1), _const_spec(g)] + [_const_spec(a) for a in consts],
        out_specs=[full, st, st],
        out_shape=[jax.ShapeDtypeStruct((n, d), F32), jax.ShapeDtypeStruct((n, n_state), F32),
                   jax.ShapeDtypeStruct((n, n_state), F32)],
        scratch_shapes=[pltpu.VMEM((n, n_state), F32), pltpu.VMEM((n, n_state), F32)],
        compiler_params=_params("arbitrary"),
        name="ssm_sample",
    )(x, mod, g, *consts)


def _nsa_proj_body(x_ref, m_ref, g_ref, w_ref, q_ref, kc_ref, ks_ref, kw_ref, gt_ref, ksb_ref, kwb_ref, *t_refs):
    h = _modulate(x_ref[...], g_ref[0:1], m_ref[0], m_ref[1]).astype(BF16)
    qc, kc = q_ref.shape[1], kc_ref.shape[1]
    q_ref[...] = (_dot(h, w_ref[:, :qc]) * (NSA_HEAD_DIM ** -0.5)).astype(BF16)
    kc_ref[...] = _dot(h, w_ref[:, qc:qc + kc])
    ks = _dot(h, w_ref[:, qc + kc:qc + 2 * kc])
    kw = _dot(h, w_ref[:, qc + 2 * kc:qc + 3 * kc])
    ks_ref[...] = ks
    kw_ref[...] = kw
    ksb_ref[...] = ks.astype(BF16)
    kwb_ref[...] = kw.astype(BF16)
    gt_ref[...] = jax.nn.sigmoid(_dot(h, w_ref[:, qc + 3 * kc:]))
    for t_ref, rows in zip(t_refs, (kc_ref[...], ks, kw)):
        t_ref[...] = rows.T


def _nsa_proj(x, mod, g, w_in, tm, n_seq=None):
    n, d = x.shape
    n_tiles = n // tm
    kc = 2 * NSA_KV_HEADS * NSA_HEAD_DIM
    ng = (w_in.shape[1] - d - 3 * kc)
    widths = [(d, BF16), (kc, F32), (kc, F32), (kc, F32), (ng, F32), (kc, BF16), (kc, BF16)]
    out_specs = [pl.BlockSpec((tm, w), lambda i: (i, 0)) for w, _ in widths]
    out_shape = [jax.ShapeDtypeStruct((n, w), dt) for w, dt in widths]
    if n_seq is not None:
        tps = n_tiles // n_seq
        out_specs += [pl.BlockSpec((None, kc, tm), lambda i: (i // tps, 0, i % tps))] * 3
        out_shape += [jax.ShapeDtypeStruct((n_seq, kc, n // n_seq), F32)] * 3
    return pl.pallas_call(
        _nsa_proj_body,
        grid=(n_tiles,),
        in_specs=[pl.BlockSpec((tm, d), lambda i: (i, 0)), _mod_spec(mod, n_tiles), _const_spec(g),
                  _const_spec(w_in)],
        out_specs=out_specs,
        out_shape=out_shape,
        compiler_params=_params("arbitrary"),
        name="nsa_proj",
    )(x, mod, g, w_in)


def _compress_step(x_of, ls, pe_ref, w1_ref, w2_ref, o_ref, acc_ref, n_l):
    dh = NSA_HEAD_DIM
    hid = w1_ref.shape[3]
    n_zg = 2 * NSA_KV_HEADS

    @pl.when(ls == 0)
    def _():
        acc_ref[...] = jnp.zeros_like(acc_ref)

    for ll in range(n_l):
        l = ls * n_l + ll
        xb = (x_of(ll) + pe_ref[pl.ds(l, 1), :]).astype(BF16)
        for zg in range(n_zg):
            acc_ref[:, zg * hid:(zg + 1) * hid] += _dot(xb[:, zg * dh:(zg + 1) * dh], w1_ref[zg // NSA_KV_HEADS, l])

    @pl.when(ls == pl.num_programs(1) - 1)
    def _():
        a = acc_ref[...]
        hidv = (a * jax.nn.sigmoid(a)).astype(BF16)
        for zg in range(n_zg):
            o_ref[:, zg * dh:(zg + 1) * dh] = _dot(hidv[:, zg * hid:(zg + 1) * hid], w2_ref[zg // NSA_KV_HEADS])


def _compress_prompt_body(x_ref, pe_ref, w1_ref, w2_ref, o_ref, acc_ref, *, n_l):
    _compress_step(lambda ll: x_ref[:, ll, :], pl.program_id(1), pe_ref, w1_ref, w2_ref, o_ref, acc_ref, n_l)


def _compress_sample_body(pt_ref, cache_ref, pe_ref, w1_ref, w2_ref, o_ref, buf_ref, sem, acc_ref, *,
                          pages_per_step, d_tiles):
    pg, ds = pl.program_id(0), pl.program_id(1)
    n_ds = pl.num_programs(1)
    n_zg = 2 * NSA_KV_HEADS
    dh = NSA_HEAD_DIM
    per_page = o_ref.shape[0]
    hid = w1_ref.shape[3] // per_page
    step = pg * n_ds + ds
    slot = step % 2

    def page_copy(p, at_step, at_slot):
        page = pt_ref[(at_step // n_ds) * pages_per_step + p]
        return pltpu.make_async_copy(cache_ref.at[page, :, pl.ds((at_step % n_ds) * d_tiles, d_tiles)],
                                     buf_ref.at[at_slot, :, :, pl.ds(p * 8, 8), :], sem.at[at_slot])

    def start_all(at_step, at_slot):
        def start(p, c):
            page_copy(p, at_step, at_slot).start()
            return c

        lax.fori_loop(0, pages_per_step, start, 0)

    @pl.when(step == 0)
    def _():
        start_all(step, slot)

    @pl.when(step + 1 < pl.num_programs(0) * n_ds)
    def _():
        start_all(step + 1, 1 - slot)

    @pl.when(ds == 0)
    def _():
        acc_ref[...] = jnp.zeros_like(acc_ref)

    def wait(p, c):
        page_copy(p, step, slot).wait()
        return c

    lax.fori_loop(0, pages_per_step, wait, 0)
    for zg in range(n_zg):
        z = zg // NSA_KV_HEADS
        for dt in range(d_tiles):
            for dd in range(8):
                d = dt * 8 + dd
                x = buf_ref[slot, zg, dt, pl.ds(dd, pages_per_step, stride=8), :] + pe_ref[zg, d:d + 1, :]
                acc_ref[zg] += _dot(x, w1_ref[z, d])

    @pl.when(ds == pl.num_programs(1) - 1)
    def _():
        for zg in range(n_zg):
            a = acc_ref[zg]
            hidv = (a * jax.nn.sigmoid(a)).astype(BF16)
            for n in range(per_page):
                o_ref[n, :, zg * dh:(zg + 1) * dh] = _dot(hidv[:, n * hid:(n + 1) * hid], w2_ref[zg // NSA_KV_HEADS])


def _pe_rows(pe):
    blk = pe.shape[1]
    return jnp.broadcast_to(pe.transpose(1, 0, 2)[:, :, None, :], (blk, 2, NSA_KV_HEADS, pe.shape[2])).reshape(blk, -1)


_COMPRESS_ROWS = 16


def _compress_prompt(kc, pe, w1, w2, nbt):
    n, c = kc.shape
    nblk = n // NSA_BLOCK
    n_l = _COMPRESS_ROWS
    x3 = kc.reshape(nblk, NSA_BLOCK, c)
    return pl.pallas_call(
        functools.partial(_compress_prompt_body, n_l=n_l),
        grid=(nblk // nbt, NSA_BLOCK // n_l),
        in_specs=[pl.BlockSpec((nbt, n_l, c), lambda i, l: (i, l, 0)), _const_spec(pe, 2), _const_spec(w1, 2),
                  _const_spec(w2, 2)],
        out_specs=pl.BlockSpec((nbt, c), lambda i, l: (i, 0)),
        out_shape=jax.ShapeDtypeStruct((nblk, c), F32),
        scratch_shapes=[pltpu.VMEM((nbt, 2 * NSA_KV_HEADS * w1.shape[3]), F32)],
        compiler_params=_params("arbitrary", "arbitrary"),
        name="compress_prompt",
    )(x3, pe, w1, w2)


def _rows_on_lanes(cache):
    return jnp.transpose(cache, (0, 2, 3, 4, 1))


def _compress_sample(page_table, cache, pe, w1, w2, pages_per_step):
    n_pool, page = cache.shape[:2]
    dh = NSA_HEAD_DIM
    n_zg = 2 * NSA_KV_HEADS
    c = n_zg * dh
    per_page = page // NSA_BLOCK
    hid = w1.shape[3]
    d_tiles = 2
    n_pages_total = page_table.size
    cache_t = _rows_on_lanes(cache).reshape(n_pool, n_zg, dh // 8, 8, page)
    pe_t = jnp.tile(jnp.repeat(jnp.swapaxes(pe, 1, 2), NSA_KV_HEADS, axis=0), (1, 1, per_page))
    w1_t = jnp.einsum("zlde,nm->zdnlme", w1, jnp.eye(per_page, dtype=w1.dtype)).reshape(2, dh, page, per_page * hid)
    grid_spec = pltpu.PrefetchScalarGridSpec(
        num_scalar_prefetch=1,
        grid=(n_pages_total // pages_per_step, dh // (8 * d_tiles)),
        in_specs=[pl.BlockSpec(memory_space=pl.ANY),
                  pl.BlockSpec((n_zg, 8 * d_tiles, page), lambda i, s, pt: (0, s, 0)),
                  pl.BlockSpec((2, 8 * d_tiles, page, per_page * hid), lambda i, s, pt: (0, s, 0, 0)),
                  pl.BlockSpec(w2.shape, lambda i, s, pt: (0, 0, 0))],
        out_specs=pl.BlockSpec((per_page, pages_per_step, c), lambda i, s, pt: (0, i, 0)),
        scratch_shapes=[pltpu.VMEM((2, n_zg, d_tiles, pages_per_step * 8, page), F32), pltpu.SemaphoreType.DMA((2,)),
                        pltpu.VMEM((n_zg, pages_per_step, per_page * hid), F32)],
    )
    out = pl.pallas_call(
        functools.partial(_compress_sample_body, pages_per_step=pages_per_step, d_tiles=d_tiles),
        grid_spec=grid_spec,
        out_shape=jax.ShapeDtypeStruct((per_page, n_pages_total, c), F32),
        compiler_params=_params("arbitrary", "arbitrary"),
        name="compress_sample",
    )(page_table.reshape(-1), cache_t, pe_t, w1_t, w2)
    return jnp.swapaxes(out, 0, 1).reshape(n_pages_total * per_page, c)


def _stack_heads(q, grp):
    dh = NSA_HEAD_DIM
    rep = q.shape[1] // (NSA_KV_HEADS * dh)
    base = grp * rep * dh
    return jnp.concatenate([q[:, base + r * dh:base + (r + 1) * dh] for r in range(rep)], axis=0)


def _cmp_branch(qs, cmpv, grp, t_row, rep):
    dh = NSA_HEAD_DIM
    kv = NSA_KV_HEADS * dh
    kc = cmpv[:, grp * dh:(grp + 1) * dh]
    vc = cmpv[:, kv + grp * dh:kv + (grp + 1) * dh]
    s = _dot_nt(qs, kc)
    n = lax.broadcasted_iota(jnp.int32, s.shape, 1)
    mask = (n + 1) * NSA_BLOCK <= t_row + 1
    s = jnp.where(mask, s, NEG_BIG)
    e = jnp.where(mask, jnp.exp(s - jnp.max(s, axis=-1, keepdims=True)), 0.0)
    p = e / jnp.maximum(jnp.sum(e, axis=-1, keepdims=True), 1e-30)
    o = _dot(p, vc)
    t = p.shape[0] // rep
    imp = p[0:t]
    for r in range(1, rep):
        imp = imp + p[r * t:(r + 1) * t]
    return o, imp


def _topk_mask(score, axis):
    idx = lax.broadcasted_iota(jnp.int32, score.shape, axis).astype(F32)
    n = float(score.shape[axis])
    sel = jnp.zeros(score.shape, F32)
    x = score
    for _ in range(NSA_TOPK):
        m = jnp.max(x, axis=axis, keepdims=True)
        first = jnp.min(jnp.where(x == m, idx, n), axis=axis, keepdims=True)
        pick = idx == first
        sel = jnp.where(pick, 1.0, sel)
        x = jnp.where(pick, -jnp.inf, x)
    return sel


def _cmpattn_prompt_body(q_ref, cmp_ref, o_ref, sel_ref):
    tq = q_ref.shape[0]
    nb = cmp_ref.shape[0]
    dh = NSA_HEAD_DIM
    rep = q_ref.shape[1] // (NSA_KV_HEADS * dh)
    t0 = pl.program_id(1) * tq
    q = q_ref[...]
    cmpv = cmp_ref[...]
    t_row = t0 + lax.broadcasted_iota(jnp.int32, (rep * tq, 1), 0) % tq
    blk = lax.broadcasted_iota(jnp.int32, (nb, tq), 0)
    jt = (t0 + lax.broadcasted_iota(jnp.int32, (nb, tq), 1)) // NSA_BLOCK
    forced = (blk == 0) | (blk == jt) | (blk == jt - 1)
    for grp in range(NSA_KV_HEADS):
        o, imp = _cmp_branch(_stack_heads(q, grp), cmpv, grp, t_row, rep)
        for r in range(rep):
            h = grp * rep + r
            o_ref[:, h * dh:(h + 1) * dh] = o[r * tq:(r + 1) * tq]
        score = jnp.where(blk <= jt, jnp.where(forced, SEL_FORCE, imp.T), SEL_MASKED)
        sel = _topk_mask(score, 0) * (score > 0.5 * SEL_MASKED).astype(F32)
        sel_ref[:, grp * nb:(grp + 1) * nb] = jnp.where(sel.T > 0.5, 0.0, NEG_BIG).astype(BF16)


def _cmpattn_prompt(q, cmp, n_seq, tq):
    n, qc = q.shape
    t = n // n_seq
    nb = cmp.shape[0] // n_seq
    return pl.pallas_call(
        _cmpattn_prompt_body,
        grid=(n_seq, t // tq),
        in_specs=[pl.BlockSpec((tq, qc), lambda b, i: (b * (t // tq) + i, 0)),
                  pl.BlockSpec((nb, cmp.shape[1]), lambda b, i: (b, 0))],
        out_specs=[pl.BlockSpec((tq, qc), lambda b, i: (b * (t // tq) + i, 0)),
                   pl.BlockSpec((tq, NSA_KV_HEADS * nb), lambda b, i: (b * (t // tq) + i, 0))],
        out_shape=[jax.ShapeDtypeStruct((n, qc), F32), jax.ShapeDtypeStruct((n, NSA_KV_HEADS * nb), BF16)],
        compiler_params=_params("arbitrary", "arbitrary"),
        name="cmpattn_prompt",
    )(q, cmp)


def _cmpattn_sample_body(q_ref, cmp_ref, o_ref, idx_ref, *, t_pos, n_cand):
    nb = cmp_ref.shape[0]
    dh = NSA_HEAD_DIM
    rep = q_ref.shape[1] // (NSA_KV_HEADS * dh)
    q = q_ref[...]
    cmpv = cmp_ref[...]
    t_row = jnp.full((rep, 1), t_pos, jnp.int32)
    width = idx_ref.shape[1]
    lanes = ((n_cand + 127) // 128) * 128
    blk = lax.broadcasted_iota(jnp.int32, (1, lanes), 1)
    jt = t_pos // NSA_BLOCK
    forced = (blk == 0) | (blk == jt) | (blk == jt - 1)
    col = lax.broadcasted_iota(jnp.int32, (1, width), 1)
    blk_f = blk.astype(F32)
    for grp in range(NSA_KV_HEADS):
        o, imp = _cmp_branch(_stack_heads(q, grp), cmpv, grp, t_row, rep)
        for r in range(rep):
            h = grp * rep + r
            o_ref[:, h * dh:(h + 1) * dh] = o[r:r + 1]
        imp = jnp.concatenate([imp, jnp.zeros((1, lanes - nb), F32)], axis=1)
        score = jnp.where(blk <= jt, jnp.where(forced, SEL_FORCE, imp), SEL_MASKED)
        x = jnp.where(blk < n_cand, score, -jnp.inf)
        row = jnp.full((1, width), -1, jnp.int32)
        for k in range(NSA_TOPK):
            m = jnp.max(x, axis=1, keepdims=True)
            first = jnp.min(jnp.where(x == m, blk_f, float(lanes)), axis=1, keepdims=True)
            chosen = jnp.where(m > 0.5 * SEL_MASKED, first, -1.0).astype(jnp.int32)
            row = jnp.where(col == k, chosen, row)
            x = jnp.where(blk_f == first, -jnp.inf, x)
        idx_ref[grp:grp + 1, :] = row


def _cmpattn_sample(q, cmp, t_pos, n_cand):
    n_seq, qc = q.shape
    nb = cmp.shape[0] // n_seq
    return pl.pallas_call(
        functools.partial(_cmpattn_sample_body, t_pos=t_pos, n_cand=n_cand),
        grid=(n_seq,),
        in_specs=[pl.BlockSpec((None, 1, qc), lambda b: (b, 0, 0)), pl.BlockSpec((nb, cmp.shape[1]), lambda b: (b, 0))],
        out_specs=[pl.BlockSpec((None, 1, qc), lambda b: (b, 0, 0)),
                   pl.BlockSpec((None, NSA_KV_HEADS, 128), lambda b: (b, 0, 0))],
        out_shape=[jax.ShapeDtypeStruct((n_seq, 1, qc), F32), jax.ShapeDtypeStruct((n_seq, NSA_KV_HEADS, 128), jnp.int32)],
        compiler_params=_params("arbitrary"),
        name="cmpattn_sample",
    )(q[:, None, :], cmp)


def _attn_prompt_body(q_ref, sel_ref, ks_ref, kw_ref, oslc_ref, owin_ref, qa_ref, m_ref, acc_ref, *, tk, n_sub):
    tq = q_ref.shape[0]
    ts = tq // n_sub
    dh = NSA_HEAD_DIM
    slab = 2 * dh
    kv = NSA_KV_HEADS * dh
    rep = q_ref.shape[1] // kv
    rows, rows_s = rep * tq, rep * ts
    nb = sel_ref.shape[1] // NSA_KV_HEADS
    t0 = pl.program_id(1) * tq
    lane = lax.broadcasted_iota(jnp.int32, (ts, slab), 1)

    for grp in range(NSA_KV_HEADS):
        off = (grp % 2) * dh
        for sub in range(n_sub):
            tok = slice(sub * ts, (sub + 1) * ts)
            parts = []
            for r in range(rep):
                h = grp * rep + r
                x = q_ref[tok, (h // 2) * slab:(h // 2 + 1) * slab].astype(F32)
                if h % 2 != grp % 2:
                    x = pltpu.roll(x, dh, axis=1)
                parts.append(jnp.where((lane >= off) & (lane < off + dh), x, 0.0))
            rr = slice(sub * rows_s, (sub + 1) * rows_s)
            qa_ref[grp, rr, 0:slab] = jnp.concatenate(parts, axis=0).astype(BF16)
            qa_ref[grp, rr, slab:slab + nb] = jnp.concatenate([sel_ref[tok, grp * nb:(grp + 1) * nb]] * rep, axis=0)
    m_ref[...] = jnp.full(m_ref.shape, 0.1 * NEG_BIG, F32)
    acc_ref[...] = jnp.zeros_like(acc_ref)

    lane_k = lax.broadcasted_iota(jnp.int32, (tk, slab), 1)
    own_half = [lane_k < dh, lane_k >= dh]

    def flash_tile(r0, n_rows, k0, bias):
        rr = slice(r0, r0 + n_rows)
        key_blk = (k0 + lax.broadcasted_iota(jnp.int32, (tk, nb), 0)) // NSA_BLOCK
        onehot = (key_blk == lax.broadcasted_iota(jnp.int32, (tk, nb), 1)).astype(BF16)
        for grp in range(NSA_KV_HEADS):
            pair = (grp // 2) * slab
            k_aug = jnp.concatenate([ks_ref[pl.ds(k0, tk), pair:pair + slab], onehot], axis=1)
            s = _dot_nt(qa_ref[grp, rr], k_aug)
            if bias is not None:
                s = (s.reshape(rep, n_rows // rep, tk) + bias[None]).reshape(n_rows, tk)
            cols = [s[:, c * slab:(c + 1) * slab] for c in range(tk // slab)]
            mx = functools.reduce(jnp.maximum, cols)
            m_old = m_ref[grp, rr]
            m_new = jnp.maximum(m_old, jnp.max(mx, axis=-1, keepdims=True))
            alpha = jnp.exp(m_old - m_new)
            p = jnp.concatenate([jnp.exp(c - m_new).astype(BF16) for c in cols], axis=1)
            v = ks_ref[pl.ds(k0, tk), kv + pair:kv + pair + slab]
            v = jnp.where(own_half[grp % 2], v, jnp.ones_like(v))
            acc_ref[grp, rr] = alpha * acc_ref[grp, rr] + _dot(p, v)
            m_ref[grp, rr] = m_new

    n_full = t0 // tk

    def full_tile(j, carry):
        flash_tile(0, rows, pl.multiple_of(j * tk, tk), None)
        return carry

    lax.fori_loop(0, n_full, full_tile, 0)

    win_len = NSA_WINDOW + ts
    for sub in range(n_sub):
        ts0 = t0 + sub * ts
        r0 = sub * rows_s
        t_col = ts0 + lax.broadcasted_iota(jnp.int32, (ts, 1), 0)
        j_own = ts0 // tk
        if sub > 0:

            def before_own(j, carry, r0=r0):
                flash_tile(r0, rows_s, pl.multiple_of(j * tk, tk), None)
                return carry

            lax.fori_loop(n_full, j_own, before_own, 0)
        k0 = pl.multiple_of(j_own * tk, tk)
        late = k0 + lax.broadcasted_iota(jnp.int32, (1, tk), 1) > t_col
        flash_tile(r0, rows_s, k0, jnp.where(late, NEG_BIG, 0.0))

        w0 = pl.multiple_of(jnp.maximum(ts0 - NSA_WINDOW, 0), ts)
        wpos = w0 + lax.broadcasted_iota(jnp.int32, (1, win_len), 1)
        win_bias = jnp.where((wpos <= t_col) & (wpos >= t_col - NSA_WINDOW), 0.0, NEG_BIG)
        tok = slice(sub * ts, (sub + 1) * ts)
        for grp in range(NSA_KV_HEADS):
            off = (grp % 2) * dh
            pair = (grp // 2) * slab
            acc = acc_ref[grp, r0:r0 + rows_s]
            o = acc / jnp.maximum(acc[:, dh - off:dh - off + 1], 1e-30)
            for r in range(rep):
                h = grp * rep + r
                oslc_ref[tok, h * dh:(h + 1) * dh] = o[r * ts:(r + 1) * ts, off:off + dh]

            s = _dot_nt(qa_ref[grp, r0:r0 + rows_s, 0:slab], kw_ref[pl.ds(w0, win_len), pair:pair + slab])
            s = s.reshape(rep, ts, win_len) + win_bias[None]
            m = jnp.maximum(jnp.max(s, axis=-1, keepdims=True), 0.1 * NEG_BIG)
            e = jnp.exp(s - m)
            o = _dot(e.reshape(rows_s, win_len), kw_ref[pl.ds(w0, win_len), kv + pair:kv + pair + slab])
            o = o / jnp.maximum(jnp.sum(e, axis=-1, keepdims=True).reshape(rows_s, 1), 1e-30)
            for r in range(rep):
                h = grp * rep + r
                owin_ref[tok, h * dh:(h + 1) * dh] = o[r * ts:(r + 1) * ts, off:off + dh]


def _attn_prompt(q, sel, ksb, kwb, n_seq):
    n, qc = q.shape
    t = n // n_seq
    dh = NSA_HEAD_DIM
    rep = qc // (NSA_KV_HEADS * dh)
    tq, ts, tk = ATTN_Q_TILE, ATTN_Q_SUB, ATTN_KV_TILE
    assert t % tk == 0 and t % tq == 0 and tq % ts == 0 and tk % ts == 0
    assert t >= NSA_WINDOW + ts and NSA_WINDOW % ts == 0
    tile = pl.BlockSpec((tq, qc), lambda b, i: (b * (t // tq) + i, 0))
    seq = pl.BlockSpec((t, ksb.shape[1]), lambda b, i: (b, 0))
    return pl.pallas_call(
        functools.partial(_attn_prompt_body, tk=tk, n_sub=tq // ts),
        grid=(n_seq, t // tq),
        in_specs=[tile, pl.BlockSpec((tq, sel.shape[1]), lambda b, i: (b * (t // tq) + i, 0)), seq, seq],
        out_specs=[tile, tile],
        out_shape=[jax.ShapeDtypeStruct((n, qc), F32), jax.ShapeDtypeStruct((n, qc), F32)],
        scratch_shapes=[pltpu.VMEM((NSA_KV_HEADS, rep * tq, 2 * dh + sel.shape[1] // NSA_KV_HEADS), BF16)]
        + [pltpu.VMEM((NSA_KV_HEADS, rep * tq, 2 * dh), F32)] * 2,
        compiler_params=_params("arbitrary", "arbitrary"),
        name="attn_prompt",
    )(q, sel, ksb, kwb)


def _softmax_with_new_key(s, ok, s_new, new_ok):
    s = jnp.where(ok, s, NEG_BIG)
    s_new = jnp.where(new_ok, s_new, NEG_BIG)
    m = jnp.maximum(jnp.max(s, axis=-1, keepdims=True), s_new)
    e = jnp.where(ok, jnp.exp(s - m), 0.0)
    e_new = jnp.where(new_ok, jnp.exp(s_new - m), 0.0)
    return e, e_new, jnp.maximum(jnp.sum(e, axis=-1, keepdims=True) + e_new, 1e-30)


def _bf16_round(x):
    return x.astype(BF16).astype(F32)


def _attn_sample_body(pt_ref, idx_ref, q_ref, ksn_ref, kwn_ref, win_ref, cache_ref, oslc_ref, owin_ref,
                      kbuf_ref, sem, *, t_pos, nb_past, n_pages):
    b = pl.program_id(0)
    dh = NSA_HEAD_DIM
    kv = NSA_KV_HEADS * dh
    rep = q_ref.shape[1] // kv
    n_sel = NSA_TOPK
    page = cache_ref.shape[4]
    per_page = page // NSA_BLOCK
    q = q_ref[...]

    def sel_index(grp, k):
        return idx_ref[(b * NSA_KV_HEADS + grp) * 128 + k]

    def in_pool(idx):
        return (idx >= 0) & (idx < nb_past)

    def page_copy(grp, k, idx):
        phys = pt_ref[b * n_pages + jnp.minimum(idx // per_page, n_pages - 1)]
        return pltpu.make_async_copy(cache_ref.at[phys, :, grp], kbuf_ref.at[grp, :, :, pl.ds(k * page, page)], sem)

    for grp in range(NSA_KV_HEADS):
        for k in range(n_sel):
            idx = sel_index(grp, k)

            @pl.when(in_pool(idx))
            def _():
                page_copy(grp, k, idx).start()

            @pl.when(jnp.logical_not(in_pool(idx)))
            def _():
                kbuf_ref[grp, :, :, k * page:(k + 1) * page] = jnp.zeros((2, dh, page), F32)

    for grp in range(NSA_KV_HEADS):
        for k in range(n_sel):
            idx = sel_index(grp, k)

            @pl.when(in_pool(idx))
            def _():
                page_copy(grp, k, idx).wait()

    lane = lax.broadcasted_iota(jnp.int32, (1, n_sel * page), 1)
    wb = win_ref.shape[3]
    wpos = t_pos - wb + lax.broadcasted_iota(jnp.int32, (1, wb), 1)
    win_ok = (wpos <= t_pos) & (wpos >= t_pos - NSA_WINDOW) & (wpos >= 0)
    for grp in range(NSA_KV_HEADS):
        qs = _stack_heads(q, grp)
        qf = qs.astype(F32)
        ok = jnp.zeros((1, n_sel * page), jnp.bool_)
        has_new = False
        for k in range(n_sel):
            idx = sel_index(grp, k)
            row = lane - k * page
            kpos = (idx // per_page) * page + row
            ok = ok | ((lane // page == k) & in_pool(idx) & (row // NSA_BLOCK == idx % per_page) & (kpos <= t_pos))
            has_new = has_new | (idx >= nb_past)
        new_ok = has_new & (nb_past * NSA_BLOCK <= t_pos)
        k_new = _bf16_round(ksn_ref[:, grp * dh:(grp + 1) * dh])
        v_new = _bf16_round(ksn_ref[:, kv + grp * dh:kv + (grp + 1) * dh])
        s_new = jnp.sum(qf * k_new, axis=-1, keepdims=True)
        e, e_new, den = _softmax_with_new_key(_dot(qs, kbuf_ref[grp, 0]), ok, s_new, new_ok)
        o = (_dot_nt(e, kbuf_ref[grp, 1]) + _bf16_round(e_new) * v_new) / den
        for r in range(rep):
            h = grp * rep + r
            oslc_ref[:, h * dh:(h + 1) * dh] = o[r:r + 1]

        k_new = _bf16_round(kwn_ref[:, grp * dh:(grp + 1) * dh])
        v_new = _bf16_round(kwn_ref[:, kv + grp * dh:kv + (grp + 1) * dh])
        s_new = jnp.sum(qf * k_new, axis=-1, keepdims=True)
        e, e_new, den = _softmax_with_new_key(_dot(qs, win_ref[0, grp]), win_ok, s_new, True)
        o = (_dot_nt(e, win_ref[1, grp]) + _bf16_round(e_new) * v_new) / den
        for r in range(rep):
            h = grp * rep + r
            owin_ref[:, h * dh:(h + 1) * dh] = o[r:r + 1]


def _attn_sample(page_table, sel_idx, q, ks_new, kw_new, win, cache, t_pos):
    n_seq, qc = q.shape
    n_pool, page = cache.shape[:2]
    dh = NSA_HEAD_DIM
    c = 2 * NSA_KV_HEADS * dh
    n_pages = page_table.shape[1]
    wb = win.shape[1]
    row3 = lambda w: pl.BlockSpec((None, 1, w), lambda b, pt, ix: (b, 0, 0))
    grid_spec = pltpu.PrefetchScalarGridSpec(
        num_scalar_prefetch=2,
        grid=(n_seq,),
        in_specs=[row3(qc), row3(c), row3(c),
                  pl.BlockSpec((None, 2, NSA_KV_HEADS, dh, wb), lambda b, pt, ix: (b, 0, 0, 0, 0)),
                  pl.BlockSpec(memory_space=pl.ANY)],
        out_specs=[row3(qc), row3(qc)],
        scratch_shapes=[pltpu.VMEM((NSA_KV_HEADS, 2, dh, NSA_TOPK * page), F32), pltpu.SemaphoreType.DMA(())],
    )
    win, cache = _rows_on_lanes(win), _rows_on_lanes(cache)
    return pl.pallas_call(
        functools.partial(_attn_sample_body, t_pos=t_pos, nb_past=t_pos // NSA_BLOCK, n_pages=n_pages),
        grid_spec=grid_spec,
        out_shape=[jax.ShapeDtypeStruct((n_seq, 1, qc), F32), jax.ShapeDtypeStruct((n_seq, 1, qc), F32)],
        compiler_params=_params("arbitrary"),
        name="attn_sample",
    )(page_table.reshape(-1), sel_idx.reshape(-1), q[:, None, :], ks_new[:, None, :], kw_new[:, None, :], win, cache)


def _nsa_merge_body(x_ref, m_ref, g_ref, oc_ref, os_ref, ow_ref, gt_ref, wout_ref, o_ref, om_ref):
    dh = NSA_HEAD_DIM
    n_heads = oc_ref.shape[1] // dh
    gt = gt_ref[...]
    for h in range(n_heads):
        c = slice(h * dh, (h + 1) * dh)
        o = (gt[:, h:h + 1] * oc_ref[:, c] + gt[:, n_heads + h:n_heads + h + 1] * os_ref[:, c]
             + gt[:, 2 * n_heads + h:2 * n_heads + h + 1] * ow_ref[:, c])
        om_ref[:, c] = o.astype(BF16)
    x = x_ref[...]
    o_ref[...] = x + m_ref[2] * _rms(_dot(om_ref[...], wout_ref[...]), g_ref[1:2])


def _nsa_merge(x, mod, g, o_cmp, o_slc, o_win, gates, w_out, tm):
    n, d = x.shape
    n_tiles = n // tm
    qc = o_cmp.shape[1]
    tile = lambda w: pl.BlockSpec((tm, w), lambda i: (i, 0))
    return pl.pallas_call(
        _nsa_merge_body,
        grid=(n_tiles,),
        in_specs=[tile(d), _mod_spec(mod, n_tiles), _const_spec(g), tile(qc), tile(qc), tile(qc),
                  tile(gates.shape[1]), _const_spec(w_out)],
        out_specs=tile(d),
        out_shape=jax.ShapeDtypeStruct((n, d), F32),
        scratch_shapes=[pltpu.VMEM((tm, qc), BF16)],
        compiler_params=_params("arbitrary"),
        name="nsa_merge",
    )(x, mod, g, o_cmp, o_slc, o_win, gates, w_out)


def _nsa_layer(xp, xs, mod_p, mod_s, g, n_seq, cache_cmp, cache_slc, cache_win, page_table,
               w_in, w1, w2, pe, w_out, tm):
    n_s = xs.shape[0]
    t = xp.shape[0] // n_seq
    page_size = cache_cmp.shape[1]
    past = page_table.shape[1] * page_size
    assert t % NSA_BLOCK == 0 and past % NSA_BLOCK == 0 and page_size % NSA_BLOCK == 0
    w_in_b, w_out_b = w_in.astype(BF16), w_out.astype(BF16)
    w1_b, w2_b = w1.astype(BF16), w2.astype(BF16)
    pe_rows = _pe_rows(pe)

    q, kc, _, _, gates, ksb, kwb, kc_t, ks_t, kw_t = _nsa_proj(xp, mod_p, g, w_in_b, tm, n_seq)
    cmp_p = _compress_prompt(kc, pe_rows, w1_b, w2_b, min(256, kc.shape[0] // NSA_BLOCK))
    o_cmp, sel = _cmpattn_prompt(q, cmp_p, n_seq, 128)
    o_slc, o_win = _attn_prompt(q, sel, ksb, kwb, n_seq)
    xp = _nsa_merge(xp, mod_p, g, o_cmp, o_slc, o_win, gates, w_out_b, tm)

    q_s, kc_s, ks_s, kw_s, gates_s, _, _ = _nsa_proj(xs, mod_s, g, w_in_b, n_s)
    cmp_s = _compress_sample(page_table, cache_cmp, pe.astype(F32), w1_b, w2_b, min(256, page_table.size))
    n_cand = -(-(past + 1) // NSA_BLOCK)
    o_cmp_s, sel_idx = _cmpattn_sample(q_s, cmp_s, past, n_cand)
    o_slc_s, o_win_s = _attn_sample(page_table, sel_idx, q_s, ks_s, kw_s, cache_win, cache_slc, past)
    xs = _nsa_merge(xs, mod_s, g, o_cmp_s.reshape(n_s, -1), o_slc_s.reshape(n_s, -1), o_win_s.reshape(n_s, -1),
                    gates_s, w_out_b, n_s)
    return xp, xs, (kc_t, ks_t, kw_t), (kc_s, ks_s, kw_s)


def kernel(x_prompt, x_sample, cache_nsa_cmp, cache_nsa_slc, cache_nsa_win, state_ssm, page_table, c_prompt, c_sample, w_mod, b_mod, norm_g, ffn_w_gate, ffn_w_up, ffn_w_down, gmlp_w_in, gmlp_b_in, gmlp_ln_g, gmlp_ln_b, gmlp_w_s, gmlp_b_s, gmlp_w_out, nsa_w_in, nsa_w_cmp1, nsa_w_cmp2, nsa_pe_cmp, nsa_w_out, ssm_lambda_re, ssm_lambda_im, ssm_b_re, ssm_b_im, ssm_c_re, ssm_c_im, ssm_d, ssm_log_step, ssm_w_glu1, ssm_b_glu1, ssm_w_glu2, ssm_b_glu2):
    n_seq, t, d = x_prompt.shape
    n_s, t_s, _ = x_sample.shape
    assert t_s == 1
    depth = w_mod.shape[0]
    tm = 512 if t % 512 == 0 else 256
    kv_shape = (2, NSA_KV_HEADS, NSA_HEAD_DIM)

    xp = x_prompt.reshape(n_seq * t, d)
    xs = x_sample.reshape(n_s, d)
    m_all = _adaln(jnp.concatenate([c_prompt, c_sample], axis=0), w_mod, b_mod)
    mods_p = m_all[:, :n_seq].reshape(depth, n_seq, 6, 1, d).transpose(0, 2, 1, 3, 4)
    mods_s = m_all[:, n_seq:].reshape(depth, n_s, 6, d).transpose(0, 2, 1, 3)[:, :, None]

    cmp_p, cmp_s, slc_p, slc_s, win_p, win_s, ssm_p, ssm_s, gv_s = [], [], [], [], [], [], [], [], []
    for i in range(depth):
        j = i // N_MIXERS
        mp, ms, g = mods_p[i], mods_s[i], norm_g[i]
        if i % N_MIXERS == 0:
            gw = (gmlp_w_in[j].astype(BF16), gmlp_b_in[j], gmlp_ln_g[j], gmlp_ln_b[j], gmlp_w_s[j], gmlp_b_s[j],
                  gmlp_w_out[j].astype(BF16))
            assert t % GMLP_CHUNK == 0 and gmlp_w_s.shape[2] == GMLP_CHUNK
            xp = _gmlp_prompt(xp, mp, g, *gw, tm)
            xs, v_new = _gmlp_sample(xs, ms, g, *gw)
            gv_s.append(v_new.reshape(n_s, 1, -1))
        elif i % N_MIXERS == 1:
            xp, xs, kv_p, kv_s = _nsa_layer(xp, xs, mp, ms, g, n_seq, cache_nsa_cmp[j], cache_nsa_slc[j],
                                            cache_nsa_win[j], page_table, nsa_w_in[j], nsa_w_cmp1[j],
                                            nsa_w_cmp2[j], nsa_pe_cmp[j], nsa_w_out[j], tm)
            rows_last = lambda a: jnp.transpose(a.reshape((n_seq,) + kv_shape + (a.shape[-1],)), (0, 4, 1, 2, 3))
            cmp_p.append(rows_last(kv_p[0]))
            slc_p.append(rows_last(kv_p[1]))
            win_p.append(rows_last(kv_p[2][:, :, t - min(NSA_WINDOW, t):]))
            cmp_s.append(kv_s[0].reshape((n_s, 1) + kv_shape))
            slc_s.append(kv_s[1].reshape((n_s, 1) + kv_shape))
            past = page_table.shape[1] * cache_nsa_cmp.shape[2]
            win = jnp.concatenate([cache_nsa_win[j], kv_s[2].reshape((n_s, 1) + kv_shape)], axis=1)
            win_s.append(win[:, win.shape[1] - min(NSA_WINDOW, past + 1):])
        else:
            assert t % SSM_CHUNK == 0
            tables, seg_tables = _ssm_tables(ssm_lambda_re[j], ssm_lambda_im[j], ssm_b_re[j], ssm_b_im[j],
                                             ssm_c_re[j], ssm_c_im[j], ssm_log_step[j], SSM_CHUNK // SSM_SEGMENTS)
            glu = (ssm_d[j], ssm_w_glu1[j].astype(BF16), ssm_b_glu1[j], ssm_w_glu2[j].astype(BF16), ssm_b_glu2[j])
            n_grp, n_st = ssm_lambda_re.shape[1:]
            xp, sr, si = _ssm_prompt(xp, mp, g, tables, seg_tables, *glu, n_seq, SSM_CHUNK)
            ssm_p.append(jnp.stack([sr.reshape(n_seq, n_grp, n_st), si.reshape(n_seq, n_grp, n_st)], axis=-1))
            h0 = state_ssm[j].reshape(n_s, n_grp * n_st, 2)
            xs, sr, si = _ssm_sample(xs, ms, g, tables, *glu, h0[..., 0], h0[..., 1])
            ssm_s.append(jnp.stack([sr.reshape(n_s, n_grp, n_st), si.reshape(n_s, n_grp, n_st)], axis=-1))
        ffn_w = (ffn_w_gate[i].astype(BF16), ffn_w_up[i].astype(BF16), ffn_w_down[i].astype(BF16))
        xp = _ffn(xp, mp, g, *ffn_w, tm)
        xs = _ffn(xs, ms, g, *ffn_w, n_s)
    return (xp.reshape(n_seq, t, d), xs.reshape(n_s, 1, d), jnp.stack(cmp_p), jnp.stack(cmp_s), jnp.stack(slc_p),
            jnp.stack(slc_s), jnp.stack(win_p), jnp.stack(win_s), jnp.stack(ssm_p), jnp.stack(ssm_s), jnp.stack(gv_s))
```

```python
import functools
import math

import jax
import jax.numpy as jnp
from jax import lax
from jax.experimental import pallas as pl
from jax.experimental.pallas import tpu as pltpu

F32 = jnp.float32
BF16 = jnp.bfloat16

RMS_EPS = 1.0e-6
LN_EPS = 1.0e-5

V7X_LANES = 128
V7X_VMEM_BYTES = 64 * 1024 * 1024
VMEM_LIMIT_BYTES = V7X_VMEM_BYTES - 8 * 1024 * 1024

N_MIXERS = 3
GMLP_GROUPS = 8
GMLP_CHUNK = 128
NSA_HEAD_DIM = 64
NSA_KV_HEADS = 4
NSA_BLOCK = 64
NSA_TOPK = 16
NSA_WINDOW = 512
ATTN_Q_TILE = 512
ATTN_Q_SUB = 128
ATTN_KV_TILE = 512
SEL_FORCE = 1.0e4
SEL_MASKED = -1.0
SSM_GROUP_WIDTH = 16
SSM_STATE = 64
SSM_SEGMENTS = 8
SSM_CHUNK = 256
NEG_BIG = -1.0e30


def _params(*sem):
    return pltpu.CompilerParams(dimension_semantics=sem, vmem_limit_bytes=VMEM_LIMIT_BYTES)


def _dot(a, b):
    return jnp.dot(a.astype(BF16), b.astype(BF16), preferred_element_type=F32)


def _dot_nt(a, b):
    return lax.dot_general(a.astype(BF16), b.astype(BF16), (((1,), (1,)), ((), ())),
                           preferred_element_type=F32)


def _rms(x, g):
    return x * lax.rsqrt(jnp.mean(x * x, axis=-1, keepdims=True) + RMS_EPS) * g


def _modulate(x, g, shift, scale):
    return _rms(x, g) * (1.0 + scale) + shift


def _const_spec(a, n_grid=1, single=False):
    nd = a.ndim
    idx = {1: lambda i: (0,) * nd, 2: lambda i, j: (0,) * nd, 3: lambda i, j, k: (0,) * nd}[n_grid]
    if single:
        return pl.BlockSpec(a.shape, idx, pipeline_mode=pl.Buffered(1))
    return pl.BlockSpec(a.shape, idx)


def _mod_spec(mod, n_tiles):
    _, n_seq, rows, d = mod.shape
    tiles_per_seq = n_tiles // n_seq
    return pl.BlockSpec((6, None, rows, d), lambda i: (0, i // tiles_per_seq, 0, 0))


def _adaln_body(c_ref, w_ref, b_ref, o_ref):
    c = c_ref[...]
    o_ref[...] = _dot(c * jax.nn.sigmoid(c), w_ref[...]) + b_ref[...]


def _adaln(c_all, w_mod, b_mod):
    depth, d, d6 = w_mod.shape
    m = c_all.shape[0]
    tn = 2048
    return pl.pallas_call(
        _adaln_body,
        grid=(depth, d6 // tn),
        in_specs=[pl.BlockSpec((m, d), lambda l, j: (0, 0)),
                  pl.BlockSpec((None, d, tn), lambda l, j: (l, 0, j)),
                  pl.BlockSpec((None, 1, tn), lambda l, j: (l, 0, j))],
        out_specs=pl.BlockSpec((None, m, tn), lambda l, j: (l, 0, j)),
        out_shape=jax.ShapeDtypeStruct((depth, m, d6), F32),
        compiler_params=_params("arbitrary", "arbitrary"),
        name="adaln",
    )(c_all, w_mod, b_mod.reshape(depth, 1, d6))


def _ffn_body(x_ref, m_ref, g_ref, wg_ref, wu_ref, wd_ref, o_ref, *, n_chunks):
    x = x_ref[...]
    h = _modulate(x, g_ref[2:3], m_ref[3], m_ref[4]).astype(BF16)
    fc = wg_ref.shape[1] // n_chunks
    acc = None
    for c in range(n_chunks):
        a = _dot(h, wg_ref[:, c * fc:(c + 1) * fc])
        b = _dot(h, wu_ref[:, c * fc:(c + 1) * fc])
        y = _dot(a * jax.nn.sigmoid(a) * b, wd_ref[c * fc:(c + 1) * fc, :])
        acc = y if acc is None else acc + y
    o_ref[...] = x + m_ref[5] * _rms(acc, g_ref[3:4])


def _ffn(x, mod, g, wg, wu, wd, tm):
    n, d = x.shape
    n_tiles = n // tm
    return pl.pallas_call(
        functools.partial(_ffn_body, n_chunks=2),
        grid=(n_tiles,),
        in_specs=[pl.BlockSpec((tm, d), lambda i: (i, 0)), _mod_spec(mod, n_tiles), _const_spec(g),
                  _const_spec(wg, single=True), _const_spec(wu, single=True), _const_spec(wd, single=True)],
        out_specs=pl.BlockSpec((tm, d), lambda i: (i, 0)),
        out_shape=jax.ShapeDtypeStruct((n, d), F32),
        compiler_params=_params("arbitrary"),
        name="ffn",
    )(x, mod, g, wg, wu, wd)


def _gmlp_front(x_ref, m_ref, g_ref, win_ref, bin_ref, lng_ref, lnb_ref):
    x = x_ref[...]
    h = _modulate(x, g_ref[0:1], m_ref[0], m_ref[1])
    z = jax.nn.gelu(_dot(h, win_ref[...]) + bin_ref[...])
    half = z.shape[1] // 2
    u, v = z[:, :half], z[:, half:]
    mu = jnp.mean(v, axis=-1, keepdims=True)
    var = jnp.mean(jnp.square(v - mu), axis=-1, keepdims=True)
    v = (v - mu) * lax.rsqrt(var + LN_EPS) * lng_ref[...] + lnb_ref[...]
    return x, u, v


def _gmlp_prompt_body(x_ref, m_ref, g_ref, win_ref, bin_ref, lng_ref, lnb_ref, ws_ref, bs_ref, wout_ref,
                      o_ref, um_ref):
    x, u, v = _gmlp_front(x_ref, m_ref, g_ref, win_ref, bin_ref, lng_ref, lnb_ref)
    vb = v.astype(BF16)
    n_groups, chunk, _ = ws_ref.shape
    gw = v.shape[1] // n_groups
    causal = (lax.broadcasted_iota(jnp.int32, (chunk, chunk), 0)
              >= lax.broadcasted_iota(jnp.int32, (chunk, chunk), 1))
    for grp in range(n_groups):
        w = jnp.where(causal, ws_ref[grp], 0.0).astype(BF16)
        cols = slice(grp * gw, (grp + 1) * gw)
        for k in range(x.shape[0] // chunk):
            rows = slice(k * chunk, (k + 1) * chunk)
            mixed = _dot(w, vb[rows, cols]) + bs_ref[:, grp:grp + 1]
            um_ref[rows, cols] = (u[rows, cols] * mixed).astype(BF16)
    y = _dot(um_ref[...], wout_ref[...])
    o_ref[...] = x + m_ref[2] * _rms(y, g_ref[1:2])


def _gmlp_sample_body(x_ref, m_ref, g_ref, win_ref, bin_ref, lng_ref, lnb_ref, ws_ref, bs_ref, wout_ref,
                      o_ref, v_ref):
    x, u, v = _gmlp_front(x_ref, m_ref, g_ref, win_ref, bin_ref, lng_ref, lnb_ref)
    v_ref[...] = v
    y = _dot(u * (ws_ref[...] * v + bs_ref[...]), wout_ref[...])
    o_ref[...] = x + m_ref[2] * _rms(y, g_ref[1:2])


def _gmlp_prompt(x, mod, g, w_in, b_in, ln_g, ln_b, w_s, b_s, w_out, tm):
    n, d = x.shape
    n_tiles = n // tm
    half = w_out.shape[0]
    args = (x, mod, g, w_in, b_in[None], ln_g[None], ln_b[None], w_s, b_s.T, w_out)
    return pl.pallas_call(
        _gmlp_prompt_body,
        grid=(n_tiles,),
        in_specs=[pl.BlockSpec((tm, d), lambda i: (i, 0)), _mod_spec(mod, n_tiles)]
        + [_const_spec(a) for a in args[2:]],
        out_specs=pl.BlockSpec((tm, d), lambda i: (i, 0)),
        out_shape=jax.ShapeDtypeStruct((n, d), F32),
        scratch_shapes=[pltpu.VMEM((tm, half), BF16)],
        compiler_params=_params("arbitrary"),
        name="gmlp_prompt",
    )(*args)


def _gmlp_sample(x, mod, g, w_in, b_in, ln_g, ln_b, w_s, b_s, w_out):
    n, d = x.shape
    half = w_out.shape[0]
    gw = half // w_s.shape[0]
    args = (x, mod, g, w_in, b_in[None], ln_g[None], ln_b[None],
            jnp.repeat(w_s[:, 0, 0], gw)[None], jnp.repeat(b_s[:, 0], gw)[None], w_out)
    return pl.pallas_call(
        _gmlp_sample_body,
        grid=(1,),
        in_specs=[pl.BlockSpec((n, d), lambda i: (0, 0)), _mod_spec(mod, 1)]
        + [_const_spec(a) for a in args[2:]],
        out_specs=[pl.BlockSpec((n, d), lambda i: (0, 0)), pl.BlockSpec((n, half), lambda i: (0, 0))],
        out_shape=[jax.ShapeDtypeStruct((n, d), F32), jax.ShapeDtypeStruct((n, half), F32)],
        compiler_params=_params("arbitrary"),
        name="gmlp_sample",
    )(*args)


def _cmul(ar, ai, br, bi):
    return ar * br - ai * bi, ar * bi + ai * br


def _powers(base, n):
    p = [base]
    for k in range(2, n + 1):
        p.append(_cmul(*p[k // 2 - 1], *p[k - k // 2 - 1]))
    return p


def _ssm_prep_body(lr_ref, li_ref, ls_ref, br_ref, bi_ref, pwr_ref, pwi_ref, par_ref, pai_ref, bbr_ref, bbi_ref):
    lr, li = lr_ref[...], li_ref[...]
    dt = jnp.exp(ls_ref[...])
    mag = jnp.exp(lr * dt)
    ab_re, ab_im = mag * jnp.cos(li * dt), mag * jnp.sin(li * dt)
    den = lr * lr + li * li
    f_re = ((ab_re - 1.0) * lr + ab_im * li) / den
    f_im = (ab_im * lr - (ab_re - 1.0) * li) / den
    bbr_ref[...] = f_re[:, None, :] * br_ref[...] - f_im[:, None, :] * bi_ref[...]
    bbi_ref[...] = f_re[:, None, :] * bi_ref[...] + f_im[:, None, :] * br_ref[...]
    n_steps = pwr_ref.shape[0]
    p = _powers((ab_re, ab_im), n_steps)
    for n in range(n_steps):
        pwr_ref[n] = p[n][0]
        pwi_ref[n] = p[n][1]
    a = _powers(p[n_steps - 1], SSM_SEGMENTS)
    par_ref[0] = jnp.ones_like(ab_re)
    pai_ref[0] = jnp.zeros_like(ab_re)
    for k in range(SSM_SEGMENTS):
        par_ref[k + 1] = a[k][0]
        pai_ref[k + 1] = a[k][1]


def _ssm_prep(lam_re, lam_im, log_step, b_re, b_im, n_steps):
    g, p = lam_re.shape
    w = b_re.shape[2]
    args = (lam_re, lam_im, log_step[:, None], jnp.swapaxes(b_re, 1, 2), jnp.swapaxes(b_im, 1, 2))
    return pl.pallas_call(
        _ssm_prep_body,
        out_shape=[jax.ShapeDtypeStruct((n_steps, g, p), F32)] * 2
        + [jax.ShapeDtypeStruct((SSM_SEGMENTS + 1, g, p), F32)] * 2
        + [jax.ShapeDtypeStruct((g, w, p), F32)] * 2,
        name="ssm_prep",
    )(*args)


def _ssm_input(x_ref, m_ref, g_ref, bbr_ref, bbi_ref, xr_ref, xi_ref):
    x = x_ref[...]
    u = _modulate(x, g_ref[0:1], m_ref[0], m_ref[1])
    ub = u.astype(BF16)
    n_kb, kin, kout = bbr_ref.shape
    for kb in range(n_kb):
        xr_ref[:, kb * kout:(kb + 1) * kout] = _dot(ub[:, kb * kin:(kb + 1) * kin], bbr_ref[kb])
        xi_ref[:, kb * kout:(kb + 1) * kout] = _dot(ub[:, kb * kin:(kb + 1) * kin], bbi_ref[kb])
    return x, u


def _ssm_readout(xr_ref, xi_ref, cr_ref, ci_ref):
    n_kb, kin, _ = cr_ref.shape
    return [_dot(xr_ref[:, kb * kin:(kb + 1) * kin], cr_ref[kb]) - _dot(xi_ref[:, kb * kin:(kb + 1) * kin], ci_ref[kb])
            for kb in range(n_kb)]


def _ssm_output(x, u, y, m_ref, g_ref, d_ref, w1_ref, b1_ref, w2_ref, b2_ref, o_ref):
    gl = jax.nn.gelu(y + d_ref[...] * u)
    out = (_dot(gl, w1_ref[...]) + b1_ref[...]) * jax.nn.sigmoid(_dot(gl, w2_ref[...]) + b2_ref[...])
    o_ref[...] = x + m_ref[2] * _rms(out, g_ref[1:2])


def _ssm_prompt_body(x_ref, m_ref, g_ref, bbr_ref, bbi_ref, pwr_ref, pwi_ref, cr_ref, ci_ref, d_ref,
                     w1_ref, b1_ref, w2_ref, b2_ref, par_ref, pai_ref, o_ref, sr_ref, si_ref,
                     perm_ref, xr_ref, xi_ref, car_ref, cai_ref, *, lane_block):
    n_seg = SSM_SEGMENTS

    @pl.when(pl.program_id(1) == 0)
    def _():
        car_ref[...] = jnp.zeros_like(car_ref)
        cai_ref[...] = jnp.zeros_like(cai_ref)

    rows, n_state = xr_ref.shape
    n_steps = rows // n_seg
    n_tiles, _, lanes = perm_ref.shape
    x = x_ref[...]
    u = _modulate(x, g_ref[0:1], m_ref[0], m_ref[1])
    for c in range(n_tiles):
        perm_ref[c] = u[:, c * lanes:(c + 1) * lanes]
    ub = jnp.concatenate(
        [jnp.concatenate([perm_ref[c, pl.ds(s, n_seg, stride=n_steps), :] for s in range(n_steps)], axis=0)
         for c in range(n_tiles)], axis=1).astype(BF16)
    n_kb, kin, kout = bbr_ref.shape
    for kb in range(n_kb):
        xr_ref[:, kb * kout:(kb + 1) * kout] = _dot(ub[:, kb * kin:(kb + 1) * kin], bbr_ref[kb])
        xi_ref[:, kb * kout:(kb + 1) * kout] = _dot(ub[:, kb * kin:(kb + 1) * kin], bbi_ref[kb])

    row = lax.broadcasted_iota(jnp.int32, (n_seg, lane_block), 0)

    def shifted(v, s):
        return jnp.where(row >= s, pltpu.roll(v, s, axis=0), 0.0)

    for cb in range(n_state // lane_block):
        cols = slice(cb * lane_block, (cb + 1) * lane_block)
        ar, ai = pwr_ref[0:1, cols], pwi_ref[0:1, cols]
        vr = vi = jnp.zeros((n_seg, lane_block), F32)
        for s in range(n_steps):
            dr, di = _cmul(ar, ai, vr, vi)
            vr = xr_ref[s * n_seg:(s + 1) * n_seg, cols] + dr
            vi = xi_ref[s * n_seg:(s + 1) * n_seg, cols] + di
            xr_ref[s * n_seg:(s + 1) * n_seg, cols] = vr
            xi_ref[s * n_seg:(s + 1) * n_seg, cols] = vi
        for s in (1, 2, 4):
            dr, di = _cmul(par_ref[s:s + 1, cols], pai_ref[s:s + 1, cols], shifted(vr, s), shifted(vi, s))
            vr, vi = vr + dr, vi + di
        in_r, in_i = car_ref[:, cols], cai_ref[:, cols]
        dr, di = _cmul(par_ref[0:n_seg, cols], pai_ref[0:n_seg, cols], in_r, in_i)
        seg_r, seg_i = shifted(vr, 1) + dr, shifted(vi, 1) + di
        dr, di = _cmul(par_ref[n_seg:n_seg + 1, cols], pai_ref[n_seg:n_seg + 1, cols], in_r, in_i)
        car_ref[:, cols] = jnp.broadcast_to(vr[n_seg - 1:n_seg], vr.shape) + dr
        cai_ref[:, cols] = jnp.broadcast_to(vi[n_seg - 1:n_seg], vi.shape) + di
        for s in range(n_steps):
            dr, di = _cmul(pwr_ref[s:s + 1, cols], pwi_ref[s:s + 1, cols], seg_r, seg_i)
            xr_ref[s * n_seg:(s + 1) * n_seg, cols] += dr
            xi_ref[s * n_seg:(s + 1) * n_seg, cols] += di
    sr_ref[...] = car_ref[0:1, :]
    si_ref[...] = cai_ref[0:1, :]
    y_perm = jnp.concatenate(_ssm_readout(xr_ref, xi_ref, cr_ref, ci_ref), axis=1)
    for c in range(n_tiles):
        perm_ref[c] = y_perm[:, c * lanes:(c + 1) * lanes]
    y = jnp.concatenate(
        [jnp.concatenate([perm_ref[c, pl.ds(seg, n_steps, stride=n_seg), :] for seg in range(n_seg)], axis=0)
         for c in range(n_tiles)], axis=1)
    _ssm_output(x, u, y, m_ref, g_ref, d_ref, w1_ref, b1_ref, w2_ref, b2_ref, o_ref)


def _ssm_sample_body(x_ref, m_ref, g_ref, bbr_ref, bbi_ref, pwr_ref, pwi_ref, cr_ref, ci_ref, d_ref,
                     w1_ref, b1_ref, w2_ref, b2_ref, hr_ref, hi_ref, o_ref, sr_ref, si_ref,
                     xr_ref, xi_ref):
    x, u = _ssm_input(x_ref, m_ref, g_ref, bbr_ref, bbi_ref, xr_ref, xi_ref)
    dr, di = _cmul(pwr_ref[0:1, :], pwi_ref[0:1, :], hr_ref[...], hi_ref[...])
    xr_ref[...] = xr_ref[...] + dr
    xi_ref[...] = xi_ref[...] + di
    sr_ref[...] = xr_ref[...]
    si_ref[...] = xi_ref[...]
    y = jnp.concatenate(_ssm_readout(xr_ref, xi_ref, cr_ref, ci_ref), axis=1)
    _ssm_output(x, u, y, m_ref, g_ref, d_ref, w1_ref, b1_ref, w2_ref, b2_ref, o_ref)


def _ssm_tables(lam_re, lam_im, b_re, b_im, c_re, c_im, log_step, n_steps):
    g, p = lam_re.shape
    w = b_re.shape[2]
    pwr, pwi, par, pai, bbr, bbi = _ssm_prep(lam_re, lam_im, log_step, b_re, b_im, n_steps)
    gb = 256 // w
    eye = jnp.eye(gb, dtype=F32)

    def bd_in(a):
        return jnp.einsum("kgip,gh->kgihp", a.reshape(g // gb, gb, w, p), eye).reshape(g // gb, gb * w, gb * p)

    def bd_out(a):
        return jnp.einsum("kgip,gh->kgphi", a.reshape(g // gb, gb, w, p), eye).reshape(g // gb, gb * p, gb * w)

    tables = (bd_in(bbr).astype(BF16), bd_in(bbi).astype(BF16), pwr.reshape(n_steps, g * p),
              pwi.reshape(n_steps, g * p), bd_out(c_re).astype(BF16), bd_out(c_im).astype(BF16))
    return tables, (par.reshape(-1, g * p), pai.reshape(-1, g * p))


def _ssm_prompt(x, mod, g, tables, seg_tables, d_skip, w1, b1, w2, b2, n_seq, tl):
    n, d = x.shape
    t = n // n_seq
    n_state = tables[2].shape[1]
    assert tl == SSM_SEGMENTS * tables[2].shape[0]
    consts = tables + (d_skip[None], w1, b1[None], w2, b2[None]) + seg_tables
    row_spec = pl.BlockSpec((tl, d), lambda b, c: (b * (t // tl) + c, 0))
    st_spec = pl.BlockSpec((None, 1, n_state), lambda b, c: (b, 0, 0))
    return pl.pallas_call(
        functools.partial(_ssm_prompt_body, lane_block=1024),
        grid=(n_seq, t // tl),
        in_specs=[row_spec, pl.BlockSpec((6, None, 1, d), lambda b, c: (0, b, 0, 0)), _const_spec(g, 2)]
        + [_const_spec(a, 2) for a in consts],
        out_specs=[row_spec, st_spec, st_spec],
        out_shape=[jax.ShapeDtypeStruct((n, d), F32), jax.ShapeDtypeStruct((n_seq, 1, n_state), F32),
                   jax.ShapeDtypeStruct((n_seq, 1, n_state), F32)],
        scratch_shapes=[pltpu.VMEM((d // V7X_LANES, tl, V7X_LANES), F32),
                        pltpu.VMEM((tl, n_state), F32), pltpu.VMEM((tl, n_state), F32),
                        pltpu.VMEM((SSM_SEGMENTS, n_state), F32), pltpu.VMEM((SSM_SEGMENTS, n_state), F32)],
        compiler_params=_params("arbitrary", "arbitrary"),
        name="ssm_prompt",
    )(x, mod, g, *consts)


def _ssm_sample(x, mod, g, tables, d_skip, w1, b1, w2, b2, h_re, h_im):
    n, d = x.shape
    n_state = tables[2].shape[1]
    consts = tables + (d_skip[None], w1, b1[None], w2, b2[None], h_re, h_im)
    full = pl.BlockSpec((n, d), lambda i: (0, 0))
    st = pl.BlockSpec((n, n_state), lambda i: (0, 0))
    return pl.pallas_call(
        _ssm_sample_body,
        grid=(1,),
        in_specs=[full, _mod_spec(mod, 1), _const_spec(g)] + [_const_spec(a) for a in consts],
        out_specs=[full, st, st],
        out_shape=[jax.ShapeDtypeStruct((n, d), F32), jax.ShapeDtypeStruct((n, n_state), F32),
                   jax.ShapeDtypeStruct((n, n_state), F32)],
        scratch_shapes=[pltpu.VMEM((n, n_state), F32), pltpu.VMEM((n, n_state), F32)],
        compiler_params=_params("arbitrary"),
        name="ssm_sample",
    )(x, mod, g, *consts)


def _nsa_proj_body(x_ref, m_ref, g_ref, w_ref, q_ref, kc_ref, ks_ref, kw_ref, gt_ref, ksb_ref, kwb_ref, *t_refs):
    h = _modulate(x_ref[...], g_ref[0:1], m_ref[0], m_ref[1]).astype(BF16)
    qc, kc = q_ref.shape[1], kc_ref.shape[1]
    q_ref[...] = (_dot(h, w_ref[:, :qc]) * (NSA_HEAD_DIM ** -0.5)).astype(BF16)
    kc_ref[...] = _dot(h, w_ref[:, qc:qc + kc])
    ks = _dot(h, w_ref[:, qc + kc:qc + 2 * kc])
    kw = _dot(h, w_ref[:, qc + 2 * kc:qc + 3 * kc])
    ks_ref[...] = ks
    kw_ref[...] = kw
    ksb_ref[...] = ks.astype(BF16)
    kwb_ref[...] = kw.astype(BF16)
    gt_ref[...] = jax.nn.sigmoid(_dot(h, w_ref[:, qc + 3 * kc:]))
    for t_ref, rows in zip(t_refs, (kc_ref[...], ks, kw)):
        t_ref[...] = rows.T


def _nsa_proj(x, mod, g, w_in, tm, n_seq=None):
    n, d = x.shape
    n_tiles = n // tm
    kc = 2 * NSA_KV_HEADS * NSA_HEAD_DIM
    ng = (w_in.shape[1] - d - 3 * kc)
    widths = [(d, BF16), (kc, F32), (kc, F32), (kc, F32), (ng, F32), (kc, BF16), (kc, BF16)]
    out_specs = [pl.BlockSpec((tm, w), lambda i: (i, 0)) for w, _ in widths]
    out_shape = [jax.ShapeDtypeStruct((n, w), dt) for w, dt in widths]
    if n_seq is not None:
        tps = n_tiles // n_seq
        out_specs += [pl.BlockSpec((None, kc, tm), lambda i: (i // tps, 0, i % tps))] * 3
        out_shape += [jax.ShapeDtypeStruct((n_seq, kc, n // n_seq), F32)] * 3
    return pl.pallas_call(
        _nsa_proj_body,
        grid=(n_tiles,),
        in_specs=[pl.BlockSpec((tm, d), lambda i: (i, 0)), _mod_spec(mod, n_tiles), _const_spec(g),
                  _const_spec(w_in)],
        out_specs=out_specs,
        out_shape=out_shape,
        compiler_params=_params("arbitrary"),
        name="nsa_proj",
    )(x, mod, g, w_in)


def _compress_step(x_of, ls, pe_ref, w1_ref, w2_ref, o_ref, acc_ref, n_l):
    dh = NSA_HEAD_DIM
    hid = w1_ref.shape[3]
    n_zg = 2 * NSA_KV_HEADS

    @pl.when(ls == 0)
    def _():
        acc_ref[...] = jnp.zeros_like(acc_ref)

    for ll in range(n_l):
        l = ls * n_l + ll
        xb = (x_of(ll) + pe_ref[pl.ds(l, 1), :]).astype(BF16)
        for zg in range(n_zg):
            acc_ref[:, zg * hid:(zg + 1) * hid] += _dot(xb[:, zg * dh:(zg + 1) * dh], w1_ref[zg // NSA_KV_HEADS, l])

    @pl.when(ls == pl.num_programs(1) - 1)
    def _():
        a = acc_ref[...]
        hidv = (a * jax.nn.sigmoid(a)).astype(BF16)
        for zg in range(n_zg):
            o_ref[:, zg * dh:(zg + 1) * dh] = _dot(hidv[:, zg * hid:(zg + 1) * hid], w2_ref[zg // NSA_KV_HEADS])


def _compress_prompt_body(x_ref, pe_ref, w1_ref, w2_ref, o_ref, acc_ref, *, n_l):
    _compress_step(lambda ll: x_ref[:, ll, :], pl.program_id(1), pe_ref, w1_ref, w2_ref, o_ref, acc_ref, n_l)


def _compress_sample_body(pt_ref, cache_ref, pe_ref, w1_ref, w2_ref, o_ref, buf_ref, sem, acc_ref, *,
                          pages_per_step, d_tiles):
    pg, ds = pl.program_id(0), pl.program_id(1)
    n_ds = pl.num_programs(1)
    n_zg = 2 * NSA_KV_HEADS
    dh = NSA_HEAD_DIM
    per_page = o_ref.shape[0]
    hid = w1_ref.shape[3] // per_page
    step = pg * n_ds + ds
    slot = step % 2

    def page_copy(p, at_step, at_slot):
        page = pt_ref[(at_step // n_ds) * pages_per_step + p]
        return pltpu.make_async_copy(cache_ref.at[page, :, pl.ds((at_step % n_ds) * d_tiles, d_tiles)],
                                     buf_ref.at[at_slot, :, :, pl.ds(p * 8, 8), :], sem.at[at_slot])

    def start_all(at_step, at_slot):
        def start(p, c):
            page_copy(p, at_step, at_slot).start()
            return c

        lax.fori_loop(0, pages_per_step, start, 0)

    @pl.when(step == 0)
    def _():
        start_all(step, slot)

    @pl.when(step + 1 < pl.num_programs(0) * n_ds)
    def _():
        start_all(step + 1, 1 - slot)

    @pl.when(ds == 0)
    def _():
        acc_ref[...] = jnp.zeros_like(acc_ref)

    def wait(p, c):
        page_copy(p, step, slot).wait()
        return c

    lax.fori_loop(0, pages_per_step, wait, 0)
    for zg in range(n_zg):
        z = zg // NSA_KV_HEADS
        for dt in range(d_tiles):
            for dd in range(8):
                d = dt * 8 + dd
                x = buf_ref[slot, zg, dt, pl.ds(dd, pages_per_step, stride=8), :] + pe_ref[zg, d:d + 1, :]
                acc_ref[zg] += _dot(x, w1_ref[z, d])

    @pl.when(ds == pl.num_programs(1) - 1)
    def _():
        for zg in range(n_zg):
            a = acc_ref[zg]
            hidv = (a * jax.nn.sigmoid(a)).astype(BF16)
            for n in range(per_page):
                o_ref[n, :, zg * dh:(zg + 1) * dh] = _dot(hidv[:, n * hid:(n + 1) * hid], w2_ref[zg // NSA_KV_HEADS])


def _pe_rows(pe):
    blk = pe.shape[1]
    return jnp.broadcast_to(pe.transpose(1, 0, 2)[:, :, None, :], (blk, 2, NSA_KV_HEADS, pe.shape[2])).reshape(blk, -1)


_COMPRESS_ROWS = 16


def _compress_prompt(kc, pe, w1, w2, nbt):
    n, c = kc.shape
    nblk = n // NSA_BLOCK
    n_l = _COMPRESS_ROWS
    x3 = kc.reshape(nblk, NSA_BLOCK, c)
    return pl.pallas_call(
        functools.partial(_compress_prompt_body, n_l=n_l),
        grid=(nblk // nbt, NSA_BLOCK // n_l),
        in_specs=[pl.BlockSpec((nbt, n_l, c), lambda i, l: (i, l, 0)), _const_spec(pe, 2), _const_spec(w1, 2),
                  _const_spec(w2, 2)],
        out_specs=pl.BlockSpec((nbt, c), lambda i, l: (i, 0)),
        out_shape=jax.ShapeDtypeStruct((nblk, c), F32),
        scratch_shapes=[pltpu.VMEM((nbt, 2 * NSA_KV_HEADS * w1.shape[3]), F32)],
        compiler_params=_params("arbitrary", "arbitrary"),
        name="compress_prompt",
    )(x3, pe, w1, w2)


def _rows_on_lanes(cache):
    return jnp.transpose(cache, (0, 2, 3, 4, 1))


def _compress_sample(page_table, cache, pe, w1, w2, pages_per_step):
    n_pool, page = cache.shape[:2]
    dh = NSA_HEAD_DIM
    n_zg = 2 * NSA_KV_HEADS
    c = n_zg * dh
    per_page = page // NSA_BLOCK
    hid = w1.shape[3]
    d_tiles = 2
    n_pages_total = page_table.size
    cache_t = _rows_on_lanes(cache).reshape(n_pool, n_zg, dh // 8, 8, page)
    pe_t = jnp.tile(jnp.repeat(jnp.swapaxes(pe, 1, 2), NSA_KV_HEADS, axis=0), (1, 1, per_page))
    w1_t = jnp.einsum("zlde,nm->zdnlme", w1, jnp.eye(per_page, dtype=w1.dtype)).reshape(2, dh, page, per_page * hid)
    grid_spec = pltpu.PrefetchScalarGridSpec(
        num_scalar_prefetch=1,
        grid=(n_pages_total // pages_per_step, dh // (8 * d_tiles)),
        in_specs=[pl.BlockSpec(memory_space=pl.ANY),
                  pl.BlockSpec((n_zg, 8 * d_tiles, page), lambda i, s, pt: (0, s, 0)),
                  pl.BlockSpec((2, 8 * d_tiles, page, per_page * hid), lambda i, s, pt: (0, s, 0, 0)),
                  pl.BlockSpec(w2.shape, lambda i, s, pt: (0, 0, 0))],
        out_specs=pl.BlockSpec((per_page, pages_per_step, c), lambda i, s, pt: (0, i, 0)),
        scratch_shapes=[pltpu.VMEM((2, n_zg, d_tiles, pages_per_step * 8, page), F32), pltpu.SemaphoreType.DMA((2,)),
                        pltpu.VMEM((n_zg, pages_per_step, per_page * hid), F32)],
    )
    out = pl.pallas_call(
        functools.partial(_compress_sample_body, pages_per_step=pages_per_step, d_tiles=d_tiles),
        grid_spec=grid_spec,
        out_shape=jax.ShapeDtypeStruct((per_page, n_pages_total, c), F32),
        compiler_params=_params("arbitrary", "arbitrary"),
        name="compress_sample",
    )(page_table.reshape(-1), cache_t, pe_t, w1_t, w2)
    return jnp.swapaxes(out, 0, 1).reshape(n_pages_total * per_page, c)


def _stack_heads(q, grp):
    dh = NSA_HEAD_DIM
    rep = q.shape[1] // (NSA_KV_HEADS * dh)
    base = grp * rep * dh
    return jnp.concatenate([q[:, base + r * dh:base + (r + 1) * dh] for r in range(rep)], axis=0)


def _cmp_branch(qs, cmpv, grp, t_row, rep):
    dh = NSA_HEAD_DIM
    kv = NSA_KV_HEADS * dh
    kc = cmpv[:, grp * dh:(grp + 1) * dh]
    vc = cmpv[:, kv + grp * dh:kv + (grp + 1) * dh]
    s = _dot_nt(qs, kc)
    n = lax.broadcasted_iota(jnp.int32, s.shape, 1)
    mask = (n + 1) * NSA_BLOCK <= t_row + 1
    s = jnp.where(mask, s, NEG_BIG)
    e = jnp.where(mask, jnp.exp(s - jnp.max(s, axis=-1, keepdims=True)), 0.0)
    p = e / jnp.maximum(jnp.sum(e, axis=-1, keepdims=True), 1e-30)
    o = _dot(p, vc)
    t = p.shape[0] // rep
    imp = p[0:t]
    for r in range(1, rep):
        imp = imp + p[r * t:(r + 1) * t]
    return o, imp


def _topk_mask(score, axis):
    idx = lax.broadcasted_iota(jnp.int32, score.shape, axis).astype(F32)
    n = float(score.shape[axis])
    x = score
    for _ in range(NSA_TOPK):
        m = jnp.max(x, axis=axis, keepdims=True)
        first = jnp.min(jnp.where(x == m, idx, n), axis=axis, keepdims=True)
        x = jnp.where(idx == first, -jnp.inf, x)
    return (x == -jnp.inf).astype(F32)


def _cmpattn_prompt_body(q_ref, cmp_ref, o_ref, sel_ref):
    tq = q_ref.shape[0]
    nb = cmp_ref.shape[0]
    dh = NSA_HEAD_DIM
    rep = q_ref.shape[1] // (NSA_KV_HEADS * dh)
    t0 = pl.program_id(1) * tq
    q = q_ref[...]
    cmpv = cmp_ref[...]
    t_row = t0 + lax.broadcasted_iota(jnp.int32, (rep * tq, 1), 0) % tq
    blk = lax.broadcasted_iota(jnp.int32, (nb, tq), 0)
    jt = (t0 + lax.broadcasted_iota(jnp.int32, (nb, tq), 1)) // NSA_BLOCK
    forced = (blk == 0) | (blk == jt) | (blk == jt - 1)
    for grp in range(NSA_KV_HEADS):
        o, imp = _cmp_branch(_stack_heads(q, grp), cmpv, grp, t_row, rep)
        for r in range(rep):
            h = grp * rep + r
            o_ref[:, h * dh:(h + 1) * dh] = o[r * tq:(r + 1) * tq]
        score = jnp.where(blk <= jt, jnp.where(forced, SEL_FORCE, imp.T), SEL_MASKED)
        sel = _topk_mask(score, 0) * (score > 0.5 * SEL_MASKED).astype(F32)
        sel_ref[:, grp * nb:(grp + 1) * nb] = jnp.where(sel.T > 0.5, 0.0, NEG_BIG).astype(BF16)


def _cmpattn_prompt(q, cmp, n_seq, tq):
    n, qc = q.shape
    t = n // n_seq
    nb = cmp.shape[0] // n_seq
    return pl.pallas_call(
        _cmpattn_prompt_body,
        grid=(n_seq, t // tq),
        in_specs=[pl.BlockSpec((tq, qc), lambda b, i: (b * (t // tq) + i, 0)),
                  pl.BlockSpec((nb, cmp.shape[1]), lambda b, i: (b, 0))],
        out_specs=[pl.BlockSpec((tq, qc), lambda b, i: (b * (t // tq) + i, 0)),
                   pl.BlockSpec((tq, NSA_KV_HEADS * nb), lambda b, i: (b * (t // tq) + i, 0))],
        out_shape=[jax.ShapeDtypeStruct((n, qc), F32), jax.ShapeDtypeStruct((n, NSA_KV_HEADS * nb), BF16)],
        compiler_params=_params("arbitrary", "arbitrary"),
        name="cmpattn_prompt",
    )(q, cmp)


def _cmpattn_sample_body(q_ref, cmp_ref, o_ref, idx_ref, *, t_pos, n_cand):
    nb = cmp_ref.shape[0]
    dh = NSA_HEAD_DIM
    rep = q_ref.shape[1] // (NSA_KV_HEADS * dh)
    q = q_ref[...]
    cmpv = cmp_ref[...]
    t_row = jnp.full((rep, 1), t_pos, jnp.int32)
    width = idx_ref.shape[1]
    lanes = ((n_cand + 127) // 128) * 128
    blk = lax.broadcasted_iota(jnp.int32, (1, lanes), 1)
    jt = t_pos // NSA_BLOCK
    forced = (blk == 0) | (blk == jt) | (blk == jt - 1)
    col = lax.broadcasted_iota(jnp.int32, (1, width), 1)
    blk_f = blk.astype(F32)
    for grp in range(NSA_KV_HEADS):
        o, imp = _cmp_branch(_stack_heads(q, grp), cmpv, grp, t_row, rep)
        for r in range(rep):
            h = grp * rep + r
            o_ref[:, h * dh:(h + 1) * dh] = o[r:r + 1]
        imp = jnp.concatenate([imp, jnp.zeros((1, lanes - nb), F32)], axis=1)
        score = jnp.where(blk <= jt, jnp.where(forced, SEL_FORCE, imp), SEL_MASKED)
        x = jnp.where(blk < n_cand, score, -jnp.inf)
        row = jnp.full((1, width), -1, jnp.int32)
        for k in range(NSA_TOPK):
            m = jnp.max(x, axis=1, keepdims=True)
            first = jnp.min(jnp.where(x == m, blk_f, float(lanes)), axis=1, keepdims=True)
            chosen = jnp.where(m > 0.5 * SEL_MASKED, first, -1.0).astype(jnp.int32)
            row = jnp.where(col == k, chosen, row)
            x = jnp.where(blk_f == first, -jnp.inf, x)
        idx_ref[grp:grp + 1, :] = row


def _cmpattn_sample(q, cmp, t_pos, n_cand):
    n_seq, qc = q.shape
    nb = cmp.shape[0] // n_seq
    return pl.pallas_call(
        functools.partial(_cmpattn_sample_body, t_pos=t_pos, n_cand=n_cand),
        grid=(n_seq,),
        in_specs=[pl.BlockSpec((None, 1, qc), lambda b: (b, 0, 0)), pl.BlockSpec((nb, cmp.shape[1]), lambda b: (b, 0))],
        out_specs=[pl.BlockSpec((None, 1, qc), lambda b: (b, 0, 0)),
                   pl.BlockSpec((None, NSA_KV_HEADS, 128), lambda b: (b, 0, 0))],
        out_shape=[jax.ShapeDtypeStruct((n_seq, 1, qc), F32), jax.ShapeDtypeStruct((n_seq, NSA_KV_HEADS, 128), jnp.int32)],
        compiler_params=_params("arbitrary"),
        name="cmpattn_sample",
    )(q[:, None, :], cmp)


def _attn_prompt_body(q_ref, sel_ref, ks_ref, kw_ref, oslc_ref, owin_ref, qa_ref, m_ref, acc_ref, *, tk, n_sub):
    tq = q_ref.shape[0]
    ts = tq // n_sub
    dh = NSA_HEAD_DIM
    slab = 2 * dh
    kv = NSA_KV_HEADS * dh
    rep = q_ref.shape[1] // kv
    rows, rows_s = rep * tq, rep * ts
    nb = sel_ref.shape[1] // NSA_KV_HEADS
    t0 = pl.program_id(1) * tq
    lane = lax.broadcasted_iota(jnp.int32, (ts, slab), 1)

    for grp in range(NSA_KV_HEADS):
        off = (grp % 2) * dh
        for sub in range(n_sub):
            tok = slice(sub * ts, (sub + 1) * ts)
            parts = []
            for r in range(rep):
                h = grp * rep + r
                x = q_ref[tok, (h // 2) * slab:(h // 2 + 1) * slab].astype(F32)
                if h % 2 != grp % 2:
                    x = pltpu.roll(x, dh, axis=1)
                parts.append(jnp.where((lane >= off) & (lane < off + dh), x, 0.0))
            rr = slice(sub * rows_s, (sub + 1) * rows_s)
            qa_ref[grp, rr, 0:slab] = jnp.concatenate(parts, axis=0).astype(BF16)
            qa_ref[grp, rr, slab:slab + nb] = jnp.concatenate([sel_ref[tok, grp * nb:(grp + 1) * nb]] * rep, axis=0)
    m_ref[...] = jnp.full(m_ref.shape, 0.1 * NEG_BIG, F32)
    acc_ref[...] = jnp.zeros_like(acc_ref)

    def flash_tile(r0, n_rows, k0, width, bias):
        rr = slice(r0, r0 + n_rows)
        key_blk = (k0 + lax.broadcasted_iota(jnp.int32, (width, nb), 0)) // NSA_BLOCK
        onehot = (key_blk == lax.broadcasted_iota(jnp.int32, (width, nb), 1)).astype(BF16)
        upper_half = lax.broadcasted_iota(jnp.int32, (width, slab), 1) >= dh
        for grp in range(NSA_KV_HEADS):
            pair = (grp // 2) * slab
            k_aug = jnp.concatenate([ks_ref[pl.ds(k0, width), pair:pair + slab], onehot], axis=1)
            s = _dot_nt(qa_ref[grp, rr], k_aug)
            if bias is not None:
                s = (s.reshape(rep, n_rows // rep, width) + bias[None]).reshape(n_rows, width)
            cols = [s[:, c * slab:(c + 1) * slab] for c in range(width // slab)]
            mx = functools.reduce(jnp.maximum, cols)
            m_old = m_ref[grp, rr]
            m_new = jnp.maximum(m_old, jnp.max(mx, axis=-1, keepdims=True))
            alpha = jnp.exp(m_old - m_new)
            p = jnp.concatenate([jnp.exp(c - m_new).astype(BF16) for c in cols], axis=1)
            v = ks_ref[pl.ds(k0, width), kv + pair:kv + pair + slab]
            v = jnp.where(upper_half == (grp % 2 == 1), v, jnp.ones_like(v))
            acc_ref[grp, rr] = alpha * acc_ref[grp, rr] + _dot(p, v)
            m_ref[grp, rr] = m_new

    n_full = t0 // tk

    def full_tile(j, carry):
        flash_tile(0, rows, pl.multiple_of(j * tk, tk), tk, None)
        return carry

    lax.fori_loop(0, n_full, full_tile, 0)

    win_len = NSA_WINDOW + ts
    for sub in range(n_sub):
        ts0 = t0 + sub * ts
        r0 = sub * rows_s
        t_col = ts0 + lax.broadcasted_iota(jnp.int32, (ts, 1), 0)
        j_own = ts0 // tk
        if sub > 0:

            def before_own(j, carry, r0=r0):
                flash_tile(r0, rows_s, pl.multiple_of(j * tk, tk), tk, None)
                return carry

            lax.fori_loop(n_full, j_own, before_own, 0)
        k0 = pl.multiple_of(j_own * tk, tk)
        width = (sub + 1) * ts if tq == tk else tk
        late = k0 + lax.broadcasted_iota(jnp.int32, (1, width), 1) > t_col
        flash_tile(r0, rows_s, k0, width, jnp.where(late, NEG_BIG, 0.0))

        w0 = pl.multiple_of(jnp.maximum(ts0 - NSA_WINDOW, 0), ts)
        wpos = w0 + lax.broadcasted_iota(jnp.int32, (1, win_len), 1)
        win_bias = jnp.where((wpos <= t_col) & (wpos >= t_col - NSA_WINDOW), 0.0, NEG_BIG)
        tok = slice(sub * ts, (sub + 1) * ts)
        for grp in range(NSA_KV_HEADS):
            off = (grp % 2) * dh
            pair = (grp // 2) * slab
            acc = acc_ref[grp, r0:r0 + rows_s]
            o = acc / jnp.maximum(acc[:, dh - off:dh - off + 1], 1e-30)
            for r in range(rep):
                h = grp * rep + r
                oslc_ref[tok, h * dh:(h + 1) * dh] = o[r * ts:(r + 1) * ts, off:off + dh]

            s = _dot_nt(qa_ref[grp, r0:r0 + rows_s, 0:slab], kw_ref[pl.ds(w0, win_len), pair:pair + slab])
            s = s.reshape(rep, ts, win_len) + win_bias[None]
            m = jnp.maximum(jnp.max(s, axis=-1, keepdims=True), 0.1 * NEG_BIG)
            e = jnp.exp(s - m)
            o = _dot(e.reshape(rows_s, win_len), kw_ref[pl.ds(w0, win_len), kv + pair:kv + pair + slab])
            o = o / jnp.maximum(jnp.sum(e, axis=-1, keepdims=True).reshape(rows_s, 1), 1e-30)
            for r in range(rep):
                h = grp * rep + r
                owin_ref[tok, h * dh:(h + 1) * dh] = o[r * ts:(r + 1) * ts, off:off + dh]


def _attn_prompt(q, sel, ksb, kwb, n_seq):
    n, qc = q.shape
    t = n // n_seq
    dh = NSA_HEAD_DIM
    rep = qc // (NSA_KV_HEADS * dh)
    tq, ts, tk = ATTN_Q_TILE, ATTN_Q_SUB, ATTN_KV_TILE
    assert t % tk == 0 and t % tq == 0 and tq % ts == 0 and tk % ts == 0
    assert t >= NSA_WINDOW + ts and NSA_WINDOW % ts == 0
    tile = pl.BlockSpec((tq, qc), lambda b, i: (b * (t // tq) + i, 0))
    seq = pl.BlockSpec((t, ksb.shape[1]), lambda b, i: (b, 0), pipeline_mode=pl.Buffered(1))
    return pl.pallas_call(
        functools.partial(_attn_prompt_body, tk=tk, n_sub=tq // ts),
        grid=(n_seq, t // tq),
        in_specs=[tile, pl.BlockSpec((tq, sel.shape[1]), lambda b, i: (b * (t // tq) + i, 0)), seq, seq],
        out_specs=[tile, tile],
        out_shape=[jax.ShapeDtypeStruct((n, qc), F32), jax.ShapeDtypeStruct((n, qc), F32)],
        scratch_shapes=[pltpu.VMEM((NSA_KV_HEADS, rep * tq, 2 * dh + sel.shape[1] // NSA_KV_HEADS), BF16)]
        + [pltpu.VMEM((NSA_KV_HEADS, rep * tq, 2 * dh), F32)] * 2,
        compiler_params=_params("arbitrary", "arbitrary"),
        name="attn_prompt",
    )(q, sel, ksb, kwb)


def _softmax_with_new_key(s, ok, s_new, new_ok):
    s = jnp.where(ok, s, NEG_BIG)
    s_new = jnp.where(new_ok, s_new, NEG_BIG)
    m = jnp.maximum(jnp.max(s, axis=-1, keepdims=True), s_new)
    e = jnp.where(ok, jnp.exp(s - m), 0.0)
    e_new = jnp.where(new_ok, jnp.exp(s_new - m), 0.0)
    return e, e_new, jnp.maximum(jnp.sum(e, axis=-1, keepdims=True) + e_new, 1e-30)


def _bf16_round(x):
    return x.astype(BF16).astype(F32)


def _attn_sample_body(pt_ref, idx_ref, q_ref, ksn_ref, kwn_ref, win_ref, cache_ref, oslc_ref, owin_ref,
                      kbuf_ref, sem, *, t_pos, nb_past, n_pages):
    b = pl.program_id(0)
    dh = NSA_HEAD_DIM
    kv = NSA_KV_HEADS * dh
    rep = q_ref.shape[1] // kv
    n_sel = NSA_TOPK
    page = cache_ref.shape[4]
    per_page = page // NSA_BLOCK
    q = q_ref[...]

    def sel_index(grp, k):
        return idx_ref[(b * NSA_KV_HEADS + grp) * 128 + k]

    def in_pool(idx):
        return (idx >= 0) & (idx < nb_past)

    def page_copy(grp, k, idx):
        phys = pt_ref[b * n_pages + jnp.minimum(idx // per_page, n_pages - 1)]
        return pltpu.make_async_copy(cache_ref.at[phys, :, grp], kbuf_ref.at[grp, :, :, pl.ds(k * page, page)], sem)

    for grp in range(NSA_KV_HEADS):
        for k in range(n_sel):
            idx = sel_index(grp, k)

            @pl.when(in_pool(idx))
            def _():
                page_copy(grp, k, idx).start()

            @pl.when(jnp.logical_not(in_pool(idx)))
            def _():
                kbuf_ref[grp, :, :, k * page:(k + 1) * page] = jnp.zeros((2, dh, page), F32)

    for grp in range(NSA_KV_HEADS):
        for k in range(n_sel):
            idx = sel_index(grp, k)

            @pl.when(in_pool(idx))
            def _():
                page_copy(grp, k, idx).wait()

    lane = lax.broadcasted_iota(jnp.int32, (1, n_sel * page), 1)
    wb = win_ref.shape[3]
    wpos = t_pos - wb + lax.broadcasted_iota(jnp.int32, (1, wb), 1)
    win_ok = (wpos <= t_pos) & (wpos >= t_pos - NSA_WINDOW) & (wpos >= 0)
    for grp in range(NSA_KV_HEADS):
        qs = _stack_heads(q, grp)
        qf = qs.astype(F32)
        ok = jnp.zeros((1, n_sel * page), jnp.bool_)
        has_new = False
        for k in range(n_sel):
            idx = sel_index(grp, k)
            row = lane - k * page
            kpos = (idx // per_page) * page + row
            ok = ok | ((lane // page == k) & in_pool(idx) & (row // NSA_BLOCK == idx % per_page) & (kpos <= t_pos))
            has_new = has_new | (idx >= nb_past)
        new_ok = has_new & (nb_past * NSA_BLOCK <= t_pos)
        k_new = _bf16_round(ksn_ref[:, grp * dh:(grp + 1) * dh])
        v_new = _bf16_round(ksn_ref[:, kv + grp * dh:kv + (grp + 1) * dh])
        s_new = jnp.sum(qf * k_new, axis=-1, keepdims=True)
        e, e_new, den = _softmax_with_new_key(_dot(qs, kbuf_ref[grp, 0]), ok, s_new, new_ok)
        o = (_dot_nt(e, kbuf_ref[grp, 1]) + _bf16_round(e_new) * v_new) / den
        for r in range(rep):
            h = grp * rep + r
            oslc_ref[:, h * dh:(h + 1) * dh] = o[r:r + 1]

        k_new = _bf16_round(kwn_ref[:, grp * dh:(grp + 1) * dh])
        v_new = _bf16_round(kwn_ref[:, kv + grp * dh:kv + (grp + 1) * dh])
        s_new = jnp.sum(qf * k_new, axis=-1, keepdims=True)
        e, e_new, den = _softmax_with_new_key(_dot(qs, win_ref[0, grp]), win_ok, s_new, True)
        o = (_dot_nt(e, win_ref[1, grp]) + _bf16_round(e_new) * v_new) / den
        for r in range(rep):
            h = grp * rep + r
            owin_ref[:, h * dh:(h + 1) * dh] = o[r:r + 1]


def _attn_sample(page_table, sel_idx, q, ks_new, kw_new, win, cache, t_pos):
    n_seq, qc = q.shape
    n_pool, page = cache.shape[:2]
    dh = NSA_HEAD_DIM
    c = 2 * NSA_KV_HEADS * dh
    n_pages = page_table.shape[1]
    wb = win.shape[1]
    row3 = lambda w: pl.BlockSpec((None, 1, w), lambda b, pt, ix: (b, 0, 0))
    grid_spec = pltpu.PrefetchScalarGridSpec(
        num_scalar_prefetch=2,
        grid=(n_seq,),
        in_specs=[row3(qc), row3(c), row3(c),
                  pl.BlockSpec((None, 2, NSA_KV_HEADS, dh, wb), lambda b, pt, ix: (b, 0, 0, 0, 0)),
                  pl.BlockSpec(memory_space=pl.ANY)],
        out_specs=[row3(qc), row3(qc)],
        scratch_shapes=[pltpu.VMEM((NSA_KV_HEADS, 2, dh, NSA_TOPK * page), F32), pltpu.SemaphoreType.DMA(())],
    )
    win, cache = _rows_on_lanes(win), _rows_on_lanes(cache)
    return pl.pallas_call(
        functools.partial(_attn_sample_body, t_pos=t_pos, nb_past=t_pos // NSA_BLOCK, n_pages=n_pages),
        grid_spec=grid_spec,
        out_shape=[jax.ShapeDtypeStruct((n_seq, 1, qc), F32), jax.ShapeDtypeStruct((n_seq, 1, qc), F32)],
        compiler_params=_params("arbitrary"),
        name="attn_sample",
    )(page_table.reshape(-1), sel_idx.reshape(-1), q[:, None, :], ks_new[:, None, :], kw_new[:, None, :], win, cache)


def _nsa_merge_body(x_ref, m_ref, g_ref, oc_ref, os_ref, ow_ref, gt_ref, wout_ref, o_ref, om_ref):
    dh = NSA_HEAD_DIM
    n_heads = oc_ref.shape[1] // dh
    gt = gt_ref[...]
    for h in range(n_heads):
        c = slice(h * dh, (h + 1) * dh)
        o = (gt[:, h:h + 1] * oc_ref[:, c] + gt[:, n_heads + h:n_heads + h + 1] * os_ref[:, c]
             + gt[:, 2 * n_heads + h:2 * n_heads + h + 1] * ow_ref[:, c])
        om_ref[:, c] = o.astype(BF16)
    x = x_ref[...]
    o_ref[...] = x + m_ref[2] * _rms(_dot(om_ref[...], wout_ref[...]), g_ref[1:2])


def _nsa_merge(x, mod, g, o_cmp, o_slc, o_win, gates, w_out, tm):
    n, d = x.shape
    n_tiles = n // tm
    qc = o_cmp.shape[1]
    tile = lambda w: pl.BlockSpec((tm, w), lambda i: (i, 0))
    return pl.pallas_call(
        _nsa_merge_body,
        grid=(n_tiles,),
        in_specs=[tile(d), _mod_spec(mod, n_tiles), _const_spec(g), tile(qc), tile(qc), tile(qc),
                  tile(gates.shape[1]), _const_spec(w_out)],
        out_specs=tile(d),
        out_shape=jax.ShapeDtypeStruct((n, d), F32),
        scratch_shapes=[pltpu.VMEM((tm, qc), BF16)],
        compiler_params=_params("arbitrary"),
        name="nsa_merge",
    )(x, mod, g, o_cmp, o_slc, o_win, gates, w_out)


def _nsa_layer(xp, xs, mod_p, mod_s, g, n_seq, cache_cmp, cache_slc, cache_win, page_table,
               w_in, w1, w2, pe, w_out, tm):
    n_s = xs.shape[0]
    t = xp.shape[0] // n_seq
    page_size = cache_cmp.shape[1]
    past = page_table.shape[1] * page_size
    assert t % NSA_BLOCK == 0 and past % NSA_BLOCK == 0 and page_size % NSA_BLOCK == 0
    w_in_b, w_out_b = w_in.astype(BF16), w_out.astype(BF16)
    w1_b, w2_b = w1.astype(BF16), w2.astype(BF16)
    pe_rows = _pe_rows(pe)

    q, kc, _, _, gates, ksb, kwb, kc_t, ks_t, kw_t = _nsa_proj(xp, mod_p, g, w_in_b, tm, n_seq)
    cmp_p = _compress_prompt(kc, pe_rows, w1_b, w2_b, min(256, kc.shape[0] // NSA_BLOCK))
    o_cmp, sel = _cmpattn_prompt(q, cmp_p, n_seq, 128)
    o_slc, o_win = _attn_prompt(q, sel, ksb, kwb, n_seq)
    xp = _nsa_merge(xp, mod_p, g, o_cmp, o_slc, o_win, gates, w_out_b, tm)

    q_s, kc_s, ks_s, kw_s, gates_s, _, _ = _nsa_proj(xs, mod_s, g, w_in_b, n_s)
    cmp_s = _compress_sample(page_table, cache_cmp, pe.astype(F32), w1_b, w2_b, min(256, page_table.size))
    n_cand = -(-(past + 1) // NSA_BLOCK)
    o_cmp_s, sel_idx = _cmpattn_sample(q_s, cmp_s, past, n_cand)
    o_slc_s, o_win_s = _attn_sample(page_table, sel_idx, q_s, ks_s, kw_s, cache_win, cache_slc, past)
    xs = _nsa_merge(xs, mod_s, g, o_cmp_s.reshape(n_s, -1), o_slc_s.reshape(n_s, -1), o_win_s.reshape(n_s, -1),
                    gates_s, w_out_b, n_s)
    return xp, xs, (kc_t, ks_t, kw_t), (kc_s, ks_s, kw_s)


def kernel(x_prompt, x_sample, cache_nsa_cmp, cache_nsa_slc, cache_nsa_win, state_ssm, page_table, c_prompt, c_sample, w_mod, b_mod, norm_g, ffn_w_gate, ffn_w_up, ffn_w_down, gmlp_w_in, gmlp_b_in, gmlp_ln_g, gmlp_ln_b, gmlp_w_s, gmlp_b_s, gmlp_w_out, nsa_w_in, nsa_w_cmp1, nsa_w_cmp2, nsa_pe_cmp, nsa_w_out, ssm_lambda_re, ssm_lambda_im, ssm_b_re, ssm_b_im, ssm_c_re, ssm_c_im, ssm_d, ssm_log_step, ssm_w_glu1, ssm_b_glu1, ssm_w_glu2, ssm_b_glu2):
    n_seq, t, d = x_prompt.shape
    n_s, t_s, _ = x_sample.shape
    assert t_s == 1
    depth = w_mod.shape[0]
    tm = 512 if t % 512 == 0 else 256
    kv_shape = (2, NSA_KV_HEADS, NSA_HEAD_DIM)

    xp = x_prompt.reshape(n_seq * t, d)
    xs = x_sample.reshape(n_s, d)
    m_all = _adaln(jnp.concatenate([c_prompt, c_sample], axis=0), w_mod, b_mod)
    mods_p = m_all[:, :n_seq].reshape(depth, n_seq, 6, 1, d).transpose(0, 2, 1, 3, 4)
    mods_s = m_all[:, n_seq:].reshape(depth, n_s, 6, d).transpose(0, 2, 1, 3)[:, :, None]

    cmp_p, cmp_s, slc_p, slc_s, win_p, win_s, ssm_p, ssm_s, gv_s = [], [], [], [], [], [], [], [], []
    for i in range(depth):
        j = i // N_MIXERS
        mp, ms, g = mods_p[i], mods_s[i], norm_g[i]
        if i % N_MIXERS == 0:
            gw = (gmlp_w_in[j].astype(BF16), gmlp_b_in[j], gmlp_ln_g[j], gmlp_ln_b[j], gmlp_w_s[j], gmlp_b_s[j],
                  gmlp_w_out[j].astype(BF16))
            assert t % GMLP_CHUNK == 0 and gmlp_w_s.shape[2] == GMLP_CHUNK
            xp = _gmlp_prompt(xp, mp, g, *gw, tm)
            xs, v_new = _gmlp_sample(xs, ms, g, *gw)
            gv_s.append(v_new.reshape(n_s, 1, -1))
        elif i % N_MIXERS == 1:
            xp, xs, kv_p, kv_s = _nsa_layer(xp, xs, mp, ms, g, n_seq, cache_nsa_cmp[j], cache_nsa_slc[j],
                                            cache_nsa_win[j], page_table, nsa_w_in[j], nsa_w_cmp1[j],
                                            nsa_w_cmp2[j], nsa_pe_cmp[j], nsa_w_out[j], tm)
            rows_last = lambda a: jnp.transpose(a.reshape((n_seq,) + kv_shape + (a.shape[-1],)), (0, 4, 1, 2, 3))
            cmp_p.append(rows_last(kv_p[0]))
            slc_p.append(rows_last(kv_p[1]))
            win_p.append(rows_last(kv_p[2][:, :, t - min(NSA_WINDOW, t):]))
            cmp_s.append(kv_s[0].reshape((n_s, 1) + kv_shape))
            slc_s.append(kv_s[1].reshape((n_s, 1) + kv_shape))
            past = page_table.shape[1] * cache_nsa_cmp.shape[2]
            win = jnp.concatenate([cache_nsa_win[j], kv_s[2].reshape((n_s, 1) + kv_shape)], axis=1)
            win_s.append(win[:, win.shape[1] - min(NSA_WINDOW, past + 1):])
        else:
            assert t % SSM_CHUNK == 0
            tables, seg_tables = _ssm_tables(ssm_lambda_re[j], ssm_lambda_im[j], ssm_b_re[j], ssm_b_im[j],
                                             ssm_c_re[j], ssm_c_im[j], ssm_log_step[j], SSM_CHUNK // SSM_SEGMENTS)
            glu = (ssm_d[j], ssm_w_glu1[j].astype(BF16), ssm_b_glu1[j], ssm_w_glu2[j].astype(BF16), ssm_b_glu2[j])
            n_grp, n_st = ssm_lambda_re.shape[1:]
            xp, sr, si = _ssm_prompt(xp, mp, g, tables, seg_tables, *glu, n_seq, SSM_CHUNK)
            ssm_p.append(jnp.stack([sr.reshape(n_seq, n_grp, n_st), si.reshape(n_seq, n_grp, n_st)], axis=-1))
            h0 = state_ssm[j].reshape(n_s, n_grp * n_st, 2)
            xs, sr, si = _ssm_sample(xs, ms, g, tables, *glu, h0[..., 0], h0[..., 1])
            ssm_s.append(jnp.stack([sr.reshape(n_s, n_grp, n_st), si.reshape(n_s, n_grp, n_st)], axis=-1))
        ffn_w = (ffn_w_gate[i].astype(BF16), ffn_w_up[i].astype(BF16), ffn_w_down[i].astype(BF16))
        xp = _ffn(xp, mp, g, *ffn_w, tm)
        xs = _ffn(xs, ms, g, *ffn_w, n_s)
    return (xp.reshape(n_seq, t, d), xs.reshape(n_s, 1, d), jnp.stack(cmp_p), jnp.stack(cmp_s), jnp.stack(slc_p),
            jnp.stack(slc_s), jnp.stack(win_p), jnp.stack(win_s), jnp.stack(ssm_p), jnp.stack(ssm_s), jnp.stack(gv_s))
```

```python
import functools
import math

import jax
import jax.numpy as jnp
from jax import lax
from jax.experimental import pallas as pl
from jax.experimental.pallas import tpu as pltpu

F32 = jnp.float32
BF16 = jnp.bfloat16

RMS_EPS = 1.0e-6
LN_EPS = 1.0e-5

V7X_LANES = 128
V7X_VMEM_BYTES = 64 * 1024 * 1024
VMEM_LIMIT_BYTES = V7X_VMEM_BYTES - 8 * 1024 * 1024

N_MIXERS = 3
GMLP_GROUPS = 8
GMLP_CHUNK = 128
NSA_HEAD_DIM = 64
NSA_KV_HEADS = 4
NSA_BLOCK = 64
NSA_TOPK = 16
NSA_WINDOW = 512
ATTN_Q_TILE = 512
ATTN_Q_SUB = 128
ATTN_KV_TILE = 512
CMP_Q_TILE = 128
SEL_FORCE = 1.0e4
SEL_MASKED = -1.0
SSM_GROUP_WIDTH = 16
SSM_STATE = 64
SSM_SEGMENTS = 8
SSM_CHUNK = 256
NEG_BIG = -1.0e30


def _params(*sem):
    return pltpu.CompilerParams(dimension_semantics=sem, vmem_limit_bytes=VMEM_LIMIT_BYTES)


def _dot(a, b):
    return jnp.dot(a.astype(BF16), b.astype(BF16), preferred_element_type=F32)


def _dot_nt(a, b):
    return lax.dot_general(a.astype(BF16), b.astype(BF16), (((1,), (1,)), ((), ())),
                           preferred_element_type=F32)


def _rms(x, g):
    return x * lax.rsqrt(jnp.mean(x * x, axis=-1, keepdims=True) + RMS_EPS) * g


def _modulate(x, g, shift, scale):
    return _rms(x, g) * (1.0 + scale) + shift


def _const_spec(a, n_grid=1, single=False):
    nd = a.ndim
    idx = {1: lambda i: (0,) * nd, 2: lambda i, j: (0,) * nd, 3: lambda i, j, k: (0,) * nd}[n_grid]
    if single:
        return pl.BlockSpec(a.shape, idx, pipeline_mode=pl.Buffered(1))
    return pl.BlockSpec(a.shape, idx)


def _mod_spec(mod, n_tiles):
    _, n_seq, rows, d = mod.shape
    tiles_per_seq = n_tiles // n_seq
    return pl.BlockSpec((6, None, rows, d), lambda i: (0, i // tiles_per_seq, 0, 0))


def _adaln_body(c_ref, w_ref, b_ref, o_ref):
    c = c_ref[...]
    o_ref[...] = _dot(c * jax.nn.sigmoid(c), w_ref[...]) + b_ref[...]


def _adaln(c_all, w_mod, b_mod):
    depth, d, d6 = w_mod.shape
    m = c_all.shape[0]
    tn = 2048
    return pl.pallas_call(
        _adaln_body,
        grid=(depth, d6 // tn),
        in_specs=[pl.BlockSpec((m, d), lambda l, j: (0, 0)),
                  pl.BlockSpec((None, d, tn), lambda l, j: (l, 0, j)),
                  pl.BlockSpec((None, 1, tn), lambda l, j: (l, 0, j))],
        out_specs=pl.BlockSpec((None, m, tn), lambda l, j: (l, 0, j)),
        out_shape=jax.ShapeDtypeStruct((depth, m, d6), F32),
        compiler_params=_params("arbitrary", "arbitrary"),
        name="adaln",
    )(c_all, w_mod, b_mod.reshape(depth, 1, d6))


def _ffn_body(x_ref, m_ref, g_ref, wg_ref, wu_ref, wd_ref, o_ref, *, n_chunks):
    x = x_ref[...]
    h = _modulate(x, g_ref[2:3], m_ref[3], m_ref[4]).astype(BF16)
    fc = wg_ref.shape[1] // n_chunks
    acc = None
    for c in range(n_chunks):
        a = _dot(h, wg_ref[:, c * fc:(c + 1) * fc])
        b = _dot(h, wu_ref[:, c * fc:(c + 1) * fc])
        y = _dot(a * jax.nn.sigmoid(a) * b, wd_ref[c * fc:(c + 1) * fc, :])
        acc = y if acc is None else acc + y
    o_ref[...] = x + m_ref[5] * _rms(acc, g_ref[3:4])


def _ffn(x, mod, g, wg, wu, wd, tm):
    n, d = x.shape
    n_tiles = n // tm
    return pl.pallas_call(
        functools.partial(_ffn_body, n_chunks=2),
        grid=(n_tiles,),
        in_specs=[pl.BlockSpec((tm, d), lambda i: (i, 0)), _mod_spec(mod, n_tiles), _const_spec(g),
                  _const_spec(wg, single=True), _const_spec(wu, single=True), _const_spec(wd, single=True)],
        out_specs=pl.BlockSpec((tm, d), lambda i: (i, 0)),
        out_shape=jax.ShapeDtypeStruct((n, d), F32),
        compiler_params=_params("arbitrary"),
        name="ffn",
    )(x, mod, g, wg, wu, wd)


def _gmlp_front(x_ref, m_ref, g_ref, win_ref, bin_ref, lng_ref, lnb_ref):
    x = x_ref[...]
    h = _modulate(x, g_ref[0:1], m_ref[0], m_ref[1])
    z = jax.nn.gelu(_dot(h, win_ref[...]) + bin_ref[...])
    half = z.shape[1] // 2
    u, v = z[:, :half], z[:, half:]
    mu = jnp.mean(v, axis=-1, keepdims=True)
    var = jnp.mean(jnp.square(v - mu), axis=-1, keepdims=True)
    v = (v - mu) * lax.rsqrt(var + LN_EPS) * lng_ref[...] + lnb_ref[...]
    return x, u, v


def _gmlp_prompt_body(x_ref, m_ref, g_ref, win_ref, bin_ref, lng_ref, lnb_ref, ws_ref, bs_ref, wout_ref,
                      o_ref, um_ref):
    x, u, v = _gmlp_front(x_ref, m_ref, g_ref, win_ref, bin_ref, lng_ref, lnb_ref)
    vb = v.astype(BF16)
    n_groups, chunk, _ = ws_ref.shape
    gw = v.shape[1] // n_groups
    causal = (lax.broadcasted_iota(jnp.int32, (chunk, chunk), 0)
              >= lax.broadcasted_iota(jnp.int32, (chunk, chunk), 1))
    for grp in range(n_groups):
        w = jnp.where(causal, ws_ref[grp], 0.0).astype(BF16)
        cols = slice(grp * gw, (grp + 1) * gw)
        for k in range(x.shape[0] // chunk):
            rows = slice(k * chunk, (k + 1) * chunk)
            mixed = _dot(w, vb[rows, cols]) + bs_ref[:, grp:grp + 1]
            um_ref[rows, cols] = (u[rows, cols] * mixed).astype(BF16)
    y = _dot(um_ref[...], wout_ref[...])
    o_ref[...] = x + m_ref[2] * _rms(y, g_ref[1:2])


def _gmlp_sample_body(x_ref, m_ref, g_ref, win_ref, bin_ref, lng_ref, lnb_ref, ws_ref, bs_ref, wout_ref,
                      o_ref, v_ref):
    x, u, v = _gmlp_front(x_ref, m_ref, g_ref, win_ref, bin_ref, lng_ref, lnb_ref)
    v_ref[...] = v
    y = _dot(u * (ws_ref[...] * v + bs_ref[...]), wout_ref[...])
    o_ref[...] = x + m_ref[2] * _rms(y, g_ref[1:2])


def _gmlp_prompt(x, mod, g, w_in, b_in, ln_g, ln_b, w_s, b_s, w_out, tm):
    n, d = x.shape
    n_tiles = n // tm
    half = w_out.shape[0]
    args = (x, mod, g, w_in, b_in[None], ln_g[None], ln_b[None], w_s, b_s.T, w_out)
    return pl.pallas_call(
        _gmlp_prompt_body,
        grid=(n_tiles,),
        in_specs=[pl.BlockSpec((tm, d), lambda i: (i, 0)), _mod_spec(mod, n_tiles)]
        + [_const_spec(a) for a in args[2:]],
        out_specs=pl.BlockSpec((tm, d), lambda i: (i, 0)),
        out_shape=jax.ShapeDtypeStruct((n, d), F32),
        scratch_shapes=[pltpu.VMEM((tm, half), BF16)],
        compiler_params=_params("arbitrary"),
        name="gmlp_prompt",
    )(*args)


def _gmlp_sample(x, mod, g, w_in, b_in, ln_g, ln_b, w_s, b_s, w_out):
    n, d = x.shape
    half = w_out.shape[0]
    gw = half // w_s.shape[0]
    args = (x, mod, g, w_in, b_in[None], ln_g[None], ln_b[None],
            jnp.repeat(w_s[:, 0, 0], gw)[None], jnp.repeat(b_s[:, 0], gw)[None], w_out)
    return pl.pallas_call(
        _gmlp_sample_body,
        grid=(1,),
        in_specs=[pl.BlockSpec((n, d), lambda i: (0, 0)), _mod_spec(mod, 1)]
        + [_const_spec(a) for a in args[2:]],
        out_specs=[pl.BlockSpec((n, d), lambda i: (0, 0)), pl.BlockSpec((n, half), lambda i: (0, 0))],
        out_shape=[jax.ShapeDtypeStruct((n, d), F32), jax.ShapeDtypeStruct((n, half), F32)],
        compiler_params=_params("arbitrary"),
        name="gmlp_sample",
    )(*args)


def _cmul(ar, ai, br, bi):
    return ar * br - ai * bi, ar * bi + ai * br


def _powers(base, n):
    p = [base]
    for k in range(2, n + 1):
        p.append(_cmul(*p[k // 2 - 1], *p[k - k // 2 - 1]))
    return p


def _ssm_prep_body(lr_ref, li_ref, ls_ref, br_ref, bi_ref, pwr_ref, pwi_ref, par_ref, pai_ref, bbr_ref, bbi_ref):
    lr, li = lr_ref[...], li_ref[...]
    dt = jnp.exp(ls_ref[...])
    mag = jnp.exp(lr * dt)
    ab_re, ab_im = mag * jnp.cos(li * dt), mag * jnp.sin(li * dt)
    den = lr * lr + li * li
    f_re = ((ab_re - 1.0) * lr + ab_im * li) / den
    f_im = (ab_im * lr - (ab_re - 1.0) * li) / den
    bbr_ref[...] = f_re[:, None, :] * br_ref[...] - f_im[:, None, :] * bi_ref[...]
    bbi_ref[...] = f_re[:, None, :] * bi_ref[...] + f_im[:, None, :] * br_ref[...]
    n_steps = pwr_ref.shape[0]
    p = _powers((ab_re, ab_im), n_steps)
    for n in range(n_steps):
        pwr_ref[n] = p[n][0]
        pwi_ref[n] = p[n][1]
    a = _powers(p[n_steps - 1], SSM_SEGMENTS)
    par_ref[0] = jnp.ones_like(ab_re)
    pai_ref[0] = jnp.zeros_like(ab_re)
    for k in range(SSM_SEGMENTS):
        par_ref[k + 1] = a[k][0]
        pai_ref[k + 1] = a[k][1]


def _ssm_prep(lam_re, lam_im, log_step, b_re, b_im, n_steps):
    g, p = lam_re.shape
    w = b_re.shape[2]
    args = (lam_re, lam_im, log_step[:, None], jnp.swapaxes(b_re, 1, 2), jnp.swapaxes(b_im, 1, 2))
    return pl.pallas_call(
        _ssm_prep_body,
        out_shape=[jax.ShapeDtypeStruct((n_steps, g, p), F32)] * 2
        + [jax.ShapeDtypeStruct((SSM_SEGMENTS + 1, g, p), F32)] * 2
        + [jax.ShapeDtypeStruct((g, w, p), F32)] * 2,
        name="ssm_prep",
    )(*args)


def _ssm_input(x_ref, m_ref, g_ref, bbr_ref, bbi_ref, xr_ref, xi_ref):
    x = x_ref[...]
    u = _modulate(x, g_ref[0:1], m_ref[0], m_ref[1])
    ub = u.astype(BF16)
    n_kb, kin, kout = bbr_ref.shape
    for kb in range(n_kb):
        xr_ref[:, kb * kout:(kb + 1) * kout] = _dot(ub[:, kb * kin:(kb + 1) * kin], bbr_ref[kb])
        xi_ref[:, kb * kout:(kb + 1) * kout] = _dot(ub[:, kb * kin:(kb + 1) * kin], bbi_ref[kb])
    return x, u


def _ssm_readout(xr_ref, xi_ref, cr_ref, ci_ref):
    n_kb, kin, _ = cr_ref.shape
    return [_dot(xr_ref[:, kb * kin:(kb + 1) * kin], cr_ref[kb]) - _dot(xi_ref[:, kb * kin:(kb + 1) * kin], ci_ref[kb])
            for kb in range(n_kb)]


def _ssm_output(x, u, y, m_ref, g_ref, d_ref, w1_ref, b1_ref, w2_ref, b2_ref, o_ref):
    gl = jax.nn.gelu(y + d_ref[...] * u)
    out = (_dot(gl, w1_ref[...]) + b1_ref[...]) * jax.nn.sigmoid(_dot(gl, w2_ref[...]) + b2_ref[...])
    o_ref[...] = x + m_ref[2] * _rms(out, g_ref[1:2])


def _ssm_prompt_body(x_ref, m_ref, g_ref, bbr_ref, bbi_ref, pwr_ref, pwi_ref, cr_ref, ci_ref, d_ref,
                     w1_ref, b1_ref, w2_ref, b2_ref, par_ref, pai_ref, o_ref, sr_ref, si_ref,
                     perm_ref, xr_ref, xi_ref, car_ref, cai_ref, *, lane_block):
    n_seg = SSM_SEGMENTS

    @pl.when(pl.program_id(1) == 0)
    def _():
        car_ref[...] = jnp.zeros_like(car_ref)
        cai_ref[...] = jnp.zeros_like(cai_ref)

    rows, n_state = xr_ref.shape
    n_steps = rows // n_seg
    n_tiles, _, lanes = perm_ref.shape
    x = x_ref[...]
    u = _modulate(x, g_ref[0:1], m_ref[0], m_ref[1])
    for c in range(n_tiles):
        perm_ref[c] = u[:, c * lanes:(c + 1) * lanes]
    ub = jnp.concatenate(
        [jnp.concatenate([perm_ref[c, pl.ds(s, n_seg, stride=n_steps), :] for s in range(n_steps)], axis=0)
         for c in range(n_tiles)], axis=1).astype(BF16)
    n_kb, kin, kout = bbr_ref.shape
    for kb in range(n_kb):
        xr_ref[:, kb * kout:(kb + 1) * kout] = _dot(ub[:, kb * kin:(kb + 1) * kin], bbr_ref[kb])
        xi_ref[:, kb * kout:(kb + 1) * kout] = _dot(ub[:, kb * kin:(kb + 1) * kin], bbi_ref[kb])

    row = lax.broadcasted_iota(jnp.int32, (n_seg, lane_block), 0)

    def shifted(v, s):
        return jnp.where(row >= s, pltpu.roll(v, s, axis=0), 0.0)

    for cb in range(n_state // lane_block):
        cols = slice(cb * lane_block, (cb + 1) * lane_block)
        ar, ai = pwr_ref[0:1, cols], pwi_ref[0:1, cols]
        vr = vi = jnp.zeros((n_seg, lane_block), F32)
        for s in range(n_steps):
            dr, di = _cmul(ar, ai, vr, vi)
            vr = xr_ref[s * n_seg:(s + 1) * n_seg, cols] + dr
            vi = xi_ref[s * n_seg:(s + 1) * n_seg, cols] + di
            xr_ref[s * n_seg:(s + 1) * n_seg, cols] = vr
            xi_ref[s * n_seg:(s + 1) * n_seg, cols] = vi
        for s in (1, 2, 4):
            dr, di = _cmul(par_ref[s:s + 1, cols], pai_ref[s:s + 1, cols], shifted(vr, s), shifted(vi, s))
            vr, vi = vr + dr, vi + di
        in_r, in_i = car_ref[:, cols], cai_ref[:, cols]
        dr, di = _cmul(par_ref[0:n_seg, cols], pai_ref[0:n_seg, cols], in_r, in_i)
        seg_r, seg_i = shifted(vr, 1) + dr, shifted(vi, 1) + di
        dr, di = _cmul(par_ref[n_seg:n_seg + 1, cols], pai_ref[n_seg:n_seg + 1, cols], in_r, in_i)
        car_ref[:, cols] = jnp.broadcast_to(vr[n_seg - 1:n_seg], vr.shape) + dr
        cai_ref[:, cols] = jnp.broadcast_to(vi[n_seg - 1:n_seg], vi.shape) + di
        for s in range(n_steps):
            dr, di = _cmul(pwr_ref[s:s + 1, cols], pwi_ref[s:s + 1, cols], seg_r, seg_i)
            xr_ref[s * n_seg:(s + 1) * n_seg, cols] += dr
            xi_ref[s * n_seg:(s + 1) * n_seg, cols] += di
    sr_ref[...] = car_ref[0:1, :]
    si_ref[...] = cai_ref[0:1, :]
    y_perm = jnp.concatenate(_ssm_readout(xr_ref, xi_ref, cr_ref, ci_ref), axis=1)
    for c in range(n_tiles):
        perm_ref[c] = y_perm[:, c * lanes:(c + 1) * lanes]
    y = jnp.concatenate(
        [jnp.concatenate([perm_ref[c, pl.ds(seg, n_steps, stride=n_seg), :] for seg in range(n_seg)], axis=0)
         for c in range(n_tiles)], axis=1)
    _ssm_output(x, u, y, m_ref, g_ref, d_ref, w1_ref, b1_ref, w2_ref, b2_ref, o_ref)


def _ssm_sample_body(x_ref, m_ref, g_ref, bbr_ref, bbi_ref, pwr_ref, pwi_ref, cr_ref, ci_ref, d_ref,
                     w1_ref, b1_ref, w2_ref, b2_ref, hr_ref, hi_ref, o_ref, sr_ref, si_ref,
                     xr_ref, xi_ref):
    x, u = _ssm_input(x_ref, m_ref, g_ref, bbr_ref, bbi_ref, xr_ref, xi_ref)
    dr, di = _cmul(pwr_ref[0:1, :], pwi_ref[0:1, :], hr_ref[...], hi_ref[...])
    xr_ref[...] = xr_ref[...] + dr
    xi_ref[...] = xi_ref[...] + di
    sr_ref[...] = xr_ref[...]
    si_ref[...] = xi_ref[...]
    y = jnp.concatenate(_ssm_readout(xr_ref, xi_ref, cr_ref, ci_ref), axis=1)
    _ssm_output(x, u, y, m_ref, g_ref, d_ref, w1_ref, b1_ref, w2_ref, b2_ref, o_ref)


def _ssm_tables(lam_re, lam_im, b_re, b_im, c_re, c_im, log_step, n_steps):
    g, p = lam_re.shape
    w = b_re.shape[2]
    pwr, pwi, par, pai, bbr, bbi = _ssm_prep(lam_re, lam_im, log_step, b_re, b_im, n_steps)
    gb = 256 // w
    eye = jnp.eye(gb, dtype=F32)

    def bd_in(a):
        return jnp.einsum("kgip,gh->kgihp", a.reshape(g // gb, gb, w, p), eye).reshape(g // gb, gb * w, gb * p)

    def bd_out(a):
        return jnp.einsum("kgip,gh->kgphi", a.reshape(g // gb, gb, w, p), eye).reshape(g // gb, gb * p, gb * w)

    tables = (bd_in(bbr).astype(BF16), bd_in(bbi).astype(BF16), pwr.reshape(n_steps, g * p),
              pwi.reshape(n_steps, g * p), bd_out(c_re).astype(BF16), bd_out(c_im).astype(BF16))
    return tables, (par.reshape(-1, g * p), pai.reshape(-1, g * p))


def _ssm_prompt(x, mod, g, tables, seg_tables, d_skip, w1, b1, w2, b2, n_seq, tl):
    n, d = x.shape
    t = n // n_seq
    n_state = tables[2].shape[1]
    assert tl == SSM_SEGMENTS * tables[2].shape[0]
    consts = tables + (d_skip[None], w1, b1[None], w2, b2[None]) + seg_tables
    row_spec = pl.BlockSpec((tl, d), lambda b, c: (b * (t // tl) + c, 0))
    st_spec = pl.BlockSpec((None, 1, n_state), lambda b, c: (b, 0, 0))
    return pl.pallas_call(
        functools.partial(_ssm_prompt_body, lane_block=1024),
        grid=(n_seq, t // tl),
        in_specs=[row_spec, pl.BlockSpec((6, None, 1, d), lambda b, c: (0, b, 0, 0)), _const_spec(g, 2)]
        + [_const_spec(a, 2) for a in consts],
        out_specs=[row_spec, st_spec, st_spec],
        out_shape=[jax.ShapeDtypeStruct((n, d), F32), jax.ShapeDtypeStruct((n_seq, 1, n_state), F32),
                   jax.ShapeDtypeStruct((n_seq, 1, n_state), F32)],
        scratch_shapes=[pltpu.VMEM((d // V7X_LANES, tl, V7X_LANES), F32),
                        pltpu.VMEM((tl, n_state), F32), pltpu.VMEM((tl, n_state), F32),
                        pltpu.VMEM((SSM_SEGMENTS, n_state), F32), pltpu.VMEM((SSM_SEGMENTS, n_state), F32)],
        compiler_params=_params("arbitrary", "arbitrary"),
        name="ssm_prompt",
    )(x, mod, g, *consts)


def _ssm_sample(x, mod, g, tables, d_skip, w1, b1, w2, b2, h_re, h_im):
    n, d = x.shape
    n_state = tables[2].shape[1]
    consts = tables + (d_skip[None], w1, b1[None], w2, b2[None], h_re, h_im)
    full = pl.BlockSpec((n, d), lambda i: (0, 0))
    st = pl.BlockSpec((n, n_state), lambda i: (0, 0))
    return pl.pallas_call(
        _ssm_sample_body,
        grid=(1,),
        in_specs=[full, _mod_spec(mod, 1), _const_spec(g)] + [_const_spec(a) for a in consts],
        out_specs=[full, st, st],
        out_shape=[jax.ShapeDtypeStruct((n, d), F32), jax.ShapeDtypeStruct((n, n_state), F32),
                   jax.ShapeDtypeStruct((n, n_state), F32)],
        scratch_shapes=[pltpu.VMEM((n, n_state), F32), pltpu.VMEM((n, n_state), F32)],
        compiler_params=_params("arbitrary"),
        name="ssm_sample",
    )(x, mod, g, *consts)


def _nsa_proj_body(x_ref, m_ref, g_ref, w_ref, q_ref, kc_ref, ks_ref, kw_ref, gt_ref, ksb_ref, kwb_ref, *t_refs):
    h = _modulate(x_ref[...], g_ref[0:1], m_ref[0], m_ref[1]).astype(BF16)
    qc, kc = q_ref.shape[1], kc_ref.shape[1]
    q_ref[...] = (_dot(h, w_ref[:, :qc]) * (NSA_HEAD_DIM ** -0.5)).astype(BF16)
    kc_ref[...] = _dot(h, w_ref[:, qc:qc + kc])
    ks = _dot(h, w_ref[:, qc + kc:qc + 2 * kc])
    kw = _dot(h, w_ref[:, qc + 2 * kc:qc + 3 * kc])
    ks_ref[...] = ks
    kw_ref[...] = kw
    ksb_ref[...] = ks.astype(BF16)
    kwb_ref[...] = kw.astype(BF16)
    gt_ref[...] = jax.nn.sigmoid(_dot(h, w_ref[:, qc + 3 * kc:]))
    for t_ref, rows in zip(t_refs, (kc_ref[...], ks, kw)):
        t_ref[...] = rows.T


def _nsa_proj(x, mod, g, w_in, tm, n_seq=None):
    n, d = x.shape
    n_tiles = n // tm
    kc = 2 * NSA_KV_HEADS * NSA_HEAD_DIM
    ng = (w_in.shape[1] - d - 3 * kc)
    widths = [(d, BF16), (kc, F32), (kc, F32), (kc, F32), (ng, F32), (kc, BF16), (kc, BF16)]
    out_specs = [pl.BlockSpec((tm, w), lambda i: (i, 0)) for w, _ in widths]
    out_shape = [jax.ShapeDtypeStruct((n, w), dt) for w, dt in widths]
    if n_seq is not None:
        tps = n_tiles // n_seq
        out_specs += [pl.BlockSpec((None, kc, tm), lambda i: (i // tps, 0, i % tps))] * 3
        out_shape += [jax.ShapeDtypeStruct((n_seq, kc, n // n_seq), F32)] * 3
    return pl.pallas_call(
        _nsa_proj_body,
        grid=(n_tiles,),
        in_specs=[pl.BlockSpec((tm, d), lambda i: (i, 0)), _mod_spec(mod, n_tiles), _const_spec(g),
                  _const_spec(w_in)],
        out_specs=out_specs,
        out_shape=out_shape,
        compiler_params=_params("arbitrary"),
        name="nsa_proj",
    )(x, mod, g, w_in)


def _compress_step(x_of, ls, pe_ref, w1_ref, w2_ref, o_ref, acc_ref, n_l):
    dh = NSA_HEAD_DIM
    hid = w1_ref.shape[3]
    n_zg = 2 * NSA_KV_HEADS

    @pl.when(ls == 0)
    def _():
        acc_ref[...] = jnp.zeros_like(acc_ref)

    for ll in range(n_l):
        l = ls * n_l + ll
        xb = (x_of(ll) + pe_ref[pl.ds(l, 1), :]).astype(BF16)
        for zg in range(n_zg):
            acc_ref[:, zg * hid:(zg + 1) * hid] += _dot(xb[:, zg * dh:(zg + 1) * dh], w1_ref[zg // NSA_KV_HEADS, l])

    @pl.when(ls == pl.num_programs(1) - 1)
    def _():
        a = acc_ref[...]
        hidv = (a * jax.nn.sigmoid(a)).astype(BF16)
        for zg in range(n_zg):
            o_ref[:, zg * dh:(zg + 1) * dh] = _dot(hidv[:, zg * hid:(zg + 1) * hid], w2_ref[zg // NSA_KV_HEADS])


def _compress_prompt_body(x_ref, pe_ref, w1_ref, w2_ref, o_ref, acc_ref, *, n_l):
    _compress_step(lambda ll: x_ref[:, ll, :], pl.program_id(1), pe_ref, w1_ref, w2_ref, o_ref, acc_ref, n_l)


def _compress_sample_body(pt_ref, cache_ref, pe_ref, w1_ref, w2_ref, o_ref, buf_ref, sem, acc_ref, *,
                          pages_per_step, d_tiles):
    pg, ds = pl.program_id(0), pl.program_id(1)
    n_ds = pl.num_programs(1)
    n_zg = 2 * NSA_KV_HEADS
    dh = NSA_HEAD_DIM
    per_page = o_ref.shape[0]
    hid = w1_ref.shape[3] // per_page
    step = pg * n_ds + ds
    slot = step % 2

    def page_copy(p, at_step, at_slot):
        page = pt_ref[(at_step // n_ds) * pages_per_step + p]
        return pltpu.make_async_copy(cache_ref.at[page, :, pl.ds((at_step % n_ds) * d_tiles, d_tiles)],
                                     buf_ref.at[at_slot, :, :, pl.ds(p * 8, 8), :], sem.at[at_slot])

    def start_all(at_step, at_slot):
        def start(p, c):
            page_copy(p, at_step, at_slot).start()
            return c

        lax.fori_loop(0, pages_per_step, start, 0, unroll=8)

    @pl.when(step == 0)
    def _():
        start_all(step, slot)

    @pl.when(step + 1 < pl.num_programs(0) * n_ds)
    def _():
        start_all(step + 1, 1 - slot)

    @pl.when(ds == 0)
    def _():
        acc_ref[...] = jnp.zeros_like(acc_ref)

    def wait(p, c):
        page_copy(p, step, slot).wait()
        return c

    lax.fori_loop(0, pages_per_step, wait, 0)
    for zg in range(n_zg):
        z = zg // NSA_KV_HEADS
        for dt in range(d_tiles):
            for dd in range(8):
                d = dt * 8 + dd
                x = buf_ref[slot, zg, dt, pl.ds(dd, pages_per_step, stride=8), :] + pe_ref[zg, d:d + 1, :]
                acc_ref[zg] += _dot(x, w1_ref[z, d])

    @pl.when(ds == pl.num_programs(1) - 1)
    def _():
        for zg in range(n_zg):
            a = acc_ref[zg]
            hidv = (a * jax.nn.sigmoid(a)).astype(BF16)
            for n in range(per_page):
                o_ref[n, :, zg * dh:(zg + 1) * dh] = _dot(hidv[:, n * hid:(n + 1) * hid], w2_ref[zg // NSA_KV_HEADS])


def _pe_rows(pe):
    blk = pe.shape[1]
    return jnp.broadcast_to(pe.transpose(1, 0, 2)[:, :, None, :], (blk, 2, NSA_KV_HEADS, pe.shape[2])).reshape(blk, -1)


_COMPRESS_ROWS = 16


def _compress_prompt(kc, pe, w1, w2, nbt):
    n, c = kc.shape
    nblk = n // NSA_BLOCK
    n_l = _COMPRESS_ROWS
    x3 = kc.reshape(nblk, NSA_BLOCK, c)
    return pl.pallas_call(
        functools.partial(_compress_prompt_body, n_l=n_l),
        grid=(nblk // nbt, NSA_BLOCK // n_l),
        in_specs=[pl.BlockSpec((nbt, n_l, c), lambda i, l: (i, l, 0)), _const_spec(pe, 2), _const_spec(w1, 2),
                  _const_spec(w2, 2)],
        out_specs=pl.BlockSpec((nbt, c), lambda i, l: (i, 0)),
        out_shape=jax.ShapeDtypeStruct((nblk, c), F32),
        scratch_shapes=[pltpu.VMEM((nbt, 2 * NSA_KV_HEADS * w1.shape[3]), F32)],
        compiler_params=_params("arbitrary", "arbitrary"),
        name="compress_prompt",
    )(x3, pe, w1, w2)


def _rows_on_lanes(cache):
    return jnp.transpose(cache, (0, 2, 3, 4, 1))


def _compress_sample(page_table, cache, pe, w1, w2, pages_per_step):
    n_pool, page = cache.shape[:2]
    dh = NSA_HEAD_DIM
    n_zg = 2 * NSA_KV_HEADS
    c = n_zg * dh
    per_page = page // NSA_BLOCK
    hid = w1.shape[3]
    d_tiles = 2
    n_pages_total = page_table.size
    cache_t = _rows_on_lanes(cache).reshape(n_pool, n_zg, dh // 8, 8, page)
    pe_t = jnp.tile(jnp.repeat(jnp.swapaxes(pe, 1, 2), NSA_KV_HEADS, axis=0), (1, 1, per_page))
    w1_t = jnp.einsum("zlde,nm->zdnlme", w1, jnp.eye(per_page, dtype=w1.dtype)).reshape(2, dh, page, per_page * hid)
    grid_spec = pltpu.PrefetchScalarGridSpec(
        num_scalar_prefetch=1,
        grid=(n_pages_total // pages_per_step, dh // (8 * d_tiles)),
        in_specs=[pl.BlockSpec(memory_space=pl.ANY),
                  pl.BlockSpec((n_zg, 8 * d_tiles, page), lambda i, s, pt: (0, s, 0)),
                  pl.BlockSpec((2, 8 * d_tiles, page, per_page * hid), lambda i, s, pt: (0, s, 0, 0)),
                  pl.BlockSpec(w2.shape, lambda i, s, pt: (0, 0, 0))],
        out_specs=pl.BlockSpec((per_page, pages_per_step, c), lambda i, s, pt: (0, i, 0)),
        scratch_shapes=[pltpu.VMEM((2, n_zg, d_tiles, pages_per_step * 8, page), F32), pltpu.SemaphoreType.DMA((2,)),
                        pltpu.VMEM((n_zg, pages_per_step, per_page * hid), F32)],
    )
    out = pl.pallas_call(
        functools.partial(_compress_sample_body, pages_per_step=pages_per_step, d_tiles=d_tiles),
        grid_spec=grid_spec,
        out_shape=jax.ShapeDtypeStruct((per_page, n_pages_total, c), F32),
        compiler_params=_params("arbitrary", "arbitrary"),
        name="compress_sample",
    )(page_table.reshape(-1), cache_t, pe_t, w1_t, w2)
    return jnp.swapaxes(out, 0, 1).reshape(n_pages_total * per_page, c)


def _stack_heads(q, grp):
    dh = NSA_HEAD_DIM
    rep = q.shape[1] // (NSA_KV_HEADS * dh)
    base = grp * rep * dh
    return jnp.concatenate([q[:, base + r * dh:base + (r + 1) * dh] for r in range(rep)], axis=0)


def _cmp_branch(qs, cmpv, grp, t_row, rep):
    dh = NSA_HEAD_DIM
    kv = NSA_KV_HEADS * dh
    kc = cmpv[:, grp * dh:(grp + 1) * dh]
    vc = cmpv[:, kv + grp * dh:kv + (grp + 1) * dh]
    s = _dot_nt(qs, kc)
    n = lax.broadcasted_iota(jnp.int32, s.shape, 1)
    mask = (n + 1) * NSA_BLOCK <= t_row + 1
    s = jnp.where(mask, s, NEG_BIG)
    e = jnp.where(mask, jnp.exp(s - jnp.max(s, axis=-1, keepdims=True)), 0.0)
    p = e / jnp.maximum(jnp.sum(e, axis=-1, keepdims=True), 1e-30)
    o = _dot(p, vc)
    t = p.shape[0] // rep
    imp = p[0:t]
    for r in range(1, rep):
        imp = imp + p[r * t:(r + 1) * t]
    return o, imp


def _topk_mask(score, axis):
    idx = lax.broadcasted_iota(jnp.int32, score.shape, axis).astype(F32)
    n = float(score.shape[axis])
    x = score
    for _ in range(NSA_TOPK):
        m = jnp.max(x, axis=axis, keepdims=True)
        first = jnp.min(jnp.where(x == m, idx, n), axis=axis, keepdims=True)
        x = jnp.where(idx == first, -jnp.inf, x)
    return (x == -jnp.inf).astype(F32)


def _cmpattn_prompt_body(q_ref, cmp_ref, o_ref, sel_ref):
    tq = q_ref.shape[0]
    nb = cmp_ref.shape[0]
    dh = NSA_HEAD_DIM
    rep = q_ref.shape[1] // (NSA_KV_HEADS * dh)
    t0 = pl.program_id(1) * tq
    q = q_ref[...]
    cmpv = cmp_ref[...]
    t_row = t0 + lax.broadcasted_iota(jnp.int32, (rep * tq, 1), 0) % tq
    blk = lax.broadcasted_iota(jnp.int32, (nb, tq), 0)
    jt = (t0 + lax.broadcasted_iota(jnp.int32, (nb, tq), 1)) // NSA_BLOCK
    forced = (blk == 0) | (blk == jt) | (blk == jt - 1)
    for grp in range(NSA_KV_HEADS):
        o, imp = _cmp_branch(_stack_heads(q, grp), cmpv, grp, t_row, rep)
        for r in range(rep):
            h = grp * rep + r
            o_ref[:, h * dh:(h + 1) * dh] = o[r * tq:(r + 1) * tq]
        score = jnp.where(blk <= jt, jnp.where(forced, SEL_FORCE, imp.T), SEL_MASKED)
        sel = _topk_mask(score, 0) * (score > 0.5 * SEL_MASKED).astype(F32)
        sel_ref[:, grp * nb:(grp + 1) * nb] = jnp.where(sel.T > 0.5, 0.0, NEG_BIG).astype(BF16)


def _cmpattn_prompt(q, cmp, n_seq, tq):
    n, qc = q.shape
    t = n // n_seq
    nb = cmp.shape[0] // n_seq
    return pl.pallas_call(
        _cmpattn_prompt_body,
        grid=(n_seq, t // tq),
        in_specs=[pl.BlockSpec((tq, qc), lambda b, i: (b * (t // tq) + i, 0)),
                  pl.BlockSpec((nb, cmp.shape[1]), lambda b, i: (b, 0))],
        out_specs=[pl.BlockSpec((tq, qc), lambda b, i: (b * (t // tq) + i, 0)),
                   pl.BlockSpec((tq, NSA_KV_HEADS * nb), lambda b, i: (b * (t // tq) + i, 0))],
        out_shape=[jax.ShapeDtypeStruct((n, qc), F32), jax.ShapeDtypeStruct((n, NSA_KV_HEADS * nb), BF16)],
        compiler_params=_params("arbitrary", "arbitrary"),
        name="cmpattn_prompt",
    )(q, cmp)


def _cmpattn_sample_body(q_ref, cmp_ref, o_ref, imp_ref, *, t_pos):
    dh = NSA_HEAD_DIM
    rep = q_ref.shape[1] // (NSA_KV_HEADS * dh)
    q = q_ref[...]
    cmpv = cmp_ref[...]
    t_row = jnp.full((rep, 1), t_pos, jnp.int32)
    for grp in range(NSA_KV_HEADS):
        o, imp = _cmp_branch(_stack_heads(q, grp), cmpv, grp, t_row, rep)
        for r in range(rep):
            h = grp * rep + r
            o_ref[:, h * dh:(h + 1) * dh] = o[r:r + 1]
        imp_ref[grp:grp + 1, :] = imp


def _select_sample_body(imp_ref, idx_ref, *, t_pos, n_cand):
    imp = imp_ref[...]
    n_rows, nb = imp.shape
    width = idx_ref.shape[1]
    lanes = ((n_cand + V7X_LANES - 1) // V7X_LANES) * V7X_LANES
    blk = lax.broadcasted_iota(jnp.int32, (1, lanes), 1)
    jt = t_pos // NSA_BLOCK
    forced = (blk == 0) | (blk == jt) | (blk == jt - 1)
    col = lax.broadcasted_iota(jnp.int32, (1, width), 1)
    blk_f = blk.astype(F32)
    if lanes > nb:
        imp = jnp.concatenate([imp, jnp.zeros((n_rows, lanes - nb), F32)], axis=1)
    score = jnp.where(blk <= jt, jnp.where(forced, SEL_FORCE, imp), SEL_MASKED)
    x = jnp.where(blk < n_cand, score, -jnp.inf)
    out = jnp.full((n_rows, width), -1, jnp.int32)
    for k in range(NSA_TOPK):
        m = jnp.max(x, axis=1, keepdims=True)
        first = jnp.min(jnp.where(x == m, blk_f, float(lanes)), axis=1, keepdims=True)
        chosen = jnp.where(m > 0.5 * SEL_MASKED, first, -1.0).astype(jnp.int32)
        out = jnp.where(col == k, chosen, out)
        x = jnp.where(blk_f == first, -jnp.inf, x)
    idx_ref[...] = out


def _cmpattn_sample(q, cmp, t_pos, n_cand):
    n_seq, qc = q.shape
    nb = cmp.shape[0] // n_seq
    o_cmp, imp = pl.pallas_call(
        functools.partial(_cmpattn_sample_body, t_pos=t_pos),
        grid=(n_seq,),
        in_specs=[pl.BlockSpec((None, 1, qc), lambda b: (b, 0, 0)), pl.BlockSpec((nb, cmp.shape[1]), lambda b: (b, 0))],
        out_specs=[pl.BlockSpec((None, 1, qc), lambda b: (b, 0, 0)),
                   pl.BlockSpec((None, NSA_KV_HEADS, nb), lambda b: (b, 0, 0))],
        out_shape=[jax.ShapeDtypeStruct((n_seq, 1, qc), F32), jax.ShapeDtypeStruct((n_seq, NSA_KV_HEADS, nb), F32)],
        compiler_params=_params("arbitrary"),
        name="cmpattn_sample",
    )(q[:, None, :], cmp)
    sel_idx = pl.pallas_call(
        functools.partial(_select_sample_body, t_pos=t_pos, n_cand=n_cand),
        out_shape=jax.ShapeDtypeStruct((n_seq * NSA_KV_HEADS, 128), jnp.int32),
        name="select_sample",
    )(imp.reshape(n_seq * NSA_KV_HEADS, nb))
    return o_cmp, sel_idx.reshape(n_seq, NSA_KV_HEADS, 128)


def _attn_prompt_body(q_ref, sel_ref, ks_ref, kw_ref, oslc_ref, owin_ref, qa_ref, m_ref, acc_ref, *, tk, n_sub):
    tq = q_ref.shape[0]
    ts = tq // n_sub
    dh = NSA_HEAD_DIM
    slab = 2 * dh
    kv = NSA_KV_HEADS * dh
    rep = q_ref.shape[1] // kv
    rows, rows_s = rep * tq, rep * ts
    nb = sel_ref.shape[1] // NSA_KV_HEADS
    t0 = pl.program_id(1) * tq
    lane = lax.broadcasted_iota(jnp.int32, (ts, slab), 1)

    for grp in range(NSA_KV_HEADS):
        off = (grp % 2) * dh
        for sub in range(n_sub):
            tok = slice(sub * ts, (sub + 1) * ts)
            parts = []
            for r in range(rep):
                h = grp * rep + r
                x = q_ref[tok, (h // 2) * slab:(h // 2 + 1) * slab].astype(F32)
                if h % 2 != grp % 2:
                    x = pltpu.roll(x, dh, axis=1)
                parts.append(jnp.where((lane >= off) & (lane < off + dh), x, 0.0))
            rr = slice(sub * rows_s, (sub + 1) * rows_s)
            qa_ref[grp, rr, 0:slab] = jnp.concatenate(parts, axis=0).astype(BF16)
            qa_ref[grp, rr, slab:slab + nb] = jnp.concatenate([sel_ref[tok, grp * nb:(grp + 1) * nb]] * rep, axis=0)
    m_ref[...] = jnp.full(m_ref.shape, 0.1 * NEG_BIG, F32)
    acc_ref[...] = jnp.zeros_like(acc_ref)

    def flash_tile(r0, n_rows, k0, width, bias):
        rr = slice(r0, r0 + n_rows)
        key_blk = (k0 + lax.broadcasted_iota(jnp.int32, (width, nb), 0)) // NSA_BLOCK
        onehot = (key_blk == lax.broadcasted_iota(jnp.int32, (width, nb), 1)).astype(BF16)
        upper_half = lax.broadcasted_iota(jnp.int32, (width, slab), 1) >= dh
        for grp in range(NSA_KV_HEADS):
            pair = (grp // 2) * slab
            k_aug = jnp.concatenate([ks_ref[pl.ds(k0, width), pair:pair + slab], onehot], axis=1)
            s = _dot_nt(qa_ref[grp, rr], k_aug)
            if bias is not None:
                s = (s.reshape(rep, n_rows // rep, width) + bias[None]).reshape(n_rows, width)
            cols = [s[:, c * slab:(c + 1) * slab] for c in range(width // slab)]
            mx = functools.reduce(jnp.maximum, cols)
            m_old = m_ref[grp, rr]
            m_new = jnp.maximum(m_old, jnp.max(mx, axis=-1, keepdims=True))
            alpha = jnp.exp(m_old - m_new)
            p = jnp.concatenate([jnp.exp(c - m_new).astype(BF16) for c in cols], axis=1)
            v = ks_ref[pl.ds(k0, width), kv + pair:kv + pair + slab]
            v = jnp.where(upper_half == (grp % 2 == 1), v, jnp.ones_like(v))
            acc_ref[grp, rr] = alpha * acc_ref[grp, rr] + _dot(p, v)
            m_ref[grp, rr] = m_new

    n_full = t0 // tk

    def full_tile(j, carry):
        flash_tile(0, rows, pl.multiple_of(j * tk, tk), tk, None)
        return carry

    lax.fori_loop(0, n_full, full_tile, 0)

    win_len = NSA_WINDOW + ts
    for sub in range(n_sub):
        ts0 = t0 + sub * ts
        r0 = sub * rows_s
        t_col = ts0 + lax.broadcasted_iota(jnp.int32, (ts, 1), 0)
        j_own = ts0 // tk
        if sub > 0:

            def before_own(j, carry, r0=r0):
                flash_tile(r0, rows_s, pl.multiple_of(j * tk, tk), tk, None)
                return carry

            lax.fori_loop(n_full, j_own, before_own, 0)
        k0 = pl.multiple_of(j_own * tk, tk)
        width = (sub + 1) * ts if tq == tk else tk
        late = k0 + lax.broadcasted_iota(jnp.int32, (1, width), 1) > t_col
        flash_tile(r0, rows_s, k0, width, jnp.where(late, NEG_BIG, 0.0))

        w0 = pl.multiple_of(jnp.maximum(ts0 - NSA_WINDOW, 0), ts)
        wpos = w0 + lax.broadcasted_iota(jnp.int32, (1, win_len), 1)
        win_bias = jnp.where((wpos <= t_col) & (wpos >= t_col - NSA_WINDOW), 0.0, NEG_BIG)
        tok = slice(sub * ts, (sub + 1) * ts)
        for grp in range(NSA_KV_HEADS):
            off = (grp % 2) * dh
            pair = (grp // 2) * slab
            acc = acc_ref[grp, r0:r0 + rows_s]
            o = acc / jnp.maximum(acc[:, dh - off:dh - off + 1], 1e-30)
            for r in range(rep):
                h = grp * rep + r
                oslc_ref[tok, h * dh:(h + 1) * dh] = o[r * ts:(r + 1) * ts, off:off + dh]

            s = _dot_nt(qa_ref[grp, r0:r0 + rows_s, 0:slab], kw_ref[pl.ds(w0, win_len), pair:pair + slab])
            s = s.reshape(rep, ts, win_len) + win_bias[None]
            m = jnp.maximum(jnp.max(s, axis=-1, keepdims=True), 0.1 * NEG_BIG)
            e = jnp.exp(s - m)
            o = _dot(e.reshape(rows_s, win_len), kw_ref[pl.ds(w0, win_len), kv + pair:kv + pair + slab])
            o = o / jnp.maximum(jnp.sum(e, axis=-1, keepdims=True).reshape(rows_s, 1), 1e-30)
            for r in range(rep):
                h = grp * rep + r
                owin_ref[tok, h * dh:(h + 1) * dh] = o[r * ts:(r + 1) * ts, off:off + dh]


def _attn_prompt(q, sel, ksb, kwb, n_seq):
    n, qc = q.shape
    t = n // n_seq
    dh = NSA_HEAD_DIM
    rep = qc // (NSA_KV_HEADS * dh)
    tq, ts, tk = ATTN_Q_TILE, ATTN_Q_SUB, ATTN_KV_TILE
    assert t % tk == 0 and t % tq == 0 and tq % ts == 0 and tk % ts == 0
    assert t >= NSA_WINDOW + ts and NSA_WINDOW % ts == 0
    tile = pl.BlockSpec((tq, qc), lambda b, i: (b * (t // tq) + i, 0))
    seq = pl.BlockSpec((t, ksb.shape[1]), lambda b, i: (b, 0), pipeline_mode=pl.Buffered(1))
    return pl.pallas_call(
        functools.partial(_attn_prompt_body, tk=tk, n_sub=tq // ts),
        grid=(n_seq, t // tq),
        in_specs=[tile, pl.BlockSpec((tq, sel.shape[1]), lambda b, i: (b * (t // tq) + i, 0)), seq, seq],
        out_specs=[tile, tile],
        out_shape=[jax.ShapeDtypeStruct((n, qc), F32), jax.ShapeDtypeStruct((n, qc), F32)],
        scratch_shapes=[pltpu.VMEM((NSA_KV_HEADS, rep * tq, 2 * dh + sel.shape[1] // NSA_KV_HEADS), BF16)]
        + [pltpu.VMEM((NSA_KV_HEADS, rep * tq, 2 * dh), F32)] * 2,
        compiler_params=_params("arbitrary", "arbitrary"),
        name="attn_prompt",
    )(q, sel, ksb, kwb)


def _softmax_with_new_key(s, ok, s_new, new_ok):
    s = jnp.where(ok, s, NEG_BIG)
    s_new = jnp.where(new_ok, s_new, NEG_BIG)
    m = jnp.maximum(jnp.max(s, axis=-1, keepdims=True), s_new)
    e = jnp.where(ok, jnp.exp(s - m), 0.0)
    e_new = jnp.where(new_ok, jnp.exp(s_new - m), 0.0)
    return e, e_new, jnp.maximum(jnp.sum(e, axis=-1, keepdims=True) + e_new, 1e-30)


def _bf16_round(x):
    return x.astype(BF16).astype(F32)


def _attn_sample_body(pt_ref, idx_ref, q_ref, ksn_ref, kwn_ref, win_ref, cache_ref, oslc_ref, owin_ref,
                      kbuf_ref, sem, *, t_pos, nb_past, n_pages):
    b = pl.program_id(0)
    dh = NSA_HEAD_DIM
    kv = NSA_KV_HEADS * dh
    rep = q_ref.shape[1] // kv
    n_sel = NSA_TOPK
    page = cache_ref.shape[4]
    per_page = page // NSA_BLOCK
    q = q_ref[...]

    def sel_index(grp, k):
        return idx_ref[(b * NSA_KV_HEADS + grp) * 128 + k]

    def in_pool(idx):
        return (idx >= 0) & (idx < nb_past)

    def page_copy(grp, k, idx):
        phys = pt_ref[b * n_pages + jnp.minimum(idx // per_page, n_pages - 1)]
        return pltpu.make_async_copy(cache_ref.at[phys, :, grp], kbuf_ref.at[grp, :, :, pl.ds(k * page, page)], sem)

    for grp in range(NSA_KV_HEADS):
        for k in range(n_sel):
            idx = sel_index(grp, k)

            @pl.when(in_pool(idx))
            def _():
                page_copy(grp, k, idx).start()

            @pl.when(jnp.logical_not(in_pool(idx)))
            def _():
                kbuf_ref[grp, :, :, k * page:(k + 1) * page] = jnp.zeros((2, dh, page), F32)

    for grp in range(NSA_KV_HEADS):
        for k in range(n_sel):
            idx = sel_index(grp, k)

            @pl.when(in_pool(idx))
            def _():
                page_copy(grp, k, idx).wait()

    lane = lax.broadcasted_iota(jnp.int32, (1, n_sel * page), 1)
    wb = win_ref.shape[3]
    wpos = t_pos - wb + lax.broadcasted_iota(jnp.int32, (1, wb), 1)
    win_ok = (wpos <= t_pos) & (wpos >= t_pos - NSA_WINDOW) & (wpos >= 0)
    for grp in range(NSA_KV_HEADS):
        qs = _stack_heads(q, grp)
        qf = qs.astype(F32)
        ok = jnp.zeros((1, n_sel * page), jnp.bool_)
        has_new = False
        for k in range(n_sel):
            idx = sel_index(grp, k)
            row = lane - k * page
            kpos = (idx // per_page) * page + row
            ok = ok | ((lane // page == k) & in_pool(idx) & (row // NSA_BLOCK == idx % per_page) & (kpos <= t_pos))
            has_new = has_new | (idx >= nb_past)
        new_ok = has_new & (nb_past * NSA_BLOCK <= t_pos)
        k_new = _bf16_round(ksn_ref[:, grp * dh:(grp + 1) * dh])
        v_new = _bf16_round(ksn_ref[:, kv + grp * dh:kv + (grp + 1) * dh])
        s_new = jnp.sum(qf * k_new, axis=-1, keepdims=True)
        e, e_new, den = _softmax_with_new_key(_dot(qs, kbuf_ref[grp, 0]), ok, s_new, new_ok)
        o = (_dot_nt(e, kbuf_ref[grp, 1]) + _bf16_round(e_new) * v_new) / den
        for r in range(rep):
            h = grp * rep + r
            oslc_ref[:, h * dh:(h + 1) * dh] = o[r:r + 1]

        k_new = _bf16_round(kwn_ref[:, grp * dh:(grp + 1) * dh])
        v_new = _bf16_round(kwn_ref[:, kv + grp * dh:kv + (grp + 1) * dh])
        s_new = jnp.sum(qf * k_new, axis=-1, keepdims=True)
        e, e_new, den = _softmax_with_new_key(_dot(qs, win_ref[0, grp]), win_ok, s_new, True)
        o = (_dot_nt(e, win_ref[1, grp]) + _bf16_round(e_new) * v_new) / den
        for r in range(rep):
            h = grp * rep + r
            owin_ref[:, h * dh:(h + 1) * dh] = o[r:r + 1]


def _attn_sample(page_table, sel_idx, q, ks_new, kw_new, win, cache, t_pos):
    n_seq, qc = q.shape
    n_pool, page = cache.shape[:2]
    dh = NSA_HEAD_DIM
    c = 2 * NSA_KV_HEADS * dh
    n_pages = page_table.shape[1]
    wb = win.shape[1]
    row3 = lambda w: pl.BlockSpec((None, 1, w), lambda b, pt, ix: (b, 0, 0))
    grid_spec = pltpu.PrefetchScalarGridSpec(
        num_scalar_prefetch=2,
        grid=(n_seq,),
        in_specs=[row3(qc), row3(c), row3(c),
                  pl.BlockSpec((None, 2, NSA_KV_HEADS, dh, wb), lambda b, pt, ix: (b, 0, 0, 0, 0)),
                  pl.BlockSpec(memory_space=pl.ANY)],
        out_specs=[row3(qc), row3(qc)],
        scratch_shapes=[pltpu.VMEM((NSA_KV_HEADS, 2, dh, NSA_TOPK * page), F32), pltpu.SemaphoreType.DMA(())],
    )
    win, cache = _rows_on_lanes(win), _rows_on_lanes(cache)
    return pl.pallas_call(
        functools.partial(_attn_sample_body, t_pos=t_pos, nb_past=t_pos // NSA_BLOCK, n_pages=n_pages),
        grid_spec=grid_spec,
        out_shape=[jax.ShapeDtypeStruct((n_seq, 1, qc), F32), jax.ShapeDtypeStruct((n_seq, 1, qc), F32)],
        compiler_params=_params("arbitrary"),
        name="attn_sample",
    )(page_table.reshape(-1), sel_idx.reshape(-1), q[:, None, :], ks_new[:, None, :], kw_new[:, None, :], win, cache)


def _nsa_merge_body(x_ref, m_ref, g_ref, oc_ref, os_ref, ow_ref, gt_ref, wout_ref, o_ref, om_ref):
    dh = NSA_HEAD_DIM
    n_heads = oc_ref.shape[1] // dh
    gt = gt_ref[...]
    for h in range(n_heads):
        c = slice(h * dh, (h + 1) * dh)
        o = (gt[:, h:h + 1] * oc_ref[:, c] + gt[:, n_heads + h:n_heads + h + 1] * os_ref[:, c]
             + gt[:, 2 * n_heads + h:2 * n_heads + h + 1] * ow_ref[:, c])
        om_ref[:, c] = o.astype(BF16)
    x = x_ref[...]
    o_ref[...] = x + m_ref[2] * _rms(_dot(om_ref[...], wout_ref[...]), g_ref[1:2])


def _nsa_merge(x, mod, g, o_cmp, o_slc, o_win, gates, w_out, tm):
    n, d = x.shape
    n_tiles = n // tm
    qc = o_cmp.shape[1]
    tile = lambda w: pl.BlockSpec((tm, w), lambda i: (i, 0))
    return pl.pallas_call(
        _nsa_merge_body,
        grid=(n_tiles,),
        in_specs=[tile(d), _mod_spec(mod, n_tiles), _const_spec(g), tile(qc), tile(qc), tile(qc),
                  tile(gates.shape[1]), _const_spec(w_out)],
        out_specs=tile(d),
        out_shape=jax.ShapeDtypeStruct((n, d), F32),
        scratch_shapes=[pltpu.VMEM((tm, qc), BF16)],
        compiler_params=_params("arbitrary"),
        name="nsa_merge",
    )(x, mod, g, o_cmp, o_slc, o_win, gates, w_out)


def _nsa_layer(xp, xs, mod_p, mod_s, g, n_seq, cache_cmp, cache_slc, cache_win, page_table,
               w_in, w1, w2, pe, w_out, tm):
    n_s = xs.shape[0]
    t = xp.shape[0] // n_seq
    page_size = cache_cmp.shape[1]
    past = page_table.shape[1] * page_size
    assert t % NSA_BLOCK == 0 and past % NSA_BLOCK == 0 and page_size % NSA_BLOCK == 0
    w_in_b, w_out_b = w_in.astype(BF16), w_out.astype(BF16)
    w1_b, w2_b = w1.astype(BF16), w2.astype(BF16)
    pe_rows = _pe_rows(pe)

    q, kc, _, _, gates, ksb, kwb, kc_t, ks_t, kw_t = _nsa_proj(xp, mod_p, g, w_in_b, tm, n_seq)
    cmp_p = _compress_prompt(kc, pe_rows, w1_b, w2_b, min(256, kc.shape[0] // NSA_BLOCK))
    o_cmp, sel = _cmpattn_prompt(q, cmp_p, n_seq, CMP_Q_TILE)
    o_slc, o_win = _attn_prompt(q, sel, ksb, kwb, n_seq)
    xp = _nsa_merge(xp, mod_p, g, o_cmp, o_slc, o_win, gates, w_out_b, tm)

    q_s, kc_s, ks_s, kw_s, gates_s, _, _ = _nsa_proj(xs, mod_s, g, w_in_b, n_s)
    cmp_s = _compress_sample(page_table, cache_cmp, pe.astype(F32), w1_b, w2_b, min(256, page_table.size))
    n_cand = -(-(past + 1) // NSA_BLOCK)
    o_cmp_s, sel_idx = _cmpattn_sample(q_s, cmp_s, past, n_cand)
    o_slc_s, o_win_s = _attn_sample(page_table, sel_idx, q_s, ks_s, kw_s, cache_win, cache_slc, past)
    xs = _nsa_merge(xs, mod_s, g, o_cmp_s.reshape(n_s, -1), o_slc_s.reshape(n_s, -1), o_win_s.reshape(n_s, -1),
                    gates_s, w_out_b, n_s)
    return xp, xs, (kc_t, ks_t, kw_t), (kc_s, ks_s, kw_s)


def kernel(x_prompt, x_sample, cache_nsa_cmp, cache_nsa_slc, cache_nsa_win, state_ssm, page_table, c_prompt, c_sample, w_mod, b_mod, norm_g, ffn_w_gate, ffn_w_up, ffn_w_down, gmlp_w_in, gmlp_b_in, gmlp_ln_g, gmlp_ln_b, gmlp_w_s, gmlp_b_s, gmlp_w_out, nsa_w_in, nsa_w_cmp1, nsa_w_cmp2, nsa_pe_cmp, nsa_w_out, ssm_lambda_re, ssm_lambda_im, ssm_b_re, ssm_b_im, ssm_c_re, ssm_c_im, ssm_d, ssm_log_step, ssm_w_glu1, ssm_b_glu1, ssm_w_glu2, ssm_b_glu2):
    n_seq, t, d = x_prompt.shape
    n_s, t_s, _ = x_sample.shape
    assert t_s == 1
    depth = w_mod.shape[0]
    tm = 512 if t % 512 == 0 else 256
    kv_shape = (2, NSA_KV_HEADS, NSA_HEAD_DIM)

    xp = x_prompt.reshape(n_seq * t, d)
    xs = x_sample.reshape(n_s, d)
    m_all = _adaln(jnp.concatenate([c_prompt, c_sample], axis=0), w_mod, b_mod)
    mods_p = m_all[:, :n_seq].reshape(depth, n_seq, 6, 1, d).transpose(0, 2, 1, 3, 4)
    mods_s = m_all[:, n_seq:].reshape(depth, n_s, 6, d).transpose(0, 2, 1, 3)[:, :, None]

    cmp_p, cmp_s, slc_p, slc_s, win_p, win_s, ssm_p, ssm_s, gv_s = [], [], [], [], [], [], [], [], []
    for i in range(depth):
        j = i // N_MIXERS
        mp, ms, g = mods_p[i], mods_s[i], norm_g[i]
        if i % N_MIXERS == 0:
            gw = (gmlp_w_in[j].astype(BF16), gmlp_b_in[j], gmlp_ln_g[j], gmlp_ln_b[j], gmlp_w_s[j], gmlp_b_s[j],
                  gmlp_w_out[j].astype(BF16))
            assert t % GMLP_CHUNK == 0 and gmlp_w_s.shape[2] == GMLP_CHUNK
            xp = _gmlp_prompt(xp, mp, g, *gw, tm)
            xs, v_new = _gmlp_sample(xs, ms, g, *gw)
            gv_s.append(v_new.reshape(n_s, 1, -1))
        elif i % N_MIXERS == 1:
            xp, xs, kv_p, kv_s = _nsa_layer(xp, xs, mp, ms, g, n_seq, cache_nsa_cmp[j], cache_nsa_slc[j],
                                            cache_nsa_win[j], page_table, nsa_w_in[j], nsa_w_cmp1[j],
                                            nsa_w_cmp2[j], nsa_pe_cmp[j], nsa_w_out[j], tm)
            rows_last = lambda a: jnp.transpose(a.reshape((n_seq,) + kv_shape + (a.shape[-1],)), (0, 4, 1, 2, 3))
            cmp_p.append(rows_last(kv_p[0]))
            slc_p.append(rows_last(kv_p[1]))
            win_p.append(rows_last(kv_p[2][:, :, t - min(NSA_WINDOW, t):]))
            cmp_s.append(kv_s[0].reshape((n_s, 1) + kv_shape))
            slc_s.append(kv_s[1].reshape((n_s, 1) + kv_shape))
            past = page_table.shape[1] * cache_nsa_cmp.shape[2]
            win = jnp.concatenate([cache_nsa_win[j], kv_s[2].reshape((n_s, 1) + kv_shape)], axis=1)
            win_s.append(win[:, win.shape[1] - min(NSA_WINDOW, past + 1):])
        else:
            assert t % SSM_CHUNK == 0
            tables, seg_tables = _ssm_tables(ssm_lambda_re[j], ssm_lambda_im[j], ssm_b_re[j], ssm_b_im[j],
                                             ssm_c_re[j], ssm_c_im[j], ssm_log_step[j], SSM_CHUNK // SSM_SEGMENTS)
            glu = (ssm_d[j], ssm_w_glu1[j].astype(BF16), ssm_b_glu1[j], ssm_w_glu2[j].astype(BF16), ssm_b_glu2[j])
            n_grp, n_st = ssm_lambda_re.shape[1:]
            xp, sr, si = _ssm_prompt(xp, mp, g, tables, seg_tables, *glu, n_seq, SSM_CHUNK)
            ssm_p.append(jnp.stack([sr.reshape(n_seq, n_grp, n_st), si.reshape(n_seq, n_grp, n_st)], axis=-1))
            h0 = state_ssm[j].reshape(n_s, n_grp * n_st, 2)
            xs, sr, si = _ssm_sample(xs, ms, g, tables, *glu, h0[..., 0], h0[..., 1])
            ssm_s.append(jnp.stack([sr.reshape(n_s, n_grp, n_st), si.reshape(n_s, n_grp, n_st)], axis=-1))
        ffn_w = (ffn_w_gate[i].astype(BF16), ffn_w_up[i].astype(BF16), ffn_w_down[i].astype(BF16))
        xp = _ffn(xp, mp, g, *ffn_w, tm)
        xs = _ffn(xs, ms, g, *ffn_w, n_s)
    return (xp.reshape(n_seq, t, d), xs.reshape(n_s, 1, d), jnp.stack(cmp_p), jnp.stack(cmp_s), jnp.stack(slc_p),
            jnp.stack(slc_s), jnp.stack(win_p), jnp.stack(win_s), jnp.stack(ssm_p), jnp.stack(ssm_s), jnp.stack(gv_s))
```

```python
import functools
import math

import jax
import jax.numpy as jnp
from jax import lax
from jax.experimental import pallas as pl
from jax.experimental.pallas import tpu as pltpu

F32 = jnp.float32
BF16 = jnp.bfloat16

RMS_EPS = 1.0e-6
LN_EPS = 1.0e-5

V7X_LANES = 128
V7X_VMEM_BYTES = 64 * 1024 * 1024
VMEM_LIMIT_BYTES = V7X_VMEM_BYTES - 8 * 1024 * 1024

N_MIXERS = 3
GMLP_GROUPS = 8
GMLP_CHUNK = 128
NSA_HEAD_DIM = 64
NSA_KV_HEADS = 4
NSA_BLOCK = 64
NSA_TOPK = 16
NSA_WINDOW = 512
ATTN_Q_TILE = 512
ATTN_Q_SUB = 128
ATTN_KV_TILE = 512
CMP_Q_TILE = 128
SEL_FORCE = 1.0e4
SEL_MASKED = -1.0
SSM_GROUP_WIDTH = 16
SSM_STATE = 64
SSM_SEGMENTS = 8
SSM_CHUNK = 256
NEG_BIG = -1.0e30


def _params(*sem):
    return pltpu.CompilerParams(dimension_semantics=sem, vmem_limit_bytes=VMEM_LIMIT_BYTES)


def _dot(a, b):
    return jnp.dot(a.astype(BF16), b.astype(BF16), preferred_element_type=F32)


def _dot_nt(a, b):
    return lax.dot_general(a.astype(BF16), b.astype(BF16), (((1,), (1,)), ((), ())),
                           preferred_element_type=F32)


def _rms(x, g):
    return x * lax.rsqrt(jnp.mean(x * x, axis=-1, keepdims=True) + RMS_EPS) * g


def _modulate(x, g, shift, scale):
    return _rms(x, g) * (1.0 + scale) + shift


def _const_spec(a, n_grid=1, single=False):
    nd = a.ndim
    idx = {1: lambda i: (0,) * nd, 2: lambda i, j: (0,) * nd, 3: lambda i, j, k: (0,) * nd}[n_grid]
    if single:
        return pl.BlockSpec(a.shape, idx, pipeline_mode=pl.Buffered(1))
    return pl.BlockSpec(a.shape, idx)


def _mod_spec(mod, n_tiles):
    _, n_seq, rows, d = mod.shape
    tiles_per_seq = n_tiles // n_seq
    return pl.BlockSpec((6, None, rows, d), lambda i: (0, i // tiles_per_seq, 0, 0))


def _adaln_body(c_ref, w_ref, b_ref, o_ref):
    c = c_ref[...]
    o_ref[...] = _dot(c * jax.nn.sigmoid(c), w_ref[...]) + b_ref[...]


def _adaln(c_all, w_mod, b_mod):
    depth, d, d6 = w_mod.shape
    m = c_all.shape[0]
    tn = 2048
    return pl.pallas_call(
        _adaln_body,
        grid=(depth, d6 // tn),
        in_specs=[pl.BlockSpec((m, d), lambda l, j: (0, 0)),
                  pl.BlockSpec((None, d, tn), lambda l, j: (l, 0, j)),
                  pl.BlockSpec((None, 1, tn), lambda l, j: (l, 0, j))],
        out_specs=pl.BlockSpec((None, m, tn), lambda l, j: (l, 0, j)),
        out_shape=jax.ShapeDtypeStruct((depth, m, d6), F32),
        compiler_params=_params("arbitrary", "arbitrary"),
        name="adaln",
    )(c_all, w_mod, b_mod.reshape(depth, 1, d6))


def _ffn_body(x_ref, m_ref, g_ref, wg_ref, wu_ref, wd_ref, o_ref, *, n_chunks):
    x = x_ref[...]
    h = _modulate(x, g_ref[2:3], m_ref[3], m_ref[4]).astype(BF16)
    fc = wg_ref.shape[1] // n_chunks
    acc = None
    for c in range(n_chunks):
        a = _dot(h, wg_ref[:, c * fc:(c + 1) * fc])
        b = _dot(h, wu_ref[:, c * fc:(c + 1) * fc])
        y = _dot(a * jax.nn.sigmoid(a) * b, wd_ref[c * fc:(c + 1) * fc, :])
        acc = y if acc is None else acc + y
    o_ref[...] = x + m_ref[5] * _rms(acc, g_ref[3:4])


def _ffn(x, mod, g, wg, wu, wd, tm):
    n, d = x.shape
    n_tiles = n // tm
    return pl.pallas_call(
        functools.partial(_ffn_body, n_chunks=2),
        grid=(n_tiles,),
        in_specs=[pl.BlockSpec((tm, d), lambda i: (i, 0)), _mod_spec(mod, n_tiles), _const_spec(g),
                  _const_spec(wg, single=True), _const_spec(wu, single=True), _const_spec(wd, single=True)],
        out_specs=pl.BlockSpec((tm, d), lambda i: (i, 0)),
        out_shape=jax.ShapeDtypeStruct((n, d), F32),
        compiler_params=_params("arbitrary"),
        name="ffn",
    )(x, mod, g, wg, wu, wd)


def _gmlp_front(x_ref, m_ref, g_ref, win_ref, bin_ref, lng_ref, lnb_ref):
    x = x_ref[...]
    h = _modulate(x, g_ref[0:1], m_ref[0], m_ref[1])
    z = jax.nn.gelu(_dot(h, win_ref[...]) + bin_ref[...])
    half = z.shape[1] // 2
    u, v = z[:, :half], z[:, half:]
    mu = jnp.mean(v, axis=-1, keepdims=True)
    var = jnp.mean(jnp.square(v - mu), axis=-1, keepdims=True)
    v = (v - mu) * lax.rsqrt(var + LN_EPS) * lng_ref[...] + lnb_ref[...]
    return x, u, v


def _gmlp_prompt_body(x_ref, m_ref, g_ref, win_ref, bin_ref, lng_ref, lnb_ref, ws_ref, bs_ref, wout_ref,
                      o_ref, um_ref):
    x, u, v = _gmlp_front(x_ref, m_ref, g_ref, win_ref, bin_ref, lng_ref, lnb_ref)
    vb = v.astype(BF16)
    n_groups, chunk, _ = ws_ref.shape
    gw = v.shape[1] // n_groups
    causal = (lax.broadcasted_iota(jnp.int32, (chunk, chunk), 0)
              >= lax.broadcasted_iota(jnp.int32, (chunk, chunk), 1))
    n_chunks = x.shape[0] // chunk
    for grp in range(n_groups):
        w = jnp.where(causal, ws_ref[grp], 0.0).astype(BF16)
        cols = slice(grp * gw, (grp + 1) * gw)
        v_all = jnp.concatenate([vb[k * chunk:(k + 1) * chunk, cols] for k in range(n_chunks)], axis=1)
        mixed = _dot(w, v_all) + bs_ref[:, grp:grp + 1]
        for k in range(n_chunks):
            rows = slice(k * chunk, (k + 1) * chunk)
            um_ref[rows, cols] = (u[rows, cols] * mixed[:, k * gw:(k + 1) * gw]).astype(BF16)
    y = _dot(um_ref[...], wout_ref[...])
    o_ref[...] = x + m_ref[2] * _rms(y, g_ref[1:2])


def _gmlp_sample_body(x_ref, m_ref, g_ref, win_ref, bin_ref, lng_ref, lnb_ref, ws_ref, bs_ref, wout_ref,
                      o_ref, v_ref):
    x, u, v = _gmlp_front(x_ref, m_ref, g_ref, win_ref, bin_ref, lng_ref, lnb_ref)
    v_ref[...] = v
    y = _dot(u * (ws_ref[...] * v + bs_ref[...]), wout_ref[...])
    o_ref[...] = x + m_ref[2] * _rms(y, g_ref[1:2])


def _gmlp_prompt(x, mod, g, w_in, b_in, ln_g, ln_b, w_s, b_s, w_out, tm):
    n, d = x.shape
    n_tiles = n // tm
    half = w_out.shape[0]
    args = (x, mod, g, w_in, b_in[None], ln_g[None], ln_b[None], w_s, b_s.T, w_out)
    return pl.pallas_call(
        _gmlp_prompt_body,
        grid=(n_tiles,),
        in_specs=[pl.BlockSpec((tm, d), lambda i: (i, 0)), _mod_spec(mod, n_tiles)]
        + [_const_spec(a) for a in args[2:]],
        out_specs=pl.BlockSpec((tm, d), lambda i: (i, 0)),
        out_shape=jax.ShapeDtypeStruct((n, d), F32),
        scratch_shapes=[pltpu.VMEM((tm, half), BF16)],
        compiler_params=_params("arbitrary"),
        name="gmlp_prompt",
    )(*args)


def _gmlp_sample(x, mod, g, w_in, b_in, ln_g, ln_b, w_s, b_s, w_out):
    n, d = x.shape
    half = w_out.shape[0]
    gw = half // w_s.shape[0]
    args = (x, mod, g, w_in, b_in[None], ln_g[None], ln_b[None],
            jnp.repeat(w_s[:, 0, 0], gw)[None], jnp.repeat(b_s[:, 0], gw)[None], w_out)
    return pl.pallas_call(
        _gmlp_sample_body,
        grid=(1,),
        in_specs=[pl.BlockSpec((n, d), lambda i: (0, 0)), _mod_spec(mod, 1)]
        + [_const_spec(a) for a in args[2:]],
        out_specs=[pl.BlockSpec((n, d), lambda i: (0, 0)), pl.BlockSpec((n, half), lambda i: (0, 0))],
        out_shape=[jax.ShapeDtypeStruct((n, d), F32), jax.ShapeDtypeStruct((n, half), F32)],
        compiler_params=_params("arbitrary"),
        name="gmlp_sample",
    )(*args)


def _cmul(ar, ai, br, bi):
    return ar * br - ai * bi, ar * bi + ai * br


def _powers(base, n):
    p = [base]
    for k in range(2, n + 1):
        p.append(_cmul(*p[k // 2 - 1], *p[k - k // 2 - 1]))
    return p


def _ssm_prep_body(lr_ref, li_ref, ls_ref, br_ref, bi_ref, pwr_ref, pwi_ref, par_ref, pai_ref, bbr_ref, bbi_ref):
    lr, li = lr_ref[...], li_ref[...]
    dt = jnp.exp(ls_ref[...])
    mag = jnp.exp(lr * dt)
    ab_re, ab_im = mag * jnp.cos(li * dt), mag * jnp.sin(li * dt)
    den = lr * lr + li * li
    f_re = ((ab_re - 1.0) * lr + ab_im * li) / den
    f_im = (ab_im * lr - (ab_re - 1.0) * li) / den
    bbr_ref[...] = f_re[:, None, :] * br_ref[...] - f_im[:, None, :] * bi_ref[...]
    bbi_ref[...] = f_re[:, None, :] * bi_ref[...] + f_im[:, None, :] * br_ref[...]
    n_steps = pwr_ref.shape[0]
    p = _powers((ab_re, ab_im), n_steps)
    for n in range(n_steps):
        pwr_ref[n] = p[n][0]
        pwi_ref[n] = p[n][1]
    a = _powers(p[n_steps - 1], SSM_SEGMENTS)
    par_ref[0] = jnp.ones_like(ab_re)
    pai_ref[0] = jnp.zeros_like(ab_re)
    for k in range(SSM_SEGMENTS):
        par_ref[k + 1] = a[k][0]
        pai_ref[k + 1] = a[k][1]


def _ssm_prep(lam_re, lam_im, log_step, b_re, b_im, n_steps):
    g, p = lam_re.shape
    w = b_re.shape[2]
    args = (lam_re, lam_im, log_step[:, None], jnp.swapaxes(b_re, 1, 2), jnp.swapaxes(b_im, 1, 2))
    return pl.pallas_call(
        _ssm_prep_body,
        out_shape=[jax.ShapeDtypeStruct((n_steps, g, p), F32)] * 2
        + [jax.ShapeDtypeStruct((SSM_SEGMENTS + 1, g, p), F32)] * 2
        + [jax.ShapeDtypeStruct((g, w, p), F32)] * 2,
        name="ssm_prep",
    )(*args)


def _ssm_input(x_ref, m_ref, g_ref, bbr_ref, bbi_ref, xr_ref, xi_ref):
    x = x_ref[...]
    u = _modulate(x, g_ref[0:1], m_ref[0], m_ref[1])
    ub = u.astype(BF16)
    n_kb, kin, kout = bbr_ref.shape
    for kb in range(n_kb):
        xr_ref[:, kb * kout:(kb + 1) * kout] = _dot(ub[:, kb * kin:(kb + 1) * kin], bbr_ref[kb])
        xi_ref[:, kb * kout:(kb + 1) * kout] = _dot(ub[:, kb * kin:(kb + 1) * kin], bbi_ref[kb])
    return x, u


def _ssm_readout(xr_ref, xi_ref, cr_ref, ci_ref):
    n_kb, kin, _ = cr_ref.shape
    return [_dot(xr_ref[:, kb * kin:(kb + 1) * kin], cr_ref[kb]) - _dot(xi_ref[:, kb * kin:(kb + 1) * kin], ci_ref[kb])
            for kb in range(n_kb)]


def _ssm_output(x, u, y, m_ref, g_ref, d_ref, w1_ref, b1_ref, w2_ref, b2_ref, o_ref):
    gl = jax.nn.gelu(y + d_ref[...] * u)
    out = (_dot(gl, w1_ref[...]) + b1_ref[...]) * jax.nn.sigmoid(_dot(gl, w2_ref[...]) + b2_ref[...])
    o_ref[...] = x + m_ref[2] * _rms(out, g_ref[1:2])


def _ssm_prompt_body(x_ref, m_ref, g_ref, bbr_ref, bbi_ref, pwr_ref, pwi_ref, cr_ref, ci_ref, d_ref,
                     w1_ref, b1_ref, w2_ref, b2_ref, par_ref, pai_ref, o_ref, sr_ref, si_ref,
                     perm_ref, xr_ref, xi_ref, car_ref, cai_ref, *, lane_block):
    n_seg = SSM_SEGMENTS

    @pl.when(pl.program_id(1) == 0)
    def _():
        car_ref[...] = jnp.zeros_like(car_ref)
        cai_ref[...] = jnp.zeros_like(cai_ref)

    rows, n_state = xr_ref.shape
    n_steps = rows // n_seg
    n_tiles, _, lanes = perm_ref.shape
    x = x_ref[...]
    u = _modulate(x, g_ref[0:1], m_ref[0], m_ref[1])
    for c in range(n_tiles):
        perm_ref[c] = u[:, c * lanes:(c + 1) * lanes]
    ub = jnp.concatenate(
        [jnp.concatenate([perm_ref[c, pl.ds(s, n_seg, stride=n_steps), :] for s in range(n_steps)], axis=0)
         for c in range(n_tiles)], axis=1).astype(BF16)
    n_kb, kin, kout = bbr_ref.shape
    for kb in range(n_kb):
        xr_ref[:, kb * kout:(kb + 1) * kout] = _dot(ub[:, kb * kin:(kb + 1) * kin], bbr_ref[kb])
        xi_ref[:, kb * kout:(kb + 1) * kout] = _dot(ub[:, kb * kin:(kb + 1) * kin], bbi_ref[kb])

    row = lax.broadcasted_iota(jnp.int32, (n_seg, lane_block), 0)

    def shifted(v, s):
        return jnp.where(row >= s, pltpu.roll(v, s, axis=0), 0.0)

    for cb in range(n_state // lane_block):
        cols = slice(cb * lane_block, (cb + 1) * lane_block)
        ar, ai = pwr_ref[0:1, cols], pwi_ref[0:1, cols]
        vr = vi = jnp.zeros((n_seg, lane_block), F32)
        for s in range(n_steps):
            dr, di = _cmul(ar, ai, vr, vi)
            vr = xr_ref[s * n_seg:(s + 1) * n_seg, cols] + dr
            vi = xi_ref[s * n_seg:(s + 1) * n_seg, cols] + di
            xr_ref[s * n_seg:(s + 1) * n_seg, cols] = vr
            xi_ref[s * n_seg:(s + 1) * n_seg, cols] = vi
        for s in (1, 2, 4):
            dr, di = _cmul(par_ref[s:s + 1, cols], pai_ref[s:s + 1, cols], shifted(vr, s), shifted(vi, s))
            vr, vi = vr + dr, vi + di
        in_r, in_i = car_ref[:, cols], cai_ref[:, cols]
        dr, di = _cmul(par_ref[0:n_seg, cols], pai_ref[0:n_seg, cols], in_r, in_i)
        seg_r, seg_i = shifted(vr, 1) + dr, shifted(vi, 1) + di
        dr, di = _cmul(par_ref[n_seg:n_seg + 1, cols], pai_ref[n_seg:n_seg + 1, cols], in_r, in_i)
        car_ref[:, cols] = jnp.broadcast_to(vr[n_seg - 1:n_seg], vr.shape) + dr
        cai_ref[:, cols] = jnp.broadcast_to(vi[n_seg - 1:n_seg], vi.shape) + di
        for s in range(n_steps):
            dr, di = _cmul(pwr_ref[s:s + 1, cols], pwi_ref[s:s + 1, cols], seg_r, seg_i)
            xr_ref[s * n_seg:(s + 1) * n_seg, cols] += dr
            xi_ref[s * n_seg:(s + 1) * n_seg, cols] += di
    sr_ref[...] = car_ref[0:1, :]
    si_ref[...] = cai_ref[0:1, :]
    y_perm = jnp.concatenate(_ssm_readout(xr_ref, xi_ref, cr_ref, ci_ref), axis=1)
    for c in range(n_tiles):
        perm_ref[c] = y_perm[:, c * lanes:(c + 1) * lanes]
    y = jnp.concatenate(
        [jnp.concatenate([perm_ref[c, pl.ds(seg, n_steps, stride=n_seg), :] for seg in range(n_seg)], axis=0)
         for c in range(n_tiles)], axis=1)
    _ssm_output(x, u, y, m_ref, g_ref, d_ref, w1_ref, b1_ref, w2_ref, b2_ref, o_ref)


def _ssm_sample_body(x_ref, m_ref, g_ref, bbr_ref, bbi_ref, pwr_ref, pwi_ref, cr_ref, ci_ref, d_ref,
                     w1_ref, b1_ref, w2_ref, b2_ref, hr_ref, hi_ref, o_ref, sr_ref, si_ref,
                     xr_ref, xi_ref):
    x, u = _ssm_input(x_ref, m_ref, g_ref, bbr_ref, bbi_ref, xr_ref, xi_ref)
    dr, di = _cmul(pwr_ref[0:1, :], pwi_ref[0:1, :], hr_ref[...], hi_ref[...])
    xr_ref[...] = xr_ref[...] + dr
    xi_ref[...] = xi_ref[...] + di
    sr_ref[...] = xr_ref[...]
    si_ref[...] = xi_ref[...]
    y = jnp.concatenate(_ssm_readout(xr_ref, xi_ref, cr_ref, ci_ref), axis=1)
    _ssm_output(x, u, y, m_ref, g_ref, d_ref, w1_ref, b1_ref, w2_ref, b2_ref, o_ref)


def _ssm_tables(lam_re, lam_im, b_re, b_im, c_re, c_im, log_step, n_steps):
    g, p = lam_re.shape
    w = b_re.shape[2]
    pwr, pwi, par, pai, bbr, bbi = _ssm_prep(lam_re, lam_im, log_step, b_re, b_im, n_steps)
    gb = 256 // w
    eye = jnp.eye(gb, dtype=F32)

    def bd_in(a):
        return jnp.einsum("kgip,gh->kgihp", a.reshape(g // gb, gb, w, p), eye).reshape(g // gb, gb * w, gb * p)

    def bd_out(a):
        return jnp.einsum("kgip,gh->kgphi", a.reshape(g // gb, gb, w, p), eye).reshape(g // gb, gb * p, gb * w)

    tables = (bd_in(bbr).astype(BF16), bd_in(bbi).astype(BF16), pwr.reshape(n_steps, g * p),
              pwi.reshape(n_steps, g * p), bd_out(c_re).astype(BF16), bd_out(c_im).astype(BF16))
    return tables, (par.reshape(-1, g * p), pai.reshape(-1, g * p))


def _ssm_prompt(x, mod, g, tables, seg_tables, d_skip, w1, b1, w2, b2, n_seq, tl):
    n, d = x.shape
    t = n // n_seq
    n_state = tables[2].shape[1]
    assert tl == SSM_SEGMENTS * tables[2].shape[0]
    consts = tables + (d_skip[None], w1, b1[None], w2, b2[None]) + seg_tables
    row_spec = pl.BlockSpec((tl, d), lambda b, c: (b * (t // tl) + c, 0))
    st_spec = pl.BlockSpec((None, 1, n_state), lambda b, c: (b, 0, 0))
    return pl.pallas_call(
        functools.partial(_ssm_prompt_body, lane_block=1024),
        grid=(n_seq, t // tl),
        in_specs=[row_spec, pl.BlockSpec((6, None, 1, d), lambda b, c: (0, b, 0, 0)), _const_spec(g, 2)]
        + [_const_spec(a, 2) for a in consts],
        out_specs=[row_spec, st_spec, st_spec],
        out_shape=[jax.ShapeDtypeStruct((n, d), F32), jax.ShapeDtypeStruct((n_seq, 1, n_state), F32),
                   jax.ShapeDtypeStruct((n_seq, 1, n_state), F32)],
        scratch_shapes=[pltpu.VMEM((d // V7X_LANES, tl, V7X_LANES), F32),
                        pltpu.VMEM((tl, n_state), F32), pltpu.VMEM((tl, n_state), F32),
                        pltpu.VMEM((SSM_SEGMENTS, n_state), F32), pltpu.VMEM((SSM_SEGMENTS, n_state), F32)],
        compiler_params=_params("arbitrary", "arbitrary"),
        name="ssm_prompt",
    )(x, mod, g, *consts)


def _ssm_sample(x, mod, g, tables, d_skip, w1, b1, w2, b2, h_re, h_im):
    n, d = x.shape
    n_state = tables[2].shape[1]
    consts = tables + (d_skip[None], w1, b1[None], w2, b2[None], h_re, h_im)
    full = pl.BlockSpec((n, d), lambda i: (0, 0))
    st = pl.BlockSpec((n, n_state), lambda i: (0, 0))
    return pl.pallas_call(
        _ssm_sample_body,
        grid=(1,),
        in_specs=[full, _mod_spec(mod, 1), _const_spec(g)] + [_const_spec(a) for a in consts],
        out_specs=[full, st, st],
        out_shape=[jax.ShapeDtypeStruct((n, d), F32), jax.ShapeDtypeStruct((n, n_state), F32),
                   jax.ShapeDtypeStruct((n, n_state), F32)],
        scratch_shapes=[pltpu.VMEM((n, n_state), F32), pltpu.VMEM((n, n_state), F32)],
        compiler_params=_params("arbitrary"),
        name="ssm_sample",
    )(x, mod, g, *consts)


def _nsa_proj_body(x_ref, m_ref, g_ref, w_ref, q_ref, kc_ref, ks_ref, kw_ref, gt_ref, ksb_ref, kwb_ref, *t_refs):
    h = _modulate(x_ref[...], g_ref[0:1], m_ref[0], m_ref[1]).astype(BF16)
    qc, kc = q_ref.shape[1], kc_ref.shape[1]
    q_ref[...] = (_dot(h, w_ref[:, :qc]) * (NSA_HEAD_DIM ** -0.5)).astype(BF16)
    kc_ref[...] = _dot(h, w_ref[:, qc:qc + kc])
    ks = _dot(h, w_ref[:, qc + kc:qc + 2 * kc])
    kw = _dot(h, w_ref[:, qc + 2 * kc:qc + 3 * kc])
    ks_ref[...] = ks
    kw_ref[...] = kw
    ksb_ref[...] = ks.astype(BF16)
    kwb_ref[...] = kw.astype(BF16)
    gt_ref[...] = jax.nn.sigmoid(_dot(h, w_ref[:, qc + 3 * kc:]))
    for t_ref, rows in zip(t_refs, (kc_ref[...], ks, kw)):
        t_ref[...] = rows.T


def _nsa_proj(x, mod, g, w_in, tm, n_seq=None):
    n, d = x.shape
    n_tiles = n // tm
    kc = 2 * NSA_KV_HEADS * NSA_HEAD_DIM
    ng = (w_in.shape[1] - d - 3 * kc)
    widths = [(d, BF16), (kc, F32), (kc, F32), (kc, F32), (ng, F32), (kc, BF16), (kc, BF16)]
    out_specs = [pl.BlockSpec((tm, w), lambda i: (i, 0)) for w, _ in widths]
    out_shape = [jax.ShapeDtypeStruct((n, w), dt) for w, dt in widths]
    if n_seq is not None:
        tps = n_tiles // n_seq
        out_specs += [pl.BlockSpec((None, kc, tm), lambda i: (i // tps, 0, i % tps))] * 3
        out_shape += [jax.ShapeDtypeStruct((n_seq, kc, n // n_seq), F32)] * 3
    return pl.pallas_call(
        _nsa_proj_body,
        grid=(n_tiles,),
        in_specs=[pl.BlockSpec((tm, d), lambda i: (i, 0)), _mod_spec(mod, n_tiles), _const_spec(g),
                  _const_spec(w_in)],
        out_specs=out_specs,
        out_shape=out_shape,
        compiler_params=_params("arbitrary"),
        name="nsa_proj",
    )(x, mod, g, w_in)


def _compress_step(x_of, ls, pe_ref, w1_ref, w2_ref, o_ref, acc_ref, n_l):
    dh = NSA_HEAD_DIM
    hid = w1_ref.shape[3]
    n_zg = 2 * NSA_KV_HEADS

    @pl.when(ls == 0)
    def _():
        acc_ref[...] = jnp.zeros_like(acc_ref)

    for ll in range(n_l):
        l = ls * n_l + ll
        xb = (x_of(ll) + pe_ref[pl.ds(l, 1), :]).astype(BF16)
        for zg in range(n_zg):
            acc_ref[:, zg * hid:(zg + 1) * hid] += _dot(xb[:, zg * dh:(zg + 1) * dh], w1_ref[zg // NSA_KV_HEADS, l])

    @pl.when(ls == pl.num_programs(1) - 1)
    def _():
        a = acc_ref[...]
        hidv = (a * jax.nn.sigmoid(a)).astype(BF16)
        for zg in range(n_zg):
            o_ref[:, zg * dh:(zg + 1) * dh] = _dot(hidv[:, zg * hid:(zg + 1) * hid], w2_ref[zg // NSA_KV_HEADS])


def _compress_prompt_body(x_ref, pe_ref, w1_ref, w2_ref, o_ref, acc_ref, *, n_l):
    _compress_step(lambda ll: x_ref[:, ll, :], pl.program_id(1), pe_ref, w1_ref, w2_ref, o_ref, acc_ref, n_l)


def _compress_sample_body(pt_ref, cache_ref, pe_ref, w1_ref, w2_ref, o_ref, buf_ref, sem, acc_ref, *,
                          pages_per_step, d_tiles):
    pg, ds = pl.program_id(0), pl.program_id(1)
    n_ds = pl.num_programs(1)
    n_zg = 2 * NSA_KV_HEADS
    dh = NSA_HEAD_DIM
    per_page = o_ref.shape[0]
    hid = w1_ref.shape[3] // per_page
    step = pg * n_ds + ds
    slot = step % 2

    def page_copy(p, at_step, at_slot):
        page = pt_ref[(at_step // n_ds) * pages_per_step + p]
        return pltpu.make_async_copy(cache_ref.at[page, :, pl.ds((at_step % n_ds) * d_tiles, d_tiles)],
                                     buf_ref.at[at_slot, :, :, pl.ds(p * 8, 8), :], sem.at[at_slot])

    def start_all(at_step, at_slot):
        def start(p, c):
            page_copy(p, at_step, at_slot).start()
            return c

        lax.fori_loop(0, pages_per_step, start, 0, unroll=8)

    @pl.when(step == 0)
    def _():
        start_all(step, slot)

    @pl.when(step + 1 < pl.num_programs(0) * n_ds)
    def _():
        start_all(step + 1, 1 - slot)

    @pl.when(ds == 0)
    def _():
        acc_ref[...] = jnp.zeros_like(acc_ref)

    def wait(p, c):
        page_copy(p, step, slot).wait()
        return c

    lax.fori_loop(0, pages_per_step, wait, 0)
    for zg in range(n_zg):
        z = zg // NSA_KV_HEADS
        for dt in range(d_tiles):
            for dd in range(8):
                d = dt * 8 + dd
                x = buf_ref[slot, zg, dt, pl.ds(dd, pages_per_step, stride=8), :] + pe_ref[zg, d:d + 1, :]
                acc_ref[zg] += _dot(x, w1_ref[z, d])

    @pl.when(ds == pl.num_programs(1) - 1)
    def _():
        for zg in range(n_zg):
            a = acc_ref[zg]
            hidv = (a * jax.nn.sigmoid(a)).astype(BF16)
            for n in range(per_page):
                o_ref[n, :, zg * dh:(zg + 1) * dh] = _dot(hidv[:, n * hid:(n + 1) * hid], w2_ref[zg // NSA_KV_HEADS])


def _pe_rows(pe):
    blk = pe.shape[1]
    return jnp.broadcast_to(pe.transpose(1, 0, 2)[:, :, None, :], (blk, 2, NSA_KV_HEADS, pe.shape[2])).reshape(blk, -1)


_COMPRESS_ROWS = 16


def _compress_prompt(kc, pe, w1, w2, nbt):
    n, c = kc.shape
    nblk = n // NSA_BLOCK
    n_l = _COMPRESS_ROWS
    x3 = kc.reshape(nblk, NSA_BLOCK, c)
    return pl.pallas_call(
        functools.partial(_compress_prompt_body, n_l=n_l),
        grid=(nblk // nbt, NSA_BLOCK // n_l),
        in_specs=[pl.BlockSpec((nbt, n_l, c), lambda i, l: (i, l, 0)), _const_spec(pe, 2), _const_spec(w1, 2),
                  _const_spec(w2, 2)],
        out_specs=pl.BlockSpec((nbt, c), lambda i, l: (i, 0)),
        out_shape=jax.ShapeDtypeStruct((nblk, c), F32),
        scratch_shapes=[pltpu.VMEM((nbt, 2 * NSA_KV_HEADS * w1.shape[3]), F32)],
        compiler_params=_params("arbitrary", "arbitrary"),
        name="compress_prompt",
    )(x3, pe, w1, w2)


def _rows_on_lanes(cache):
    return jnp.transpose(cache, (0, 2, 3, 4, 1))


def _compress_sample(page_table, cache, pe, w1, w2, pages_per_step):
    n_pool, page = cache.shape[:2]
    dh = NSA_HEAD_DIM
    n_zg = 2 * NSA_KV_HEADS
    c = n_zg * dh
    per_page = page // NSA_BLOCK
    hid = w1.shape[3]
    d_tiles = 2
    n_pages_total = page_table.size
    cache_t = _rows_on_lanes(cache).reshape(n_pool, n_zg, dh // 8, 8, page)
    pe_t = jnp.tile(jnp.repeat(jnp.swapaxes(pe, 1, 2), NSA_KV_HEADS, axis=0), (1, 1, per_page))
    w1_t = jnp.einsum("zlde,nm->zdnlme", w1, jnp.eye(per_page, dtype=w1.dtype)).reshape(2, dh, page, per_page * hid)
    grid_spec = pltpu.PrefetchScalarGridSpec(
        num_scalar_prefetch=1,
        grid=(n_pages_total // pages_per_step, dh // (8 * d_tiles)),
        in_specs=[pl.BlockSpec(memory_space=pl.ANY),
                  pl.BlockSpec((n_zg, 8 * d_tiles, page), lambda i, s, pt: (0, s, 0)),
                  pl.BlockSpec((2, 8 * d_tiles, page, per_page * hid), lambda i, s, pt: (0, s, 0, 0)),
                  pl.BlockSpec(w2.shape, lambda i, s, pt: (0, 0, 0))],
        out_specs=pl.BlockSpec((per_page, pages_per_step, c), lambda i, s, pt: (0, i, 0)),
        scratch_shapes=[pltpu.VMEM((2, n_zg, d_tiles, pages_per_step * 8, page), F32), pltpu.SemaphoreType.DMA((2,)),
                        pltpu.VMEM((n_zg, pages_per_step, per_page * hid), F32)],
    )
    out = pl.pallas_call(
        functools.partial(_compress_sample_body, pages_per_step=pages_per_step, d_tiles=d_tiles),
        grid_spec=grid_spec,
        out_shape=jax.ShapeDtypeStruct((per_page, n_pages_total, c), F32),
        compiler_params=_params("arbitrary", "arbitrary"),
        name="compress_sample",
    )(page_table.reshape(-1), cache_t, pe_t, w1_t, w2)
    return jnp.swapaxes(out, 0, 1).reshape(n_pages_total * per_page, c)


def _stack_heads(q, grp):
    dh = NSA_HEAD_DIM
    rep = q.shape[1] // (NSA_KV_HEADS * dh)
    base = grp * rep * dh
    return jnp.concatenate([q[:, base + r * dh:base + (r + 1) * dh] for r in range(rep)], axis=0)


def _cmp_branch(qs, cmpv, grp, t_row, rep):
    dh = NSA_HEAD_DIM
    kv = NSA_KV_HEADS * dh
    kc = cmpv[:, grp * dh:(grp + 1) * dh]
    vc = cmpv[:, kv + grp * dh:kv + (grp + 1) * dh]
    s = _dot_nt(qs, kc)
    n = lax.broadcasted_iota(jnp.int32, s.shape, 1)
    mask = (n + 1) * NSA_BLOCK <= t_row + 1
    s = jnp.where(mask, s, NEG_BIG)
    e = jnp.where(mask, jnp.exp(s - jnp.max(s, axis=-1, keepdims=True)), 0.0)
    p = e / jnp.maximum(jnp.sum(e, axis=-1, keepdims=True), 1e-30)
    o = _dot(p, vc)
    t = p.shape[0] // rep
    imp = p[0:t]
    for r in range(1, rep):
        imp = imp + p[r * t:(r + 1) * t]
    return o, imp


def _topk_mask(score, axis):
    idx = lax.broadcasted_iota(jnp.int32, score.shape, axis).astype(F32)
    n = float(score.shape[axis])
    x = score
    for _ in range(NSA_TOPK):
        m = jnp.max(x, axis=axis, keepdims=True)
        first = jnp.min(jnp.where(x == m, idx, n), axis=axis, keepdims=True)
        x = jnp.where(idx == first, -jnp.inf, x)
    return (x == -jnp.inf).astype(F32)


def _cmpattn_prompt_body(q_ref, cmp_ref, o_ref, sel_ref):
    tq = q_ref.shape[0]
    nb = cmp_ref.shape[0]
    dh = NSA_HEAD_DIM
    rep = q_ref.shape[1] // (NSA_KV_HEADS * dh)
    t0 = pl.program_id(1) * tq
    q = q_ref[...]
    cmpv = cmp_ref[...]
    t_row = t0 + lax.broadcasted_iota(jnp.int32, (rep * tq, 1), 0) % tq
    blk = lax.broadcasted_iota(jnp.int32, (nb, tq), 0)
    jt = (t0 + lax.broadcasted_iota(jnp.int32, (nb, tq), 1)) // NSA_BLOCK
    forced = (blk == 0) | (blk == jt) | (blk == jt - 1)
    for grp in range(NSA_KV_HEADS):
        o, imp = _cmp_branch(_stack_heads(q, grp), cmpv, grp, t_row, rep)
        for r in range(rep):
            h = grp * rep + r
            o_ref[:, h * dh:(h + 1) * dh] = o[r * tq:(r + 1) * tq]
        score = jnp.where(blk <= jt, jnp.where(forced, SEL_FORCE, imp.T), SEL_MASKED)
        sel = _topk_mask(score, 0) * (score > 0.5 * SEL_MASKED).astype(F32)
        sel_ref[:, grp * nb:(grp + 1) * nb] = jnp.where(sel.T > 0.5, 0.0, NEG_BIG).astype(BF16)


def _cmpattn_prompt(q, cmp, n_seq, tq):
    n, qc = q.shape
    t = n // n_seq
    nb = cmp.shape[0] // n_seq
    return pl.pallas_call(
        _cmpattn_prompt_body,
        grid=(n_seq, t // tq),
        in_specs=[pl.BlockSpec((tq, qc), lambda b, i: (b * (t // tq) + i, 0)),
                  pl.BlockSpec((nb, cmp.shape[1]), lambda b, i: (b, 0))],
        out_specs=[pl.BlockSpec((tq, qc), lambda b, i: (b * (t // tq) + i, 0)),
                   pl.BlockSpec((tq, NSA_KV_HEADS * nb), lambda b, i: (b * (t // tq) + i, 0))],
        out_shape=[jax.ShapeDtypeStruct((n, qc), F32), jax.ShapeDtypeStruct((n, NSA_KV_HEADS * nb), BF16)],
        compiler_params=_params("arbitrary", "arbitrary"),
        name="cmpattn_prompt",
    )(q, cmp)


def _cmpattn_sample_body(q_ref, cmp_ref, o_ref, imp_ref, *, t_pos):
    dh = NSA_HEAD_DIM
    rep = q_ref.shape[1] // (NSA_KV_HEADS * dh)
    q = q_ref[...]
    cmpv = cmp_ref[...]
    t_row = jnp.full((rep, 1), t_pos, jnp.int32)
    for grp in range(NSA_KV_HEADS):
        o, imp = _cmp_branch(_stack_heads(q, grp), cmpv, grp, t_row, rep)
        for r in range(rep):
            h = grp * rep + r
            o_ref[:, h * dh:(h + 1) * dh] = o[r:r + 1]
        imp_ref[grp:grp + 1, :] = imp


def _select_sample_body(imp_ref, idx_ref, *, t_pos, n_cand):
    imp = imp_ref[...]
    n_rows, nb = imp.shape
    width = idx_ref.shape[1]
    lanes = ((n_cand + V7X_LANES - 1) // V7X_LANES) * V7X_LANES
    blk = lax.broadcasted_iota(jnp.int32, (1, lanes), 1)
    jt = t_pos // NSA_BLOCK
    forced = (blk == 0) | (blk == jt) | (blk == jt - 1)
    col = lax.broadcasted_iota(jnp.int32, (1, width), 1)
    blk_f = blk.astype(F32)
    if lanes > nb:
        imp = jnp.concatenate([imp, jnp.zeros((n_rows, lanes - nb), F32)], axis=1)
    score = jnp.where(blk <= jt, jnp.where(forced, SEL_FORCE, imp), SEL_MASKED)
    x = jnp.where(blk < n_cand, score, -jnp.inf)
    out = jnp.full((n_rows, width), -1, jnp.int32)
    for k in range(NSA_TOPK):
        m = jnp.max(x, axis=1, keepdims=True)
        first = jnp.min(jnp.where(x == m, blk_f, float(lanes)), axis=1, keepdims=True)
        chosen = jnp.where(m > 0.5 * SEL_MASKED, first, -1.0).astype(jnp.int32)
        out = jnp.where(col == k, chosen, out)
        x = jnp.where(blk_f == first, -jnp.inf, x)
    idx_ref[...] = out


def _cmpattn_sample(q, cmp, t_pos, n_cand):
    n_seq, qc = q.shape
    nb = cmp.shape[0] // n_seq
    o_cmp, imp = pl.pallas_call(
        functools.partial(_cmpattn_sample_body, t_pos=t_pos),
        grid=(n_seq,),
        in_specs=[pl.BlockSpec((None, 1, qc), lambda b: (b, 0, 0)), pl.BlockSpec((nb, cmp.shape[1]), lambda b: (b, 0))],
        out_specs=[pl.BlockSpec((None, 1, qc), lambda b: (b, 0, 0)),
                   pl.BlockSpec((None, NSA_KV_HEADS, nb), lambda b: (b, 0, 0))],
        out_shape=[jax.ShapeDtypeStruct((n_seq, 1, qc), F32), jax.ShapeDtypeStruct((n_seq, NSA_KV_HEADS, nb), F32)],
        compiler_params=_params("arbitrary"),
        name="cmpattn_sample",
    )(q[:, None, :], cmp)
    sel_idx = pl.pallas_call(
        functools.partial(_select_sample_body, t_pos=t_pos, n_cand=n_cand),
        out_shape=jax.ShapeDtypeStruct((n_seq * NSA_KV_HEADS, 128), jnp.int32),
        name="select_sample",
    )(imp.reshape(n_seq * NSA_KV_HEADS, nb))
    return o_cmp, sel_idx.reshape(n_seq, NSA_KV_HEADS, 128)


def _attn_prompt_body(q_ref, sel_ref, ks_ref, kw_ref, oslc_ref, owin_ref, qa_ref, m_ref, acc_ref, *, tk, n_sub):
    tq = q_ref.shape[0]
    ts = tq // n_sub
    dh = NSA_HEAD_DIM
    slab = 2 * dh
    kv = NSA_KV_HEADS * dh
    rep = q_ref.shape[1] // kv
    rows, rows_s = rep * tq, rep * ts
    nb = sel_ref.shape[1] // NSA_KV_HEADS
    t0 = pl.program_id(1) * tq
    lane = lax.broadcasted_iota(jnp.int32, (ts, slab), 1)

    for grp in range(NSA_KV_HEADS):
        off = (grp % 2) * dh
        for sub in range(n_sub):
            tok = slice(sub * ts, (sub + 1) * ts)
            parts = []
            for r in range(rep):
                h = grp * rep + r
                x = q_ref[tok, (h // 2) * slab:(h // 2 + 1) * slab].astype(F32)
                if h % 2 != grp % 2:
                    x = pltpu.roll(x, dh, axis=1)
                parts.append(jnp.where((lane >= off) & (lane < off + dh), x, 0.0))
            rr = slice(sub * rows_s, (sub + 1) * rows_s)
            qa_ref[grp, rr, 0:slab] = jnp.concatenate(parts, axis=0).astype(BF16)
            qa_ref[grp, rr, slab:slab + nb] = jnp.concatenate([sel_ref[tok, grp * nb:(grp + 1) * nb]] * rep, axis=0)
    m_ref[...] = jnp.full(m_ref.shape, 0.1 * NEG_BIG, F32)
    acc_ref[...] = jnp.zeros_like(acc_ref)

    def flash_tile(r0, n_rows, k0, width, bias):
        rr = slice(r0, r0 + n_rows)
        key_blk = (k0 + lax.broadcasted_iota(jnp.int32, (width, nb), 0)) // NSA_BLOCK
        onehot = (key_blk == lax.broadcasted_iota(jnp.int32, (width, nb), 1)).astype(BF16)
        upper_half = lax.broadcasted_iota(jnp.int32, (width, slab), 1) >= dh
        for grp in range(NSA_KV_HEADS):
            pair = (grp // 2) * slab
            k_aug = jnp.concatenate([ks_ref[pl.ds(k0, width), pair:pair + slab], onehot], axis=1)
            s = _dot_nt(qa_ref[grp, rr], k_aug)
            if bias is not None:
                s = (s.reshape(rep, n_rows // rep, width) + bias[None]).reshape(n_rows, width)
            cols = [s[:, c * slab:(c + 1) * slab] for c in range(width // slab)]
            mx = functools.reduce(jnp.maximum, cols)
            m_old = m_ref[grp, rr]
            m_new = jnp.maximum(m_old, jnp.max(mx, axis=-1, keepdims=True))
            alpha = jnp.exp(m_old - m_new)
            p = jnp.concatenate([jnp.exp(c - m_new).astype(BF16) for c in cols], axis=1)
            v = ks_ref[pl.ds(k0, width), kv + pair:kv + pair + slab]
            v = jnp.where(upper_half == (grp % 2 == 1), v, jnp.ones_like(v))
            acc_ref[grp, rr] = alpha * acc_ref[grp, rr] + _dot(p, v)
            m_ref[grp, rr] = m_new

    n_full = t0 // tk

    def full_tile(j, carry):
        flash_tile(0, rows, pl.multiple_of(j * tk, tk), tk, None)
        return carry

    lax.fori_loop(0, n_full, full_tile, 0)

    win_len = NSA_WINDOW + ts
    for sub in range(n_sub):
        ts0 = t0 + sub * ts
        r0 = sub * rows_s
        t_col = ts0 + lax.broadcasted_iota(jnp.int32, (ts, 1), 0)
        j_own = ts0 // tk
        if sub > 0:

            def before_own(j, carry, r0=r0):
                flash_tile(r0, rows_s, pl.multiple_of(j * tk, tk), tk, None)
                return carry

            lax.fori_loop(n_full, j_own, before_own, 0)
        k0 = pl.multiple_of(j_own * tk, tk)
        width = (sub + 1) * ts if tq == tk else tk
        late = k0 + lax.broadcasted_iota(jnp.int32, (1, width), 1) > t_col
        flash_tile(r0, rows_s, k0, width, jnp.where(late, NEG_BIG, 0.0))

        w0 = pl.multiple_of(jnp.maximum(ts0 - NSA_WINDOW, 0), ts)
        wpos = w0 + lax.broadcasted_iota(jnp.int32, (1, win_len), 1)
        win_bias = jnp.where((wpos <= t_col) & (wpos >= t_col - NSA_WINDOW), 0.0, NEG_BIG)
        tok = slice(sub * ts, (sub + 1) * ts)
        for grp in range(NSA_KV_HEADS):
            off = (grp % 2) * dh
            pair = (grp // 2) * slab
            acc = acc_ref[grp, r0:r0 + rows_s]
            o = acc / jnp.maximum(acc[:, dh - off:dh - off + 1], 1e-30)
            for r in range(rep):
                h = grp * rep + r
                oslc_ref[tok, h * dh:(h + 1) * dh] = o[r * ts:(r + 1) * ts, off:off + dh]

            s = _dot_nt(qa_ref[grp, r0:r0 + rows_s, 0:slab], kw_ref[pl.ds(w0, win_len), pair:pair + slab])
            s = s.reshape(rep, ts, win_len) + win_bias[None]
            m = jnp.maximum(jnp.max(s, axis=-1, keepdims=True), 0.1 * NEG_BIG)
            e = jnp.exp(s - m)
            o = _dot(e.reshape(rows_s, win_len), kw_ref[pl.ds(w0, win_len), kv + pair:kv + pair + slab])
            o = o / jnp.maximum(jnp.sum(e, axis=-1, keepdims=True).reshape(rows_s, 1), 1e-30)
            for r in range(rep):
                h = grp * rep + r
                owin_ref[tok, h * dh:(h + 1) * dh] = o[r * ts:(r + 1) * ts, off:off + dh]


def _attn_prompt(q, sel, ksb, kwb, n_seq):
    n, qc = q.shape
    t = n // n_seq
    dh = NSA_HEAD_DIM
    rep = qc // (NSA_KV_HEADS * dh)
    tq, ts, tk = ATTN_Q_TILE, ATTN_Q_SUB, ATTN_KV_TILE
    assert t % tk == 0 and t % tq == 0 and tq % ts == 0 and tk % ts == 0
    assert t >= NSA_WINDOW + ts and NSA_WINDOW % ts == 0
    tile = pl.BlockSpec((tq, qc), lambda b, i: (b * (t // tq) + i, 0))
    seq = pl.BlockSpec((t, ksb.shape[1]), lambda b, i: (b, 0), pipeline_mode=pl.Buffered(1))
    return pl.pallas_call(
        functools.partial(_attn_prompt_body, tk=tk, n_sub=tq // ts),
        grid=(n_seq, t // tq),
        in_specs=[tile, pl.BlockSpec((tq, sel.shape[1]), lambda b, i: (b * (t // tq) + i, 0)), seq, seq],
        out_specs=[tile, tile],
        out_shape=[jax.ShapeDtypeStruct((n, qc), F32), jax.ShapeDtypeStruct((n, qc), F32)],
        scratch_shapes=[pltpu.VMEM((NSA_KV_HEADS, rep * tq, 2 * dh + sel.shape[1] // NSA_KV_HEADS), BF16)]
        + [pltpu.VMEM((NSA_KV_HEADS, rep * tq, 2 * dh), F32)] * 2,
        compiler_params=_params("arbitrary", "arbitrary"),
        name="attn_prompt",
    )(q, sel, ksb, kwb)


def _softmax_with_new_key(s, ok, s_new, new_ok):
    s = jnp.where(ok, s, NEG_BIG)
    s_new = jnp.where(new_ok, s_new, NEG_BIG)
    m = jnp.maximum(jnp.max(s, axis=-1, keepdims=True), s_new)
    e = jnp.where(ok, jnp.exp(s - m), 0.0)
    e_new = jnp.where(new_ok, jnp.exp(s_new - m), 0.0)
    return e, e_new, jnp.maximum(jnp.sum(e, axis=-1, keepdims=True) + e_new, 1e-30)


def _bf16_round(x):
    return x.astype(BF16).astype(F32)


def _attn_sample_body(pt_ref, idx_ref, q_ref, ksn_ref, kwn_ref, win_ref, cache_ref, oslc_ref, owin_ref,
                      kbuf_ref, sem, *, t_pos, nb_past, n_pages):
    b = pl.program_id(0)
    dh = NSA_HEAD_DIM
    kv = NSA_KV_HEADS * dh
    rep = q_ref.shape[1] // kv
    n_sel = NSA_TOPK
    page = cache_ref.shape[4]
    per_page = page // NSA_BLOCK
    q = q_ref[...]

    def sel_index(grp, k):
        return idx_ref[(b * NSA_KV_HEADS + grp) * 128 + k]

    def in_pool(idx):
        return (idx >= 0) & (idx < nb_past)

    def page_copy(grp, k, idx):
        phys = pt_ref[b * n_pages + jnp.minimum(idx // per_page, n_pages - 1)]
        return pltpu.make_async_copy(cache_ref.at[phys, :, grp], kbuf_ref.at[grp, :, :, pl.ds(k * page, page)], sem)

    for grp in range(NSA_KV_HEADS):
        for k in range(n_sel):
            idx = sel_index(grp, k)

            @pl.when(in_pool(idx))
            def _():
                page_copy(grp, k, idx).start()

            @pl.when(jnp.logical_not(in_pool(idx)))
            def _():
                kbuf_ref[grp, :, :, k * page:(k + 1) * page] = jnp.zeros((2, dh, page), F32)

    for grp in range(NSA_KV_HEADS):
        for k in range(n_sel):
            idx = sel_index(grp, k)

            @pl.when(in_pool(idx))
            def _():
                page_copy(grp, k, idx).wait()

    lane = lax.broadcasted_iota(jnp.int32, (1, n_sel * page), 1)
    wb = win_ref.shape[3]
    wpos = t_pos - wb + lax.broadcasted_iota(jnp.int32, (1, wb), 1)
    win_ok = (wpos <= t_pos) & (wpos >= t_pos - NSA_WINDOW) & (wpos >= 0)
    for grp in range(NSA_KV_HEADS):
        qs = _stack_heads(q, grp)
        qf = qs.astype(F32)
        ok = jnp.zeros((1, n_sel * page), jnp.bool_)
        has_new = False
        for k in range(n_sel):
            idx = sel_index(grp, k)
            row = lane - k * page
            kpos = (idx // per_page) * page + row
            ok = ok | ((lane // page == k) & in_pool(idx) & (row // NSA_BLOCK == idx % per_page) & (kpos <= t_pos))
            has_new = has_new | (idx >= nb_past)
        new_ok = has_new & (nb_past * NSA_BLOCK <= t_pos)
        k_new = _bf16_round(ksn_ref[:, grp * dh:(grp + 1) * dh])
        v_new = _bf16_round(ksn_ref[:, kv + grp * dh:kv + (grp + 1) * dh])
        s_new = jnp.sum(qf * k_new, axis=-1, keepdims=True)
        e, e_new, den = _softmax_with_new_key(_dot(qs, kbuf_ref[grp, 0]), ok, s_new, new_ok)
        o = (_dot_nt(e, kbuf_ref[grp, 1]) + _bf16_round(e_new) * v_new) / den
        for r in range(rep):
            h = grp * rep + r
            oslc_ref[:, h * dh:(h + 1) * dh] = o[r:r + 1]

        k_new = _bf16_round(kwn_ref[:, grp * dh:(grp + 1) * dh])
        v_new = _bf16_round(kwn_ref[:, kv + grp * dh:kv + (grp + 1) * dh])
        s_new = jnp.sum(qf * k_new, axis=-1, keepdims=True)
        e, e_new, den = _softmax_with_new_key(_dot(qs, win_ref[0, grp]), win_ok, s_new, True)
        o = (_dot_nt(e, win_ref[1, grp]) + _bf16_round(e_new) * v_new) / den
        for r in range(rep):
            h = grp * rep + r
            owin_ref[:, h * dh:(h + 1) * dh] = o[r:r + 1]


def _attn_sample(page_table, sel_idx, q, ks_new, kw_new, win, cache, t_pos):
    n_seq, qc = q.shape
    n_pool, page = cache.shape[:2]
    dh = NSA_HEAD_DIM
    c = 2 * NSA_KV_HEADS * dh
    n_pages = page_table.shape[1]
    wb = win.shape[1]
    row3 = lambda w: pl.BlockSpec((None, 1, w), lambda b, pt, ix: (b, 0, 0))
    grid_spec = pltpu.PrefetchScalarGridSpec(
        num_scalar_prefetch=2,
        grid=(n_seq,),
        in_specs=[row3(qc), row3(c), row3(c),
                  pl.BlockSpec((None, 2, NSA_KV_HEADS, dh, wb), lambda b, pt, ix: (b, 0, 0, 0, 0)),
                  pl.BlockSpec(memory_space=pl.ANY)],
        out_specs=[row3(qc), row3(qc)],
        scratch_shapes=[pltpu.VMEM((NSA_KV_HEADS, 2, dh, NSA_TOPK * page), F32), pltpu.SemaphoreType.DMA(())],
    )
    win, cache = _rows_on_lanes(win), _rows_on_lanes(cache)
    return pl.pallas_call(
        functools.partial(_attn_sample_body, t_pos=t_pos, nb_past=t_pos // NSA_BLOCK, n_pages=n_pages),
        grid_spec=grid_spec,
        out_shape=[jax.ShapeDtypeStruct((n_seq, 1, qc), F32), jax.ShapeDtypeStruct((n_seq, 1, qc), F32)],
        compiler_params=_params("arbitrary"),
        name="attn_sample",
    )(page_table.reshape(-1), sel_idx.reshape(-1), q[:, None, :], ks_new[:, None, :], kw_new[:, None, :], win, cache)


def _nsa_merge_body(x_ref, m_ref, g_ref, oc_ref, os_ref, ow_ref, gt_ref, wout_ref, o_ref, om_ref):
    dh = NSA_HEAD_DIM
    n_heads = oc_ref.shape[1] // dh
    gt = gt_ref[...]
    for h in range(n_heads):
        c = slice(h * dh, (h + 1) * dh)
        o = (gt[:, h:h + 1] * oc_ref[:, c] + gt[:, n_heads + h:n_heads + h + 1] * os_ref[:, c]
             + gt[:, 2 * n_heads + h:2 * n_heads + h + 1] * ow_ref[:, c])
        om_ref[:, c] = o.astype(BF16)
    x = x_ref[...]
    o_ref[...] = x + m_ref[2] * _rms(_dot(om_ref[...], wout_ref[...]), g_ref[1:2])


def _nsa_merge(x, mod, g, o_cmp, o_slc, o_win, gates, w_out, tm):
    n, d = x.shape
    n_tiles = n // tm
    qc = o_cmp.shape[1]
    tile = lambda w: pl.BlockSpec((tm, w), lambda i: (i, 0))
    return pl.pallas_call(
        _nsa_merge_body,
        grid=(n_tiles,),
        in_specs=[tile(d), _mod_spec(mod, n_tiles), _const_spec(g), tile(qc), tile(qc), tile(qc),
                  tile(gates.shape[1]), _const_spec(w_out)],
        out_specs=tile(d),
        out_shape=jax.ShapeDtypeStruct((n, d), F32),
        scratch_shapes=[pltpu.VMEM((tm, qc), BF16)],
        compiler_params=_params("arbitrary"),
        name="nsa_merge",
    )(x, mod, g, o_cmp, o_slc, o_win, gates, w_out)


def _nsa_layer(xp, xs, mod_p, mod_s, g, n_seq, cache_cmp, cache_slc, cache_win, page_table,
               w_in, w1, w2, pe, w_out, tm):
    n_s = xs.shape[0]
    t = xp.shape[0] // n_seq
    page_size = cache_cmp.shape[1]
    past = page_table.shape[1] * page_size
    assert t % NSA_BLOCK == 0 and past % NSA_BLOCK == 0 and page_size % NSA_BLOCK == 0
    w_in_b, w_out_b = w_in.astype(BF16), w_out.astype(BF16)
    w1_b, w2_b = w1.astype(BF16), w2.astype(BF16)
    pe_rows = _pe_rows(pe)

    q, kc, _, _, gates, ksb, kwb, kc_t, ks_t, kw_t = _nsa_proj(xp, mod_p, g, w_in_b, tm, n_seq)
    cmp_p = _compress_prompt(kc, pe_rows, w1_b, w2_b, min(256, kc.shape[0] // NSA_BLOCK))
    o_cmp, sel = _cmpattn_prompt(q, cmp_p, n_seq, CMP_Q_TILE)
    o_slc, o_win = _attn_prompt(q, sel, ksb, kwb, n_seq)
    xp = _nsa_merge(xp, mod_p, g, o_cmp, o_slc, o_win, gates, w_out_b, tm)

    q_s, kc_s, ks_s, kw_s, gates_s, _, _ = _nsa_proj(xs, mod_s, g, w_in_b, n_s)
    cmp_s = _compress_sample(page_table, cache_cmp, pe.astype(F32), w1_b, w2_b, min(256, page_table.size))
    n_cand = -(-(past + 1) // NSA_BLOCK)
    o_cmp_s, sel_idx = _cmpattn_sample(q_s, cmp_s, past, n_cand)
    o_slc_s, o_win_s = _attn_sample(page_table, sel_idx, q_s, ks_s, kw_s, cache_win, cache_slc, past)
    xs = _nsa_merge(xs, mod_s, g, o_cmp_s.reshape(n_s, -1), o_slc_s.reshape(n_s, -1), o_win_s.reshape(n_s, -1),
                    gates_s, w_out_b, n_s)
    return xp, xs, (kc_t, ks_t, kw_t), (kc_s, ks_s, kw_s)


def kernel(x_prompt, x_sample, cache_nsa_cmp, cache_nsa_slc, cache_nsa_win, state_ssm, page_table, c_prompt, c_sample, w_mod, b_mod, norm_g, ffn_w_gate, ffn_w_up, ffn_w_down, gmlp_w_in, gmlp_b_in, gmlp_ln_g, gmlp_ln_b, gmlp_w_s, gmlp_b_s, gmlp_w_out, nsa_w_in, nsa_w_cmp1, nsa_w_cmp2, nsa_pe_cmp, nsa_w_out, ssm_lambda_re, ssm_lambda_im, ssm_b_re, ssm_b_im, ssm_c_re, ssm_c_im, ssm_d, ssm_log_step, ssm_w_glu1, ssm_b_glu1, ssm_w_glu2, ssm_b_glu2):
    n_seq, t, d = x_prompt.shape
    n_s, t_s, _ = x_sample.shape
    assert t_s == 1
    depth = w_mod.shape[0]
    tm = 512 if t % 512 == 0 else 256
    kv_shape = (2, NSA_KV_HEADS, NSA_HEAD_DIM)

    xp = x_prompt.reshape(n_seq * t, d)
    xs = x_sample.reshape(n_s, d)
    m_all = _adaln(jnp.concatenate([c_prompt, c_sample], axis=0), w_mod, b_mod)
    mods_p = m_all[:, :n_seq].reshape(depth, n_seq, 6, 1, d).transpose(0, 2, 1, 3, 4)
    mods_s = m_all[:, n_seq:].reshape(depth, n_s, 6, d).transpose(0, 2, 1, 3)[:, :, None]

    cmp_p, cmp_s, slc_p, slc_s, win_p, win_s, ssm_p, ssm_s, gv_s = [], [], [], [], [], [], [], [], []
    for i in range(depth):
        j = i // N_MIXERS
        mp, ms, g = mods_p[i], mods_s[i], norm_g[i]
        if i % N_MIXERS == 0:
            gw = (gmlp_w_in[j].astype(BF16), gmlp_b_in[j], gmlp_ln_g[j], gmlp_ln_b[j], gmlp_w_s[j], gmlp_b_s[j],
                  gmlp_w_out[j].astype(BF16))
            assert t % GMLP_CHUNK == 0 and gmlp_w_s.shape[2] == GMLP_CHUNK
            xp = _gmlp_prompt(xp, mp, g, *gw, tm)
            xs, v_new = _gmlp_sample(xs, ms, g, *gw)
            gv_s.append(v_new.reshape(n_s, 1, -1))
        elif i % N_MIXERS == 1:
            xp, xs, kv_p, kv_s = _nsa_layer(xp, xs, mp, ms, g, n_seq, cache_nsa_cmp[j], cache_nsa_slc[j],
                                            cache_nsa_win[j], page_table, nsa_w_in[j], nsa_w_cmp1[j],
                                            nsa_w_cmp2[j], nsa_pe_cmp[j], nsa_w_out[j], tm)
            rows_last = lambda a: jnp.transpose(a.reshape((n_seq,) + kv_shape + (a.shape[-1],)), (0, 4, 1, 2, 3))
            cmp_p.append(rows_last(kv_p[0]))
            slc_p.append(rows_last(kv_p[1]))
            win_p.append(rows_last(kv_p[2][:, :, t - min(NSA_WINDOW, t):]))
            cmp_s.append(kv_s[0].reshape((n_s, 1) + kv_shape))
            slc_s.append(kv_s[1].reshape((n_s, 1) + kv_shape))
            past = page_table.shape[1] * cache_nsa_cmp.shape[2]
            win = jnp.concatenate([cache_nsa_win[j], kv_s[2].reshape((n_s, 1) + kv_shape)], axis=1)
            win_s.append(win[:, win.shape[1] - min(NSA_WINDOW, past + 1):])
        else:
            assert t % SSM_CHUNK == 0
            tables, seg_tables = _ssm_tables(ssm_lambda_re[j], ssm_lambda_im[j], ssm_b_re[j], ssm_b_im[j],
                                             ssm_c_re[j], ssm_c_im[j], ssm_log_step[j], SSM_CHUNK // SSM_SEGMENTS)
            glu = (ssm_d[j], ssm_w_glu1[j].astype(BF16), ssm_b_glu1[j], ssm_w_glu2[j].astype(BF16), ssm_b_glu2[j])
            n_grp, n_st = ssm_lambda_re.shape[1:]
            xp, sr, si = _ssm_prompt(xp, mp, g, tables, seg_tables, *glu, n_seq, SSM_CHUNK)
            ssm_p.append(jnp.stack([sr.reshape(n_seq, n_grp, n_st), si.reshape(n_seq, n_grp, n_st)], axis=-1))
            h0 = state_ssm[j].reshape(n_s, n_grp * n_st, 2)
            xs, sr, si = _ssm_sample(xs, ms, g, tables, *glu, h0[..., 0], h0[..., 1])
            ssm_s.append(jnp.stack([sr.reshape(n_s, n_grp, n_st), si.reshape(n_s, n_grp, n_st)], axis=-1))
        ffn_w = (ffn_w_gate[i].astype(BF16), ffn_w_up[i].astype(BF16), ffn_w_down[i].astype(BF16))
        xp = _ffn(xp, mp, g, *ffn_w, tm)
        xs = _ffn(xs, ms, g, *ffn_w, n_s)
    return (xp.reshape(n_seq, t, d), xs.reshape(n_s, 1, d), jnp.stack(cmp_p), jnp.stack(cmp_s), jnp.stack(slc_p),
            jnp.stack(slc_s), jnp.stack(win_p), jnp.stack(win_s), jnp.stack(ssm_p), jnp.stack(ssm_s), jnp.stack(gv_s))
```
